```python
import math
import jax
import jax.numpy as jnp
from jax import lax
import numpy as np

D_MODEL = 1024
BATCH = 32
SEQ = 256
DEPTH = 4
DEC_BATCH = 8
DEC_SEQ = 4096
PAST_LEN = 512

GRID_W = 64
EPS = 1e-6
POS_BASE = 10000.0
NEG_BIG = -1e30
CHUNK = 64
H_CHUNK = 32

M_HEADS = 4
M_DH = 128
M_W = M_HEADS * M_DH
H_HEADS = 4
H_DK = 128
H_DV = 128
H_W = H_HEADS * H_DV
S_HEADS = 16
S_P = 64
S_W = S_HEADS * S_P
S_GROUPS = 4
S_N = 128
CONV_K = 3
CONV_CH = S_W + 2 * S_GROUPS * S_N
D_FF = ((8 * D_MODEL + 3 * 256 - 1) // (3 * 256)) * 256

IN_SIZES = (M_W, M_W, M_W, M_W, 2 * M_HEADS, 2 * M_HEADS,
            H_HEADS * H_DK, H_W, H_W, 2 * H_HEADS * H_DK,
            S_W, CONV_CH, 2 * S_HEADS,
            3 * D_MODEL)
N_IN = sum(IN_SIZES)
SPLIT_POINTS = tuple(int(s) for s in np.cumsum(IN_SIZES)[:-1])

kernel_name = "bidir_mlstm_hgrn2_ssd_diffusion_step"


def rmsnorm(x, w):
    xf = x.astype(jnp.float32)
    y = xf * lax.rsqrt(jnp.mean(xf * xf, axis=-1, keepdims=True) + EPS)
    return (y * w.astype(jnp.float32)).astype(x.dtype)


def head_rmsnorm(x, w):
    y = x * lax.rsqrt(jnp.mean(x * x, axis=-1, keepdims=True) + EPS)
    return y.reshape(x.shape[:2] + (-1,)) * w.astype(jnp.float32)


def grid_pos_embed(rows):
    quarter = D_MODEL // 4
    freq = POS_BASE ** (-jnp.arange(quarter, dtype=jnp.float32) / quarter)
    r = jnp.arange(rows, dtype=jnp.float32)[:, None] * freq
    cl = jnp.arange(GRID_W, dtype=jnp.float32)[:, None] * freq
    row_e = jnp.concatenate([jnp.sin(r), jnp.cos(r)], axis=-1)
    col_e = jnp.concatenate([jnp.sin(cl), jnp.cos(cl)], axis=-1)
    emb = jnp.concatenate([jnp.broadcast_to(row_e[:, None], (rows, GRID_W, D_MODEL // 2)),
                           jnp.broadcast_to(col_e[None], (rows, GRID_W, D_MODEL // 2))], axis=-1)
    return emb.reshape(rows * GRID_W, D_MODEL)


def dwconv(u, w, b):
    pad = CONV_K // 2
    y = lax.conv_general_dilated(u, w[:, None, :].astype(u.dtype), window_strides=(1,),
                                 padding=[(pad, pad)], dimension_numbers=("NWC", "WIO", "NWC"),
                                 feature_group_count=u.shape[-1])
    return y + b.astype(u.dtype)


def _chunks(a, c):
    b, l, hh = a.shape[:3]
    a = a.reshape((b, l // c, c, hh) + a.shape[3:])
    return jnp.moveaxis(a, (1, 3), (0, 2))


def _unchunk(a):
    a = jnp.moveaxis(a, (0, 2), (1, 3))
    return a.reshape((a.shape[0], a.shape[1] * a.shape[2]) + a.shape[3:])


def mlstm_scan(q, k, v, ig, lf, c0, n0, m0):
    causal = jnp.tril(jnp.ones((CHUNK, CHUNK), dtype=bool))

    def step(carry, inp):
        cmat, nvec, m = carry
        qc, kc, vc, ic, fc = inp
        b = jnp.cumsum(fc, axis=-1)
        dlog = jnp.where(causal, b[..., :, None] - b[..., None, :] + ic[..., None, :], NEG_BIG)
        inter = b + m[..., None]
        mt = jnp.maximum(inter, jnp.max(dlog, axis=-1))
        s = jnp.einsum("bhtd,bhsd->bhts", qc, kc) * jnp.exp(dlog - mt[..., None])
        wi = jnp.exp(inter - mt)
        num = jnp.einsum("bhts,bhse->bhte", s, vc) + wi[..., None] * jnp.einsum("bhtd,bhde->bhte", qc, cmat)
        den = jnp.sum(s, axis=-1) + wi * jnp.einsum("bhtd,bhd->bht", qc, nvec)
        hc = num / jnp.maximum(jnp.abs(den), jnp.exp(-mt))[..., None]
        m_end = mt[..., -1]
        wk = jnp.exp(b[..., -1:] - b + ic - m_end[..., None])
        carry_decay = jnp.exp(b[..., -1] + m - m_end)
        c_new = carry_decay[..., None, None] * cmat + jnp.einsum("bhs,bhsd,bhse->bhde", wk, kc, vc)
        n_new = carry_decay[..., None] * nvec + jnp.einsum("bhs,bhsd->bhd", wk, kc)
        return (c_new, n_new, m_end), hc

    xs = tuple(_chunks(a, CHUNK) for a in (q, k, v, ig, lf))
    (c_f, n_f, m_f), h = lax.scan(step, (c0, n0, m0), xs)
    return _unchunk(h), (c_f, n_f, m_f)


def hgrn_scan(q, k, v, logf, s0):
    causal = jnp.tril(jnp.ones((H_CHUNK, H_CHUNK), dtype=bool))[..., None]

    def step(s, inp):
        qc, kc, vc, gc = inp
        g = jnp.cumsum(gc, axis=-2)
        diff = jnp.where(causal, g[..., :, None, :] - g[..., None, :, :], 0.0)
        w = jnp.exp(diff) * causal
        att = jnp.einsum("bhtk,bhsk,bhtsk->bhts", qc, kc, w)
        o = jnp.einsum("bhts,bhsv->bhtv", att, vc) + jnp.einsum("bhtk,bhkv->bhtv", qc * jnp.exp(g), s)
        g_end = g[..., -1:, :]
        s_new = jnp.exp(g_end[..., 0, :])[..., None] * s + jnp.einsum("bhsk,bhsv->bhkv", kc * jnp.exp(g_end - g), vc)
        return s_new, o

    xs = tuple(_chunks(a, H_CHUNK) for a in (q, k, v, logf))
    s_fin, o = lax.scan(step, s0, xs)
    return _unchunk(o), (s_fin,)


def ssd_scan(x, dt, bm, cm, a, h0):
    causal = jnp.tril(jnp.ones((CHUNK, CHUNK), dtype=bool))
    n_rep = S_HEADS // S_GROUPS

    def step(h, inp):
        xc, dtc, bc, cc = inp
        bsz = xc.shape[0]
        la = jnp.cumsum(dtc * a[:, None], axis=-1)
        diff = jnp.where(causal, la[..., :, None] - la[..., None, :], 0.0)
        seg = jnp.exp(diff) * causal * dtc[..., None, :]
        seg = seg.reshape(bsz, S_GROUPS, n_rep, CHUNK, CHUNK)
        cb = jnp.einsum("bgtn,bgsn->bgts", cc, bc)
        xg = xc.reshape(bsz, S_GROUPS, n_rep, CHUNK, S_P)
        y_intra = jnp.einsum("bgrts,bgts,bgrsp->bgrtp", seg, cb, xg)
        hg = h.reshape(bsz, S_GROUPS, n_rep, S_P, S_N)
        y_inter = jnp.einsum("bgtn,bgrpn->bgrtp", cc, hg) * jnp.exp(la).reshape(bsz, S_GROUPS, n_rep, CHUNK, 1)
        y = (y_intra + y_inter).reshape(bsz, S_HEADS, CHUNK, S_P)
        la_end = la[..., -1]
        wk = (jnp.exp(la_end[..., None] - la) * dtc).reshape(bsz, S_GROUPS, n_rep, CHUNK)
        h_new = jnp.exp(la_end)[..., None, None] * h + jnp.einsum(
            "bgrs,bgrsp,bgsn->bgrpn", wk, xg, bc).reshape(bsz, S_HEADS, S_P, S_N)
        return h_new, y

    xs = (_chunks(x, CHUNK), _chunks(dt, CHUNK), _chunks(bm, CHUNK), _chunks(cm, CHUNK))
    h_fin, y = lax.scan(step, h0, xs)
    return _unchunk(y), (h_fin,)


def _bidir(scan_fn, seqs_f, seqs_b, init_f, init_b):
    y_f, st_f = scan_fn(*seqs_f, *init_f)
    y_b, st_b = scan_fn(*[jnp.flip(s, axis=1) for s in seqs_b], *init_b)
    return y_f + jnp.flip(y_b, axis=1), st_f, st_b


def mixer(h, lp, init, rows):
    f32 = jnp.float32
    bsz, L, _ = h.shape
    c0, n0, m0, s0, h0 = (s.astype(f32) for s in init)
    parts = jnp.split(h @ lp["w_in"], SPLIT_POINTS, axis=-1)
    mq, mk, mv, mo, mi, mf, hq, hi, hg, hf, sz, sxbc, sdt, bgate = (p.astype(f32) for p in parts)

    q = mq.reshape(bsz, L, M_HEADS, M_DH)
    k = mk.reshape(bsz, L, M_HEADS, M_DH) * (M_DH ** -0.5)
    v = mv.reshape(bsz, L, M_HEADS, M_DH)
    ig = mi.reshape(bsz, L, 2, M_HEADS) + lp["m_bi"]
    lf = jax.nn.log_sigmoid(mf.reshape(bsz, L, 2, M_HEADS) + lp["m_bf"])
    hm, mst_f, mst_b = _bidir(mlstm_scan,
                              (q, k, v, ig[:, :, 0], lf[:, :, 0]), (q, k, v, ig[:, :, 1], lf[:, :, 1]),
                              (c0[:, 0], n0[:, 0], m0[:, 0]), (c0[:, 1], n0[:, 1], m0[:, 1]))
    y_m = head_rmsnorm(hm, lp["m_norm"]) * jax.nn.sigmoid(mo)

    lb = lp["h_lb"].reshape(2, H_HEADS, H_DK)
    fgate = lb + (1.0 - lb) * jax.nn.sigmoid(hf.reshape(bsz, L, 2, H_HEADS, H_DK))
    logf = jnp.log(fgate)
    kk = 1.0 - fgate
    hq4 = hq.reshape(bsz, L, H_HEADS, H_DK)
    iv = hi.reshape(bsz, L, H_HEADS, H_DV)
    ho, hs_f, hs_b = _bidir(hgrn_scan,
                            (hq4, kk[:, :, 0], iv, logf[:, :, 0]), (hq4, kk[:, :, 1], iv, logf[:, :, 1]),
                            (s0[:, 0],), (s0[:, 1],))
    y_h = head_rmsnorm(ho, lp["h_norm"]) * jax.nn.silu(hg)

    if rows is None:
        u = dwconv(sxbc, lp["s_conv_w"], lp["s_conv_b"])
    else:
        u = dwconv(sxbc.reshape(bsz * rows, GRID_W, CONV_CH), lp["s_conv_w"], lp["s_conv_b"])
        u = u.reshape(bsz, L, CONV_CH)
    u = jax.nn.silu(u)
    xs, bm, cm = jnp.split(u, (S_W, S_W + S_GROUPS * S_N), axis=-1)
    xs = xs.reshape(bsz, L, S_HEADS, S_P)
    bm = bm.reshape(bsz, L, S_GROUPS, S_N)
    cm = cm.reshape(bsz, L, S_GROUPS, S_N)
    dt = jax.nn.softplus(sdt.reshape(bsz, L, 2, S_HEADS) + lp["s_dt_bias"])
    a = -jnp.exp(lp["s_a_log"].astype(f32))
    ys, ss_f, ss_b = _bidir(ssd_scan,
                            (xs, dt[:, :, 0], bm, cm), (xs, dt[:, :, 1], bm, cm),
                            (a[0], h0[:, 0]), (a[1], h0[:, 1]))
    ys = ys + lp["s_d"].astype(f32)[:, None] * xs
    y_s = rmsnorm(ys.reshape(bsz, L, S_W) * jax.nn.silu(sz), lp["s_norm"])

    g = jax.nn.sigmoid(bgate.reshape(bsz, L, 3, D_MODEL))
    merged = (g[:, :, 0] * (y_m @ lp["w_bm"]) + g[:, :, 1] * (y_h @ lp["w_bh"])
              + g[:, :, 2] * (y_s @ lp["w_bs"]))
    out = (merged @ lp["w_out"]).astype(h.dtype)
    if rows is not None:
        return out, None
    states = tuple(jnp.stack([sf, sb], axis=1).astype(h.dtype)
                   for sf, sb in zip(mst_f + hs_f + ss_f, mst_b + hs_b + ss_b))
    return out, states


def trunk_layer(x, cvec, lp, init, rows):
    mod = jax.nn.silu(cvec) @ lp["w_ada"] + lp["b_ada"]
    sh1, sc1, g1, sh2, sc2, g2 = jnp.split(mod[:, None, :], 6, axis=-1)
    h = rmsnorm(x, lp["norm1"]) * (1 + sc1) + sh1
    mix, states = mixer(h, lp, init, rows)
    x = x + g1 * mix
    h = rmsnorm(x, lp["norm2"]) * (1 + sc2) + sh2
    a, b = jnp.split(h @ lp["w_gu"], 2, axis=-1)
    x = x + g2 * ((jax.nn.silu(a) * b) @ lp["w_down"])
    return x, states


def setup_inputs(seed: int = 0) -> dict:
    key = jax.random.key(seed)
    ks = jax.random.split(key, 40)
    f32 = jnp.float32
    D = D_MODEL

    def nrm(k, shape, scale):
        return jax.random.normal(k, shape, f32) * scale

    dt0 = jnp.exp(jax.random.uniform(ks[20], (DEPTH, 2, S_HEADS), f32, math.log(1e-3), math.log(1e-1)))
    return {
        "x_prompt": nrm(ks[0], (BATCH, SEQ, D), 1.0),
        "x_sample": nrm(ks[1], (DEC_BATCH, DEC_SEQ, D), 1.0),
        "state_mlstm_c": nrm(ks[2], (DEC_BATCH, DEPTH, 2, M_HEADS, M_DH, M_DH), 0.1),
        "state_mlstm_n": nrm(ks[3], (DEC_BATCH, DEPTH, 2, M_HEADS, M_DH), 0.1),
        "state_mlstm_m": nrm(ks[4], (DEC_BATCH, DEPTH, 2, M_HEADS), 0.5),
        "state_hgrn": nrm(ks[5], (DEC_BATCH, DEPTH, 2, H_HEADS, H_DK, H_DV), 0.3),
        "state_ssm": nrm(ks[6], (DEC_BATCH, DEPTH, 2, S_HEADS, S_P, S_N), 0.1),
        "c": nrm(ks[7], (DEC_BATCH, D), 1.0),
        "c_ctx": nrm(ks[8], (D,), 1.0),
        "w_ada": nrm(ks[9], (DEPTH, D, 6 * D), 0.5 * D ** -0.5),
        "b_ada": nrm(ks[10], (DEPTH, 6 * D), 0.02),
        "norm1": 1.0 + nrm(ks[11], (DEPTH, D), 0.02),
        "norm2": 1.0 + nrm(ks[12], (DEPTH, D), 0.02),
        "w_in": nrm(ks[13], (DEPTH, D, N_IN), D ** -0.5),
        "m_bi": nrm(ks[14], (DEPTH, 2, M_HEADS), 0.1),
        "m_bf": 3.0 + nrm(ks[15], (DEPTH, 2, M_HEADS), 0.5),
        "m_norm": 1.0 + nrm(ks[16], (DEPTH, M_W), 0.02),
        "h_lb": nrm(ks[17], (2, DEPTH, H_HEADS * H_DK), 1.0),
        "h_norm": 1.0 + nrm(ks[18], (DEPTH, H_W), 0.02),
        "s_conv_w": nrm(ks[19], (DEPTH, CONV_K, CONV_CH), CONV_K ** -0.5),
        "s_conv_b": nrm(ks[21], (DEPTH, CONV_CH), 0.02),
        "s_dt_bias": dt0 + jnp.log(-jnp.expm1(-dt0)),
        "s_a_log": jnp.log(jax.random.uniform(ks[22], (DEPTH, 2, S_HEADS), f32, 1.0, 16.0)),
        "s_d": 1.0 + nrm(ks[23], (DEPTH, S_HEADS), 0.1),
        "s_norm": 1.0 + nrm(ks[24], (DEPTH, S_W), 0.02),
        "w_bm": nrm(ks[25], (DEPTH, M_W, D), M_W ** -0.5),
        "w_bh": nrm(ks[26], (DEPTH, H_W, D), H_W ** -0.5),
        "w_bs": nrm(ks[27], (DEPTH, S_W, D), S_W ** -0.5),
        "w_out": nrm(ks[28], (DEPTH, D, D), D ** -0.5),
        "w_gu": nrm(ks[29], (DEPTH, D, 2 * D_FF), D ** -0.5),
        "w_down": nrm(ks[30], (DEPTH, D_FF, D), D_FF ** -0.5),
        "norm_f": 1.0 + nrm(ks[31], (D,), 0.02),
    }


def reference(x_prompt, x_sample, state_mlstm_c, state_mlstm_n, state_mlstm_m, state_hgrn, state_ssm,
              c, c_ctx, w_ada, b_ada, norm1, norm2, w_in, m_bi, m_bf, m_norm, h_lb, h_norm,
              s_conv_w, s_conv_b, s_dt_bias, s_a_log, s_d, s_norm, w_bm, w_bh, w_bs, w_out,
              w_gu, w_down, norm_f):
    f32 = jnp.float32
    p = jax.nn.softmax(h_lb.astype(f32), axis=1)
    lower = jnp.cumsum(p, axis=1) - p[:, :1]

    def layer_params(i):
        return {"w_ada": w_ada[i], "b_ada": b_ada[i], "norm1": norm1[i], "norm2": norm2[i],
                "w_in": w_in[i], "m_bi": m_bi[i], "m_bf": m_bf[i], "m_norm": m_norm[i],
                "h_lb": lower[:, i], "h_norm": h_norm[i], "s_conv_w": s_conv_w[i], "s_conv_b": s_conv_b[i],
                "s_dt_bias": s_dt_bias[i], "s_a_log": s_a_log[i], "s_d": s_d[i], "s_norm": s_norm[i],
                "w_bm": w_bm[i], "w_bh": w_bh[i], "w_bs": w_bs[i], "w_out": w_out[i],
                "w_gu": w_gu[i], "w_down": w_down[i]}

    bp = x_prompt.shape[0]
    zero_init = (jnp.zeros((bp, 2, M_HEADS, M_DH, M_DH), f32), jnp.zeros((bp, 2, M_HEADS, M_DH), f32),
                 jnp.zeros((bp, 2, M_HEADS), f32), jnp.zeros((bp, 2, H_HEADS, H_DK, H_DV), f32),
                 jnp.zeros((bp, 2, S_HEADS, S_P, S_N), f32))
    xp = x_prompt
    ctx_states = []
    for i in range(DEPTH):
        xp, st = trunk_layer(xp, c_ctx[None, :], layer_params(i), zero_init, None)
        ctx_states.append(st)
    y_prompt = rmsnorm(xp, norm_f)
    new_mlstm_c = jnp.stack([st[0] for st in ctx_states], axis=1)
    new_mlstm_n = jnp.stack([st[1] for st in ctx_states], axis=1)
    new_mlstm_m = jnp.stack([st[2] for st in ctx_states], axis=1)
    new_hgrn = jnp.stack([st[3] for st in ctx_states], axis=1)
    new_ssm = jnp.stack([st[4] for st in ctx_states], axis=1)

    rows = x_sample.shape[1] // GRID_W
    xs = x_sample + grid_pos_embed(rows).astype(x_sample.dtype)
    for i in range(DEPTH):
        cached = (state_mlstm_c[:, i], state_mlstm_n[:, i], state_mlstm_m[:, i], state_hgrn[:, i], state_ssm[:, i])
        xs, _ = trunk_layer(xs, c, layer_params(i), cached, rows)
    y_sample = rmsnorm(xs, norm_f)

    return (y_prompt, y_sample, new_mlstm_c, new_mlstm_n, new_mlstm_m, new_hgrn, new_ssm)
```

```python
import functools

import numpy as np
import jax
import jax.numpy as jnp
from jax import lax
from jax.experimental import pallas as pl
from jax.experimental.pallas import tpu as pltpu

F32 = jnp.float32
BF16 = jnp.bfloat16

GRID_W = 64
EPS = 1e-6
POS_BASE = 10000.0
NEG_BIG = -1e30
M_HEADS, M_DH = 4, 128
H_HEADS, H_DK, H_DV = 4, 128, 128
S_HEADS, S_P, S_GROUPS, S_N = 16, 64, 4, 128
M_W = M_HEADS * M_DH
H_W = H_HEADS * H_DV
S_W = S_HEADS * S_P
CONV_CH = S_W + 2 * S_GROUPS * S_N

LANES = 128
VMEM_LIMIT = 56 * 1024 * 1024
SCAN_CHUNK = 128
SCAN_BLOCK = 256
PROJ_TN = 1536

OFF_XBC, OFF_SZ, OFF_BG, OFF_M, OFF_H = 0, 2048, 3072, 6144, 8192
N_MAIN = 10752
GC_IG, GC_LF, GC_DT, GC_DTA = 0, 8, 16, 48


def _cparams(sem):
    return pltpu.CompilerParams(dimension_semantics=sem, vmem_limit_bytes=VMEM_LIMIT)


def _sigmoid(x):
    return 1.0 / (1.0 + jnp.exp(-x))


def _silu(x):
    return x * _sigmoid(x)


def _softplus(x):
    return jnp.maximum(x, 0.0) + jnp.log(1.0 + jnp.exp(-jnp.abs(x)))


def _dot(a, b):
    return jnp.dot(a.astype(BF16), b.astype(BF16), preferred_element_type=F32)


def _dot_nt(a, b):
    return lax.dot_general(a.astype(BF16), b.astype(BF16), (((1,), (1,)), ((), ())),
                           preferred_element_type=F32)


def _dot_tn(a, b):
    return lax.dot_general(a.astype(BF16), b.astype(BF16), (((0,), (0,)), ((), ())),
                           preferred_element_type=F32)


def _split3(x):
    x1 = x.astype(BF16)
    r1 = x - x1.astype(F32)
    x2 = r1.astype(BF16)
    x3 = (r1 - x2.astype(F32)).astype(BF16)
    return x1, x2, x3


def _sum01(sel, x):
    x1, x2, x3 = _split3(x)
    d = functools.partial(jnp.dot, preferred_element_type=F32)
    return d(sel, x1) + d(sel, x2) + d(sel, x3)


def _sum01_nt(x, sel):
    x1, x2, x3 = _split3(x)
    d = lambda a: lax.dot_general(a, sel, (((1,), (1,)), ((), ())), preferred_element_type=F32)
    return d(x1) + d(x2) + d(x3)


class _Cfg:
    def __init__(self, n_ctx, l_ctx, n_smp, l_smp, d_model, d_ff):
        self.n_ctx, self.l_ctx, self.n_smp, self.l_smp = n_ctx, l_ctx, n_smp, l_smp
        self.d, self.d_ff = d_model, d_ff
        self.t_ctx = n_ctx * l_ctx
        self.t_smp = n_smp * l_smp
        self.t = self.t_ctx + self.t_smp
        self.tb = SCAN_BLOCK
        self.c = SCAN_CHUNK
        assert l_ctx % self.tb == 0 and l_smp % self.tb == 0 and self.tb % self.c == 0
        assert l_ctx & (l_ctx - 1) == 0 and l_smp % GRID_W == 0
        self.nbc = l_ctx // self.tb
        self.nbs = l_smp // self.tb
        self.gc0 = self.t_ctx // self.tb
        self.nb = self.t // self.tb
        self.tm = 1024 if (self.t_ctx % 1024 == 0 and self.t_smp % 1024 == 0 and l_smp % 1024 == 0) else 256
        assert self.t_ctx % self.tm == 0 and l_smp % self.tm == 0 and self.tm % l_ctx == 0
        self.tm_merge = 256
        self.tm_ffn = 512 if self.tm == 1024 else 256

    def mod_row(self, tm):
        nct = self.t_ctx // tm
        per = self.l_smp // tm

        def f(m):
            return jnp.where(m < nct, self.n_smp, jnp.maximum(m - nct, 0) // per)
        return f


def _scan_pos(cfg, d, i):
    g = i + d * (cfg.nb - 1 - 2 * i)
    is_ctx = g < cfg.gc0
    gs = jnp.maximum(g - cfg.gc0, 0)
    pos = jnp.where(is_ctx, lax.rem(g, cfg.nbc), lax.rem(gs, cfg.nbs))
    nblk = jnp.where(is_ctx, cfg.nbc, cfg.nbs)
    spos = pos + d * (nblk - 1 - 2 * pos)
    return g, is_ctx, spos == 0, spos == nblk - 1


def _blk(cfg, d, i):
    return i + d * (cfg.nb - 1 - 2 * i)


def _smp_seq(cfg, d, i):
    return jnp.clip((_blk(cfg, d, i) - cfg.gc0) // cfg.nbs, 0, cfg.n_smp - 1)


def _ctx_seq(cfg, d, i):
    return jnp.minimum(_blk(cfg, d, i) // cfg.nbc, cfg.n_ctx - 1)


def _tri_consts(c):
    t = np.arange(c)
    fwd = (t[None, :] <= t[:, None]).astype(np.float32)
    return jnp.asarray(np.stack([fwd, fwd.T]), dtype=BF16)


def _level_consts(c):
    levels = int(np.log2(c))
    p = np.arange(c)
    mats = []
    for lev in range(levels):
        half = 1 << lev
        same = (p[:, None] // half) == (p[None, :] // half)
        upper = ((p & half) != 0)[:, None]
        m = np.where(upper, same & (p[None, :] <= p[:, None]), same & (p[None, :] > p[:, None]))
        mats.append(m.astype(np.float32))
    mats.append((p[None, :] <= p[:, None]).astype(np.float32))
    fwd = np.concatenate(mats, axis=0)
    bwd = np.concatenate([m[::-1, ::-1] for m in mats], axis=0)
    return jnp.asarray(np.stack([fwd, bwd]), dtype=BF16), levels


def _embed_kernel(xp_ref, xs_ref, pos_ref, o_ref, *, nct):
    m = pl.program_id(0)

    @pl.when(m < nct)
    def _():
        o_ref[...] = xp_ref[...]

    @pl.when(m >= nct)
    def _():
        o_ref[...] = xs_ref[...] + pos_ref[...]


def _embed(cfg, xp2, xs2, pos):
    tm = cfg.tm
    nct = cfg.t_ctx // tm
    per = cfg.l_smp // tm
    return pl.pallas_call(
        functools.partial(_embed_kernel, nct=nct),
        grid=(cfg.t // tm,),
        in_specs=[pl.BlockSpec((tm, cfg.d), lambda m: (jnp.minimum(m, nct - 1), 0)),
                  pl.BlockSpec((tm, cfg.d), lambda m: (jnp.maximum(m - nct, 0), 0)),
                  pl.BlockSpec((tm, cfg.d), lambda m: (lax.rem(jnp.maximum(m - nct, 0), per), 0))],
        out_specs=pl.BlockSpec((tm, cfg.d), lambda m: (m, 0)),
        out_shape=jax.ShapeDtypeStruct((cfg.t, cfg.d), F32),
        compiler_params=_cparams(("arbitrary",)),
        name="embed",
    )(xp2, xs2, pos)


def _ada_kernel(c_ref, w_ref, b_ref, o_ref):
    o_ref[0] = _dot(_silu(c_ref[...]), w_ref[0]) + b_ref[0]


def _ada(cvec, w_ada, b_ada):
    depth, d, n = w_ada.shape
    rows = cvec.shape[0]
    tn = 1536
    return pl.pallas_call(
        _ada_kernel,
        grid=(depth, n // tn),
        in_specs=[pl.BlockSpec((rows, d), lambda l, j: (0, 0)),
                  pl.BlockSpec((1, d, tn), lambda l, j: (l, 0, j)),
                  pl.BlockSpec((1, 1, tn), lambda l, j: (l, 0, j))],
        out_specs=pl.BlockSpec((1, rows, tn), lambda l, j: (l, 0, j)),
        out_shape=jax.ShapeDtypeStruct((depth, rows, n), F32),
        compiler_params=_cparams(("arbitrary", "arbitrary")),
        name="ada",
    )(cvec, w_ada, b_ada.reshape(depth, 1, n))


def _inproj_kernel(x_ref, sc_ref, sh_ref, nw_ref, w_ref, ws_ref, gb_ref, ga_ref, cw_ref, cb_ref,
                   p_ref, g_ref, gt_ref, xn_ref, *, tm, tn, nct, seg_ctx):
    m = pl.program_id(0)
    n = pl.program_id(1)

    @pl.when(n == 0)
    def _():
        x = x_ref[...]
        y = x * lax.rsqrt(jnp.mean(x * x, axis=-1, keepdims=True) + EPS) * nw_ref[...]
        h = (y * (1.0 + sc_ref[0]) + sh_ref[0]).astype(BF16)
        xn_ref[...] = h
        raw = jnp.dot(h, ws_ref[...], preferred_element_type=F32) + gb_ref[...]
        lane = lax.broadcasted_iota(jnp.int32, raw.shape, 1)
        sp = _softplus(raw)
        g = jnp.where(lane < GC_LF, raw,
                      jnp.where(lane < GC_DT, -_softplus(-raw),
                                jnp.where(lane < GC_DTA, sp, sp * ga_ref[...])))
        g_ref[...] = g
        gt_ref[...] = g.T

    acc = jnp.dot(xn_ref[...], w_ref[...], preferred_element_type=F32)

    n_conv_full = CONV_CH // tn
    conv_rem = CONV_CH - n_conv_full * tn

    def conv_cols(ncols):
        row = lax.broadcasted_iota(jnp.int32, (tm, LANES), 0)
        seg_mask = jnp.where(m < nct, seg_ctx - 1, GRID_W - 1)
        inseg = row & seg_mask
        has_prev = inseg != 0
        has_next = inseg != seg_mask
        for cb in range(ncols // LANES):
            sl = slice(cb * LANES, (cb + 1) * LANES)
            u = acc[:, sl]
            up = jnp.where(has_prev, pltpu.roll(u, 1, 0), 0.0)
            un = jnp.where(has_next, pltpu.roll(u, tm - 1, 0), 0.0)
            cw = cw_ref[0]
            v = cw[0:1, sl] * up + cw[1:2, sl] * u + cw[2:3, sl] * un + cb_ref[0][:, sl]
            p_ref[:, sl] = _silu(v)

    @pl.when(n < n_conv_full)
    def _():
        conv_cols(tn)

    if conv_rem:
        @pl.when(n == n_conv_full)
        def _():
            conv_cols(conv_rem)
            p_ref[:, conv_rem:] = acc[:, conv_rem:]

    @pl.when(n > (n_conv_full if conv_rem else n_conv_full - 1))
    def _():
        p_ref[...] = acc


def _inproj(cfg, x, mod, nw, w_main, w_small, gbias, gmul, cw, cb):
    tm, tn = cfg.tm, PROJ_TN
    nct = cfg.t_ctx // tm
    row = cfg.mod_row(tm)
    nrow = cfg.n_smp + 1
    rows_pad = mod.shape[0] // 6
    nt = N_MAIN // tn
    kern = functools.partial(_inproj_kernel, tm=tm, tn=tn, nct=nct, seg_ctx=cfg.l_ctx)
    del nrow
    return pl.pallas_call(
        kern,
        grid=(cfg.t // tm, nt),
        in_specs=[pl.BlockSpec((tm, cfg.d), lambda m, n: (m, 0)),
                  pl.BlockSpec((1, 1, cfg.d), lambda m, n: (1 * rows_pad + row(m), 0, 0)),
                  pl.BlockSpec((1, 1, cfg.d), lambda m, n: (0 * rows_pad + row(m), 0, 0)),
                  pl.BlockSpec((1, cfg.d), lambda m, n: (0, 0)),
                  pl.BlockSpec((cfg.d, tn), lambda m, n: (0, n)),
                  pl.BlockSpec((cfg.d, LANES), lambda m, n: (0, 0)),
                  pl.BlockSpec((1, LANES), lambda m, n: (0, 0)),
                  pl.BlockSpec((1, LANES), lambda m, n: (0, 0)),
                  pl.BlockSpec((1, 3, tn), lambda m, n: (jnp.minimum(n, 1), 0, 0)),
                  pl.BlockSpec((1, 1, tn), lambda m, n: (jnp.minimum(n, 1), 0, 0))],
        out_specs=[pl.BlockSpec((tm, tn), lambda m, n: (m, n)),
                   pl.BlockSpec((tm, LANES), lambda m, n: (m, 0)),
                   pl.BlockSpec((LANES, tm), lambda m, n: (0, m))],
        out_shape=[jax.ShapeDtypeStruct((cfg.t, N_MAIN), F32),
                   jax.ShapeDtypeStruct((cfg.t, LANES), F32),
                   jax.ShapeDtypeStruct((LANES, cfg.t), F32)],
        scratch_shapes=[pltpu.VMEM((tm, cfg.d), BF16)],
        compiler_params=_cparams(("arbitrary", "arbitrary")),
        name="inproj",
    )(x, mod, mod, nw, w_main, w_small, gbias, gmul, cw, cb)


def _mlstm_kernel(q_ref, k_ref, v_ref, g_ref, gt_ref, tri_ref, c0_ref, m0_ref,
                  y_ref, cout_ref, mout_ref, c_sc, m_sc, *, cfg):
    c, tb = cfg.c, cfg.tb
    d = pl.program_id(0)
    i = pl.program_id(1)
    _, is_ctx, first, last = _scan_pos(cfg, d, i)

    @pl.when(jnp.logical_and(first, is_ctx))
    def _():
        c_sc[...] = jnp.zeros_like(c_sc)
        m_sc[...] = jnp.zeros_like(m_sc)

    @pl.when(jnp.logical_and(first, jnp.logical_not(is_ctx)))
    def _():
        c_sc[...] = c0_ref[...]
        m_sc[...] = m0_ref[...]

    def run(rev):
        dd = 1 if rev else 0
        sel = tri_ref[dd]
        tpos = lax.broadcasted_iota(jnp.int32, (c, c), 0)
        spos = lax.broadcasted_iota(jnp.int32, (c, c), 1)
        mask = (spos >= tpos) if rev else (spos <= tpos)
        unit = (lax.broadcasted_iota(jnp.int32, (c, LANES), 1) == 0).astype(F32)
        end = 0 if rev else c - 1
        nch = tb // c
        for ci in range(nch):
            r0 = (nch - 1 - ci if rev else ci) * c
            gc = g_ref[r0:r0 + c, :]
            gtc = gt_ref[:, r0:r0 + c]
            bcol = _sum01(sel, gc)
            brow = _sum01_nt(gtc[GC_LF:GC_LF + 8], sel)
            for h in range(M_HEADS):
                col = dd * M_HEADS + h
                hs = slice(h * M_DH, (h + 1) * M_DH)
                q = q_ref[r0:r0 + c, hs]
                k = k_ref[r0:r0 + c, hs] * (M_DH ** -0.5)
                vaug = jnp.concatenate([v_ref[r0:r0 + c, hs], unit], axis=1)
                b_c = bcol[:, GC_LF + col:GC_LF + col + 1]
                b_r = brow[col:col + 1, :]
                i_r = gtc[GC_IG + col:GC_IG + col + 1, :]
                i_c = gc[:, GC_IG + col:GC_IG + col + 1]
                m_prev = m_sc[h][:, 0:1]
                dlog = jnp.where(mask, b_c - b_r + i_r, NEG_BIG)
                inter = b_c + m_prev
                mt = jnp.maximum(inter, jnp.max(dlog, axis=-1, keepdims=True))
                s = _dot_nt(q, k) * jnp.exp(dlog - mt)
                wi = jnp.exp(inter - mt)
                caug = c_sc[h]
                numden = _dot(s, vaug) + wi * _dot(q, caug)
                den = numden[:, M_DH:M_DH + 1]
                hc = numden[:, :M_DH] / jnp.maximum(jnp.abs(den), jnp.exp(-mt))
                y_ref[r0:r0 + c, hs] = hc
                m_end = mt[end:end + 1, :]
                b_end = b_c[end:end + 1, :]
                wk = jnp.exp(b_end - b_c + i_c - m_end)
                carry = jnp.exp(b_end + m_prev - m_end)
                c_sc[h] = carry * caug + _dot_tn(k * wk, vaug)
                m_sc[h] = jnp.broadcast_to(m_end, (1, LANES))

    @pl.when(d == 0)
    def _():
        run(False)

    @pl.when(d == 1)
    def _():
        run(True)

    @pl.when(jnp.logical_and(last, is_ctx))
    def _():
        cout_ref[...] = c_sc[...]
        mout_ref[...] = m_sc[...]


def _mlstm(cfg, p, g, gt, tri, c0, m0):
    tb = cfg.tb
    blk = functools.partial(_blk, cfg)
    smp = functools.partial(_smp_seq, cfg)
    ctx = functools.partial(_ctx_seq, cfg)
    qi = OFF_M // M_W
    st = (None, None, M_HEADS, M_DH, 2 * M_DH)
    ms = (None, None, M_HEADS, 1, LANES)
    return pl.pallas_call(
        functools.partial(_mlstm_kernel, cfg=cfg),
        grid=(2, cfg.nb),
        in_specs=[pl.BlockSpec((tb, M_W), lambda d, i: (blk(d, i), qi)),
                  pl.BlockSpec((tb, M_W), lambda d, i: (blk(d, i), qi + 1)),
                  pl.BlockSpec((tb, M_W), lambda d, i: (blk(d, i), qi + 2)),
                  pl.BlockSpec((tb, LANES), lambda d, i: (blk(d, i), 0)),
                  pl.BlockSpec((LANES, tb), lambda d, i: (0, blk(d, i))),
                  pl.BlockSpec((2, cfg.c, cfg.c), lambda d, i: (0, 0, 0)),
                  pl.BlockSpec(st, lambda d, i: (smp(d, i), d, 0, 0, 0)),
                  pl.BlockSpec(ms, lambda d, i: (smp(d, i), d, 0, 0, 0))],
        out_specs=[pl.BlockSpec((None, tb, M_W), lambda d, i: (d, blk(d, i), 0)),
                   pl.BlockSpec(st, lambda d, i: (ctx(d, i), d, 0, 0, 0)),
                   pl.BlockSpec(ms, lambda d, i: (ctx(d, i), d, 0, 0, 0))],
        out_shape=[jax.ShapeDtypeStruct((2, cfg.t, M_W), F32),
                   jax.ShapeDtypeStruct((cfg.n_ctx, 2, M_HEADS, M_DH, 2 * M_DH), F32),
                   jax.ShapeDtypeStruct((cfg.n_ctx, 2, M_HEADS, 1, LANES), F32)],
        scratch_shapes=[pltpu.VMEM((M_HEADS, M_DH, 2 * M_DH), F32),
                        pltpu.VMEM((M_HEADS, 1, LANES), F32)],
        compiler_params=_cparams(("arbitrary", "arbitrary")),
        name="mlstm",
    )(p, p, p, g, gt, tri, c0, m0)


def _hgrn_kernel(q_ref, v_ref, f_ref, lb_ref, lvl_ref, s0_ref, y_ref, sout_ref, s_sc, *, cfg, levels):
    c, tb = cfg.c, cfg.tb
    d = pl.program_id(0)
    i = pl.program_id(1)
    _, is_ctx, first, last = _scan_pos(cfg, d, i)

    @pl.when(jnp.logical_and(first, is_ctx))
    def _():
        s_sc[...] = jnp.zeros_like(s_sc)

    @pl.when(jnp.logical_and(first, jnp.logical_not(is_ctx)))
    def _():
        for h in range(H_HEADS):
            s_sc[h] = s0_ref[h].T

    def run(rev):
        dd = 1 if rev else 0
        sel = lvl_ref[dd]
        tpos = lax.broadcasted_iota(jnp.int32, (c, c), 0)
        spos = lax.broadcasted_iota(jnp.int32, (c, c), 1)
        before = (spos > tpos) if rev else (spos < tpos)
        pair = jnp.where(before, tpos ^ spos, 0)
        ones = jnp.ones((H_DK, LANES), BF16)
        end = 0 if rev else c - 1
        nch = tb // c
        lb = lb_ref[0]
        for ci in range(nch):
            r0 = (nch - 1 - ci if rev else ci) * c
            f = lb + (1.0 - lb) * _sigmoid(f_ref[r0:r0 + c, :])
            logf = jnp.log(f)
            kk_all = 1.0 - f
            e_all = _sum01(sel, logf)
            for h in range(H_HEADS):
                hs = slice(h * H_DK, (h + 1) * H_DK)
                q = q_ref[r0:r0 + c, hs]
                v = v_ref[r0:r0 + c, hs]
                kk = kk_all[:, hs]
                att = jnp.zeros((c, c), F32)
                for lev in range(levels):
                    x = jnp.exp(e_all[lev * c:(lev + 1) * c, hs])
                    a = _dot_nt(q * x, kk * x)
                    att = jnp.where((pair >> lev) == 1, a, att)
                gcum = e_all[levels * c:(levels + 1) * c, hs]
                diag = jnp.dot((q * kk).astype(BF16), ones, preferred_element_type=F32)
                st = s_sc[h]
                o = _dot(att, v) + diag * v + _dot_nt(q * jnp.exp(gcum), st)
                y_ref[r0:r0 + c, hs] = o
                g_end = gcum[end:end + 1, :]
                s_sc[h] = jnp.exp(g_end) * st + _dot_tn(v, kk * jnp.exp(g_end - gcum))

    @pl.when(d == 0)
    def _():
        run(False)

    @pl.when(d == 1)
    def _():
        run(True)

    @pl.when(jnp.logical_and(last, is_ctx))
    def _():
        for h in range(H_HEADS):
            sout_ref[h] = s_sc[h].T


def _hgrn(cfg, p, lb, lvl, levels, s0):
    tb = cfg.tb
    blk = functools.partial(_blk, cfg)
    smp = functools.partial(_smp_seq, cfg)
    ctx = functools.partial(_ctx_seq, cfg)
    qi = OFF_H // H_W
    st = (None, None, H_HEADS, H_DK, H_DV)
    return pl.pallas_call(
        functools.partial(_hgrn_kernel, cfg=cfg, levels=levels),
        grid=(2, cfg.nb),
        in_specs=[pl.BlockSpec((tb, H_W), lambda d, i: (blk(d, i), qi)),
                  pl.BlockSpec((tb, H_W), lambda d, i: (blk(d, i), qi + 1)),
                  pl.BlockSpec((tb, H_W), lambda d, i: (blk(d, i), qi + 3 + d)),
                  pl.BlockSpec((1, 1, H_W), lambda d, i: (d, 0, 0)),
                  pl.BlockSpec(lvl.shape, lambda d, i: (0, 0, 0)),
                  pl.BlockSpec(st, lambda d, i: (smp(d, i), d, 0, 0, 0))],
        out_specs=[pl.BlockSpec((None, tb, H_W), lambda d, i: (d, blk(d, i), 0)),
                   pl.BlockSpec(st, lambda d, i: (ctx(d, i), d, 0, 0, 0))],
        out_shape=[jax.ShapeDtypeStruct((2, cfg.t, H_W), F32),
                   jax.ShapeDtypeStruct((cfg.n_ctx, 2, H_HEADS, H_DK, H_DV), F32)],
        scratch_shapes=[pltpu.VMEM((H_HEADS, H_DV, H_DK), F32)],
        compiler_params=_cparams(("arbitrary", "arbitrary")),
        name="hgrn",
    )(p, p, p, lb, lvl, s0)


def _ssd_kernel(x_ref, b_ref, cm_ref, g_ref, gt_ref, tri_ref, sd_ref, h0_ref, y_ref, hout_ref, h_sc, *, cfg):
    c, tb = cfg.c, cfg.tb
    d = pl.program_id(0)
    i = pl.program_id(1)
    _, is_ctx, first, last = _scan_pos(cfg, d, i)
    n_rep = S_HEADS // S_GROUPS

    @pl.when(jnp.logical_and(first, is_ctx))
    def _():
        h_sc[...] = jnp.zeros_like(h_sc)

    @pl.when(jnp.logical_and(first, jnp.logical_not(is_ctx)))
    def _():
        h_sc[...] = h0_ref[...]

    def run(rev):
        dd = 1 if rev else 0
        sel = tri_ref[dd]
        tpos = lax.broadcasted_iota(jnp.int32, (c, c), 0)
        spos = lax.broadcasted_iota(jnp.int32, (c, c), 1)
        mask = (spos >= tpos) if rev else (spos <= tpos)
        end = 0 if rev else c - 1
        nch = tb // c
        for ci in range(nch):
            r0 = (nch - 1 - ci if rev else ci) * c
            gc = g_ref[r0:r0 + c, :]
            gtc = gt_ref[:, r0:r0 + c]
            lacol = _sum01(sel, gc)
            larow = _sum01_nt(gtc[GC_DTA + dd * S_HEADS:GC_DTA + (dd + 1) * S_HEADS], sel)
            for grp in range(S_GROUPS):
                ns = slice(grp * S_N, (grp + 1) * S_N)
                bm = b_ref[r0:r0 + c, ns]
                cm = cm_ref[r0:r0 + c, ns]
                cb = _dot_nt(cm, bm)
                for r in range(n_rep):
                    h = grp * n_rep + r
                    ps = slice(h * S_P, (h + 1) * S_P)
                    xh = x_ref[r0:r0 + c, ps]
                    la_c = lacol[:, GC_DTA + dd * S_HEADS + h:GC_DTA + dd * S_HEADS + h + 1]
                    la_r = larow[h:h + 1, :]
                    dt_r = gtc[GC_DT + dd * S_HEADS + h:GC_DT + dd * S_HEADS + h + 1, :]
                    dt_c = gc[:, GC_DT + dd * S_HEADS + h:GC_DT + dd * S_HEADS + h + 1]
                    seg = jnp.where(mask, jnp.exp(jnp.where(mask, la_c - la_r, 0.0)), 0.0) * dt_r
                    hst = h_sc[h]
                    y = _dot(seg * cb, xh) + jnp.exp(la_c) * _dot_nt(cm, hst)
                    if not rev:
                        y = y + sd_ref[:, ps] * xh
                    y_ref[r0:r0 + c, ps] = y
                    la_end = la_c[end:end + 1, :]
                    wk = jnp.exp(la_end - la_c) * dt_c
                    h_sc[h] = jnp.exp(la_end) * hst + _dot_tn(xh * wk, bm)

    @pl.when(d == 0)
    def _():
        run(False)

    @pl.when(d == 1)
    def _():
        run(True)

    @pl.when(jnp.logical_and(last, is_ctx))
    def _():
        hout_ref[...] = h_sc[...]


def _ssd(cfg, p, g, gt, tri, sd, h0):
    tb = cfg.tb
    blk = functools.partial(_blk, cfg)
    smp = functools.partial(_smp_seq, cfg)
    ctx = functools.partial(_ctx_seq, cfg)
    gn = S_GROUPS * S_N
    st = (None, None, S_HEADS, S_P, S_N)
    return pl.pallas_call(
        functools.partial(_ssd_kernel, cfg=cfg),
        grid=(2, cfg.nb),
        in_specs=[pl.BlockSpec((tb, S_W), lambda d, i: (blk(d, i), OFF_XBC // S_W)),
                  pl.BlockSpec((tb, gn), lambda d, i: (blk(d, i), (OFF_XBC + S_W) // gn)),
                  pl.BlockSpec((tb, gn), lambda d, i: (blk(d, i), (OFF_XBC + S_W) // gn + 1)),
                  pl.BlockSpec((tb, LANES), lambda d, i: (blk(d, i), 0)),
                  pl.BlockSpec((LANES, tb), lambda d, i: (0, blk(d, i))),
                  pl.BlockSpec((2, cfg.c, cfg.c), lambda d, i: (0, 0, 0)),
                  pl.BlockSpec((1, S_W), lambda d, i: (0, 0)),
                  pl.BlockSpec(st, lambda d, i: (smp(d, i), d, 0, 0, 0))],
        out_specs=[pl.BlockSpec((None, tb, S_W), lambda d, i: (d, blk(d, i), 0)),
                   pl.BlockSpec(st, lambda d, i: (ctx(d, i), d, 0, 0, 0))],
        out_shape=[jax.ShapeDtypeStruct((2, cfg.t, S_W), F32),
                   jax.ShapeDtypeStruct((cfg.n_ctx, 2, S_HEADS, S_P, S_N), F32)],
        scratch_shapes=[pltpu.VMEM((S_HEADS, S_P, S_N), F32)],
        compiler_params=_cparams(("arbitrary", "arbitrary")),
        name="ssd",
    )(p, p, p, g, gt, tri, sd, h0)


def _head_norm(y, heads, width):
    outs = []
    for h in range(heads):
        s = y[:, h * width:(h + 1) * width]
        outs.append(s * lax.rsqrt(jnp.mean(s * s, axis=-1, keepdims=True) + EPS))
    return jnp.concatenate(outs, axis=1)


def _merge_kernel(x_ref, gate_ref, mo_ref, hg_ref, sz_ref, bg_ref, ym_ref, yh_ref, ys_ref,
                  mn_ref, hn_ref, sn_ref, wbm_ref, wbh_ref, wbs_ref, wo_ref, o_ref, *, d_model):
    y_m = _head_norm(ym_ref[0] + ym_ref[1], M_HEADS, M_DH) * mn_ref[...] * _sigmoid(mo_ref[...])
    y_h = _head_norm(yh_ref[0] + yh_ref[1], H_HEADS, H_DV) * hn_ref[...] * _silu(hg_ref[...])
    ys = (ys_ref[0] + ys_ref[1]) * _silu(sz_ref[...])
    y_s = ys * lax.rsqrt(jnp.mean(ys * ys, axis=-1, keepdims=True) + EPS) * sn_ref[...]
    bg = bg_ref[...]
    dm = d_model
    merged = (_sigmoid(bg[:, :dm]) * jnp.dot(y_m.astype(BF16), wbm_ref[...], preferred_element_type=F32)
              + _sigmoid(bg[:, dm:2 * dm]) * jnp.dot(y_h.astype(BF16), wbh_ref[...], preferred_element_type=F32)
              + _sigmoid(bg[:, 2 * dm:]) * jnp.dot(y_s.astype(BF16), wbs_ref[...], preferred_element_type=F32))
    out = jnp.dot(merged.astype(BF16), wo_ref[...], preferred_element_type=F32)
    o_ref[...] = x_ref[...] + gate_ref[0] * out


def _merge(cfg, x, mod, p, ym, yh, ys, mn, hn, sn, wbm, wbh, wbs, wo):
    tm = cfg.tm_merge
    row = cfg.mod_row(tm)
    rows_pad = mod.shape[0] // 6
    d = cfg.d
    const = lambda shape: pl.BlockSpec(shape, lambda m: (0,) * len(shape))
    return pl.pallas_call(
        functools.partial(_merge_kernel, d_model=d),
        grid=(cfg.t // tm,),
        in_specs=[pl.BlockSpec((tm, d), lambda m: (m, 0)),
                  pl.BlockSpec((1, 1, d), lambda m: (2 * rows_pad + row(m), 0, 0)),
                  pl.BlockSpec((tm, M_W), lambda m: (m, (OFF_M + 3 * M_W) // M_W)),
                  pl.BlockSpec((tm, H_W), lambda m: (m, (OFF_H + 2 * H_W) // H_W)),
                  pl.BlockSpec((tm, S_W), lambda m: (m, OFF_SZ // S_W)),
                  pl.BlockSpec((tm, 3 * d), lambda m: (m, OFF_BG // (3 * d))),
                  pl.BlockSpec((2, tm, M_W), lambda m: (0, m, 0)),
                  pl.BlockSpec((2, tm, H_W), lambda m: (0, m, 0)),
                  pl.BlockSpec((2, tm, S_W), lambda m: (0, m, 0)),
                  const((1, M_W)), const((1, H_W)), const((1, S_W)),
                  const((M_W, d)), const((H_W, d)), const((S_W, d)), const((d, d))],
        out_specs=pl.BlockSpec((tm, d), lambda m: (m, 0)),
        out_shape=jax.ShapeDtypeStruct((cfg.t, d), F32),
        compiler_params=_cparams(("arbitrary",)),
        name="merge",
    )(x, mod, p, p, p, p, ym, yh, ys, mn, hn, sn, wbm, wbh, wbs, wo)


def _ffn_up_kernel(x_ref, sc_ref, sh_ref, nw_ref, wa_ref, wb_ref, o_ref, *, d_ff, tc):
    x = x_ref[...]
    y = x * lax.rsqrt(jnp.mean(x * x, axis=-1, keepdims=True) + EPS) * nw_ref[...]
    h = (y * (1.0 + sc_ref[0]) + sh_ref[0]).astype(BF16)
    for j in range(d_ff // tc):
        sl = slice(j * tc, (j + 1) * tc)
        a = jnp.dot(h, wa_ref[:, sl], preferred_element_type=F32)
        b = jnp.dot(h, wb_ref[:, sl], preferred_element_type=F32)
        o_ref[:, sl] = (_silu(a) * b).astype(BF16)


def _ffn_up(cfg, x, mod, nw, w_gu):
    tm = cfg.tm_ffn
    row = cfg.mod_row(tm)
    rows_pad = mod.shape[0] // 6
    d, d_ff = cfg.d, cfg.d_ff
    return pl.pallas_call(
        functools.partial(_ffn_up_kernel, d_ff=d_ff, tc=256),
        grid=(cfg.t // tm,),
        in_specs=[pl.BlockSpec((tm, d), lambda m: (m, 0)),
                  pl.BlockSpec((1, 1, d), lambda m: (4 * rows_pad + row(m), 0, 0)),
                  pl.BlockSpec((1, 1, d), lambda m: (3 * rows_pad + row(m), 0, 0)),
                  pl.BlockSpec((1, d), lambda m: (0, 0)),
                  pl.BlockSpec((d, d_ff), lambda m: (0, 0)),
                  pl.BlockSpec((d, d_ff), lambda m: (0, 1))],
        out_specs=pl.BlockSpec((tm, d_ff), lambda m: (m, 0)),
        out_shape=jax.ShapeDtypeStruct((cfg.t, d_ff), BF16),
        compiler_params=_cparams(("arbitrary",)),
        name="ffn_up",
    )(x, mod, mod, nw, w_gu, w_gu)


def _ffn_down_kernel(x_ref, gate_ref, a_ref, w_ref, o_ref):
    o_ref[...] = x_ref[...] + gate_ref[0] * jnp.dot(a_ref[...], w_ref[...], preferred_element_type=F32)


def _ffn_down_final_kernel(x_ref, gate_ref, a_ref, w_ref, nf_ref, op_ref, os_ref, *, nct):
    m = pl.program_id(0)
    x = x_ref[...] + gate_ref[0] * jnp.dot(a_ref[...], w_ref[...], preferred_element_type=F32)
    y = x * lax.rsqrt(jnp.mean(x * x, axis=-1, keepdims=True) + EPS) * nf_ref[...]

    @pl.when(m < nct)
    def _():
        op_ref[...] = y

    @pl.when(m >= nct)
    def _():
        os_ref[...] = y


def _ffn_down(cfg, x, mod, act, w_down, norm_f=None):
    tm = cfg.tm_ffn
    row = cfg.mod_row(tm)
    rows_pad = mod.shape[0] // 6
    d, d_ff = cfg.d, cfg.d_ff
    in_specs = [pl.BlockSpec((tm, d), lambda m: (m, 0)),
                pl.BlockSpec((1, 1, d), lambda m: (5 * rows_pad + row(m), 0, 0)),
                pl.BlockSpec((tm, d_ff), lambda m: (m, 0)),
                pl.BlockSpec((d_ff, d), lambda m: (0, 0))]
    if norm_f is None:
        return pl.pallas_call(
            _ffn_down_kernel,
            grid=(cfg.t // tm,),
            in_specs=in_specs,
            out_specs=pl.BlockSpec((tm, d), lambda m: (m, 0)),
            out_shape=jax.ShapeDtypeStruct((cfg.t, d), F32),
            compiler_params=_cparams(("arbitrary",)),
            name="ffn_down",
        )(x, mod, act, w_down)
    nct = cfg.t_ctx // tm
    return pl.pallas_call(
        functools.partial(_ffn_down_final_kernel, nct=nct),
        grid=(cfg.t // tm,),
        in_specs=in_specs + [pl.BlockSpec((1, d), lambda m: (0, 0))],
        out_specs=[pl.BlockSpec((tm, d), lambda m: (jnp.minimum(m, nct - 1), 0)),
                   pl.BlockSpec((tm, d), lambda m: (jnp.maximum(m - nct, 0), 0))],
        out_shape=[jax.ShapeDtypeStruct((cfg.t_ctx, d), F32),
                   jax.ShapeDtypeStruct((cfg.t_smp, d), F32)],
        compiler_params=_cparams(("arbitrary",)),
        name="ffn_down_final",
    )(x, mod, act, w_down, norm_f)


def _grid_pos_embed(rows, d_model):
    quarter = d_model // 4
    freq = POS_BASE ** (-jnp.arange(quarter, dtype=F32) / quarter)
    r = jnp.arange(rows, dtype=F32)[:, None] * freq
    cl = jnp.arange(GRID_W, dtype=F32)[:, None] * freq
    row_e = jnp.concatenate([jnp.sin(r), jnp.cos(r)], axis=-1)
    col_e = jnp.concatenate([jnp.sin(cl), jnp.cos(cl)], axis=-1)
    emb = jnp.concatenate([jnp.broadcast_to(row_e[:, None], (rows, GRID_W, d_model // 2)),
                           jnp.broadcast_to(col_e[None], (rows, GRID_W, d_model // 2))], axis=-1)
    return emb.reshape(rows * GRID_W, d_model)


def kernel(x_prompt, x_sample, state_mlstm_c, state_mlstm_n, state_mlstm_m, state_hgrn, state_ssm, c, c_ctx, w_ada, b_ada, norm1, norm2, w_in, m_bi, m_bf, m_norm, h_lb, h_norm, s_conv_w, s_conv_b, s_dt_bias, s_a_log, s_d, s_norm, w_bm, w_bh, w_bs, w_out, w_gu, w_down, norm_f):
    n_ctx, l_ctx, d = x_prompt.shape
    n_smp, l_smp, _ = x_sample.shape
    depth = w_in.shape[0]
    d_ff = w_down.shape[1]
    cfg = _Cfg(n_ctx, l_ctx, n_smp, l_smp, d, d_ff)

    tri = _tri_consts(cfg.c)
    lvl, levels = _level_consts(cfg.c)

    rows_pad = -(-(n_smp + 1) // 8) * 8
    cvec = jnp.zeros((rows_pad, d), F32).at[:n_smp].set(c).at[n_smp].set(c_ctx)
    mod_all = _ada(cvec, w_ada, b_ada)
    mod_all = mod_all.reshape(depth, rows_pad, 6, d).transpose(0, 2, 1, 3).reshape(depth, 6 * rows_pad, 1, d)

    p_lb = jax.nn.softmax(h_lb.astype(F32), axis=1)
    lower = jnp.cumsum(p_lb, axis=1) - p_lb[:, :1]

    pos = _grid_pos_embed(l_smp // GRID_W, d)
    x = _embed(cfg, x_prompt.reshape(cfg.t_ctx, d), x_sample.reshape(cfg.t_smp, d), pos)

    a_neg = -jnp.exp(s_a_log.astype(F32))
    new_c, new_n, new_m, new_h, new_s = [], [], [], [], []
    y_p = y_s = None
    for i in range(depth):
        w = w_in[i]
        o_mi, o_hq, o_sz, o_xbc, o_dt, o_bg = 2048, 2064, 4624, 5648, 7696, 7728
        w_main = jnp.concatenate([w[:, o_xbc:o_dt], w[:, o_sz:o_xbc], w[:, o_bg:], w[:, :o_mi],
                                  w[:, o_hq:o_sz]], axis=1).astype(BF16)
        w_small = jnp.concatenate([w[:, o_mi:o_hq], w[:, o_dt:o_bg], w[:, o_dt:o_bg],
                                   jnp.zeros((d, LANES - GC_DTA - 2 * S_HEADS), F32)], axis=1).astype(BF16)
        dtb = s_dt_bias[i].reshape(-1)
        gbias = jnp.concatenate([m_bi[i].reshape(-1), m_bf[i].reshape(-1), dtb, dtb,
                                 jnp.zeros((LANES - GC_DTA - 2 * S_HEADS,), F32)]).reshape(1, LANES)
        gmul = jnp.concatenate([jnp.ones((GC_DTA,), F32), a_neg[i].reshape(-1),
                                jnp.zeros((LANES - GC_DTA - 2 * S_HEADS,), F32)]).reshape(1, LANES)
        ncw = -(-CONV_CH // PROJ_TN) * PROJ_TN
        cw = jnp.zeros((3, ncw), F32).at[:, :CONV_CH].set(s_conv_w[i]).reshape(3, ncw // PROJ_TN, PROJ_TN)
        cw = cw.transpose(1, 0, 2)
        cb = jnp.zeros((ncw,), F32).at[:CONV_CH].set(s_conv_b[i]).reshape(ncw // PROJ_TN, 1, PROJ_TN)
        mod = mod_all[i]

        p, g, gt = _inproj(cfg, x, mod, norm1[i].reshape(1, d), w_main, w_small, gbias, gmul, cw, cb)

        c0 = jnp.concatenate([state_mlstm_c[:, i], state_mlstm_n[:, i][..., None],
                              jnp.zeros(state_mlstm_c[:, i].shape[:-1] + (M_DH - 1,), F32)], axis=-1)
        m0 = jnp.broadcast_to(state_mlstm_m[:, i][..., None, None], (n_smp, 2, M_HEADS, 1, LANES))
        ym, c_fin, m_fin = _mlstm(cfg, p, g, gt, tri, c0, m0)
        yh, h_fin = _hgrn(cfg, p, lower[:, i].reshape(2, 1, H_W), lvl, levels, state_hgrn[:, i])
        sd = jnp.repeat(s_d[i].astype(F32), S_P).reshape(1, S_W)
        ys, s_fin = _ssd(cfg, p, g, gt, tri, sd, state_ssm[:, i])

        x = _merge(cfg, x, mod, p, ym, yh, ys, m_norm[i].reshape(1, M_W), h_norm[i].reshape(1, H_W),
                   s_norm[i].reshape(1, S_W), w_bm[i].astype(BF16), w_bh[i].astype(BF16),
                   w_bs[i].astype(BF16), w_out[i].astype(BF16))
        act = _ffn_up(cfg, x, mod, norm2[i].reshape(1, d), w_gu[i].astype(BF16))
        if i + 1 < depth:
            x = _ffn_down(cfg, x, mod, act, w_down[i].astype(BF16))
        else:
            y_p, y_s = _ffn_down(cfg, x, mod, act, w_down[i].astype(BF16), norm_f.reshape(1, d))

        new_c.append(c_fin[..., :M_DH])
        new_n.append(c_fin[..., M_DH])
        new_m.append(m_fin[..., 0, 0])
        new_h.append(h_fin)
        new_s.append(s_fin)

    return (y_p.reshape(n_ctx, l_ctx, d), y_s.reshape(n_smp, l_smp, d),
            jnp.stack(new_c, axis=1), jnp.stack(new_n, axis=1), jnp.stack(new_m, axis=1),
            jnp.stack(new_h, axis=1), jnp.stack(new_s, axis=1))
```

```python
import functools

import numpy as np
import jax
import jax.numpy as jnp
from jax import lax
from jax.experimental import pallas as pl
from jax.experimental.pallas import tpu as pltpu

F32 = jnp.float32
BF16 = jnp.bfloat16

GRID_W = 64
EPS = 1e-6
POS_BASE = 10000.0
NEG_BIG = -1e30
M_HEADS, M_DH = 4, 128
H_HEADS, H_DK, H_DV = 4, 128, 128
S_HEADS, S_P, S_GROUPS, S_N = 16, 64, 4, 128
M_W = M_HEADS * M_DH
H_W = H_HEADS * H_DV
S_W = S_HEADS * S_P
CONV_CH = S_W + 2 * S_GROUPS * S_N

LANES = 128
VMEM_LIMIT = 56 * 1024 * 1024
SCAN_CHUNK = 128
SCAN_BLOCK = 256
PROJ_TN = 1536

OFF_XBC, OFF_SZ, OFF_BG, OFF_M, OFF_H = 0, 2048, 3072, 6144, 8192
N_MAIN = 10752
GC_IG, GC_LF, GC_DT, GC_DTA = 0, 8, 16, 48


def _cparams(sem):
    return pltpu.CompilerParams(dimension_semantics=sem, vmem_limit_bytes=VMEM_LIMIT)


def _sigmoid(x):
    return 1.0 / (1.0 + jnp.exp(-x))


def _silu(x):
    return x * _sigmoid(x)


def _softplus(x):
    return jnp.maximum(x, 0.0) + jnp.log(1.0 + jnp.exp(-jnp.abs(x)))


def _dot(a, b):
    return jnp.dot(a.astype(BF16), b.astype(BF16), preferred_element_type=F32)


def _dot_nt(a, b):
    return lax.dot_general(a.astype(BF16), b.astype(BF16), (((1,), (1,)), ((), ())),
                           preferred_element_type=F32)


def _split3(x):
    x1 = x.astype(BF16)
    r1 = x - x1.astype(F32)
    x2 = r1.astype(BF16)
    x3 = (r1 - x2.astype(F32)).astype(BF16)
    return x1, x2, x3


def _sum01(sel, x):
    x1, x2, x3 = _split3(x)
    d = functools.partial(jnp.dot, preferred_element_type=F32)
    return d(sel, x1) + d(sel, x2) + d(sel, x3)


def _sum01_nt(x, sel):
    x1, x2, x3 = _split3(x)
    d = lambda a: lax.dot_general(a, sel, (((1,), (1,)), ((), ())), preferred_element_type=F32)
    return d(x1) + d(x2) + d(x3)


class _Cfg:
    def __init__(self, n_ctx, l_ctx, n_smp, l_smp, d_model, d_ff):
        self.n_ctx, self.l_ctx, self.n_smp, self.l_smp = n_ctx, l_ctx, n_smp, l_smp
        self.d, self.d_ff = d_model, d_ff
        self.t_ctx = n_ctx * l_ctx
        self.t_smp = n_smp * l_smp
        self.t = self.t_ctx + self.t_smp
        self.tb = SCAN_BLOCK
        self.c = SCAN_CHUNK
        assert l_ctx % self.tb == 0 and l_smp % self.tb == 0 and self.tb % self.c == 0
        assert l_ctx & (l_ctx - 1) == 0 and l_smp % GRID_W == 0
        self.nbc = l_ctx // self.tb
        self.nbs = l_smp // self.tb
        self.gc0 = self.t_ctx // self.tb
        self.nb = self.t // self.tb
        self.tm = 1024 if (self.t_ctx % 1024 == 0 and self.t_smp % 1024 == 0 and l_smp % 1024 == 0) else 256
        assert self.t_ctx % self.tm == 0 and l_smp % self.tm == 0 and self.tm % l_ctx == 0
        self.tm_merge = 256
        self.tm_ffn = 512 if self.tm == 1024 else 256

    def mod_row(self, tm):
        nct = self.t_ctx // tm
        per = self.l_smp // tm

        def f(m):
            return jnp.where(m < nct, self.n_smp, jnp.maximum(m - nct, 0) // per)
        return f


def _scan_pos(cfg, d, i):
    g = i + d * (cfg.nb - 1 - 2 * i)
    is_ctx = g < cfg.gc0
    gs = jnp.maximum(g - cfg.gc0, 0)
    pos = jnp.where(is_ctx, lax.rem(g, cfg.nbc), lax.rem(gs, cfg.nbs))
    nblk = jnp.where(is_ctx, cfg.nbc, cfg.nbs)
    spos = pos + d * (nblk - 1 - 2 * pos)
    return g, is_ctx, spos == 0, spos == nblk - 1


def _blk(cfg, d, i):
    return i + d * (cfg.nb - 1 - 2 * i)


def _smp_seq(cfg, d, i):
    return jnp.clip((_blk(cfg, d, i) - cfg.gc0) // cfg.nbs, 0, cfg.n_smp - 1)


def _ctx_seq(cfg, d, i):
    return jnp.minimum(_blk(cfg, d, i) // cfg.nbc, cfg.n_ctx - 1)


def _tri_consts(c):
    t = np.arange(c)
    fwd = (t[None, :] <= t[:, None]).astype(np.float32)
    return jnp.asarray(np.stack([fwd, fwd.T]), dtype=BF16)


def _embed_kernel(xp_ref, xs_ref, pos_ref, o_ref, *, nct):
    m = pl.program_id(0)

    @pl.when(m < nct)
    def _():
        o_ref[...] = xp_ref[...]

    @pl.when(m >= nct)
    def _():
        o_ref[...] = xs_ref[...] + pos_ref[...]


def _embed(cfg, xp2, xs2, pos):
    tm = cfg.tm
    nct = cfg.t_ctx // tm
    per = cfg.l_smp // tm
    return pl.pallas_call(
        functools.partial(_embed_kernel, nct=nct),
        grid=(cfg.t // tm,),
        in_specs=[pl.BlockSpec((tm, cfg.d), lambda m: (jnp.minimum(m, nct - 1), 0)),
                  pl.BlockSpec((tm, cfg.d), lambda m: (jnp.maximum(m - nct, 0), 0)),
                  pl.BlockSpec((tm, cfg.d), lambda m: (lax.rem(jnp.maximum(m - nct, 0), per), 0))],
        out_specs=pl.BlockSpec((tm, cfg.d), lambda m: (m, 0)),
        out_shape=jax.ShapeDtypeStruct((cfg.t, cfg.d), F32),
        compiler_params=_cparams(("arbitrary",)),
        name="embed",
    )(xp2, xs2, pos)


def _ada_kernel(c_ref, w_ref, b_ref, o_ref):
    o_ref[0] = _dot(_silu(c_ref[...]), w_ref[0]) + b_ref[0]


def _ada(cvec, w_ada, b_ada):
    depth, d, n = w_ada.shape
    rows = cvec.shape[0]
    tn = 1536
    return pl.pallas_call(
        _ada_kernel,
        grid=(depth, n // tn),
        in_specs=[pl.BlockSpec((rows, d), lambda l, j: (0, 0)),
                  pl.BlockSpec((1, d, tn), lambda l, j: (l, 0, j)),
                  pl.BlockSpec((1, 1, tn), lambda l, j: (l, 0, j))],
        out_specs=pl.BlockSpec((1, rows, tn), lambda l, j: (l, 0, j)),
        out_shape=jax.ShapeDtypeStruct((depth, rows, n), F32),
        compiler_params=_cparams(("arbitrary", "arbitrary")),
        name="ada",
    )(cvec, w_ada, b_ada.reshape(depth, 1, n))


def _store_transposed(dst_ref, src, col0, ncols):
    for cb in range(ncols // LANES):
        blk = src[:, col0 + cb * LANES:col0 + (cb + 1) * LANES]
        dst_ref[cb * LANES:(cb + 1) * LANES, :] = blk.T.astype(BF16)


def _inproj_kernel(x_ref, sc_ref, sh_ref, nw_ref, w_ref, ws_ref, gb_ref, ga_ref, cw_ref, cb_ref,
                   p_ref, g_ref, gt_ref, bmt_ref, kt_ref, vt_ref, xn_ref, *, tm, tn, nct, seg_ctx):
    m = pl.program_id(0)
    n = pl.program_id(1)

    @pl.when(n == 0)
    def _():
        x = x_ref[...]
        y = x * lax.rsqrt(jnp.mean(x * x, axis=-1, keepdims=True) + EPS) * nw_ref[...]
        h = (y * (1.0 + sc_ref[0]) + sh_ref[0]).astype(BF16)
        xn_ref[...] = h
        raw = jnp.dot(h, ws_ref[...], preferred_element_type=F32) + gb_ref[...]
        lane = lax.broadcasted_iota(jnp.int32, raw.shape, 1)
        sp = _softplus(raw)
        g = jnp.where(lane < GC_LF, raw,
                      jnp.where(lane < GC_DT, -_softplus(-raw),
                                jnp.where(lane < GC_DTA, sp, sp * ga_ref[...])))
        g_ref[...] = g
        gt_ref[...] = g.T

    acc = jnp.dot(xn_ref[...], w_ref[...], preferred_element_type=F32)

    n_conv_full = CONV_CH // tn
    conv_rem = CONV_CH - n_conv_full * tn

    gn = S_GROUPS * S_N

    def conv_cols(ncols, tile):
        row = lax.broadcasted_iota(jnp.int32, (tm, LANES), 0)
        seg_mask = jnp.where(m < nct, seg_ctx - 1, GRID_W - 1)
        inseg = row & seg_mask
        has_prev = inseg != 0
        has_next = inseg != seg_mask
        for cb in range(ncols // LANES):
            sl = slice(cb * LANES, (cb + 1) * LANES)
            u = acc[:, sl]
            up = jnp.where(has_prev, pltpu.roll(u, 1, 0), 0.0)
            un = jnp.where(has_next, pltpu.roll(u, tm - 1, 0), 0.0)
            cw = cw_ref[0]
            v = _silu(cw[0:1, sl] * up + cw[1:2, sl] * u + cw[2:3, sl] * un + cb_ref[0][:, sl])
            p_ref[:, sl] = v
            col = tile * tn + cb * LANES - S_W
            if 0 <= col < gn:
                bmt_ref[col:col + LANES, :] = v.T.astype(BF16)

    for tile in range(n_conv_full):
        @pl.when(n == tile)
        def _(tile=tile):
            conv_cols(tn, tile)

    if conv_rem:
        @pl.when(n == n_conv_full)
        def _():
            conv_cols(conv_rem, n_conv_full)
            p_ref[:, conv_rem:] = acc[:, conv_rem:]

    @pl.when(n > (n_conv_full if conv_rem else n_conv_full - 1))
    def _():
        p_ref[...] = acc

    for dst_ref, col in ((kt_ref, OFF_M + M_W), (vt_ref, OFF_H + H_W)):
        tile, off = divmod(col, tn)
        assert off + M_W <= tn and tile > n_conv_full

        @pl.when(n == tile)
        def _(dst_ref=dst_ref, off=off):
            _store_transposed(dst_ref, acc, off, M_W)


def _inproj(cfg, x, mod, nw, w_main, w_small, gbias, gmul, cw, cb):
    tm, tn = cfg.tm, PROJ_TN
    nct = cfg.t_ctx // tm
    row = cfg.mod_row(tm)
    rows_pad = mod.shape[0] // 6
    nt = N_MAIN // tn
    kern = functools.partial(_inproj_kernel, tm=tm, tn=tn, nct=nct, seg_ctx=cfg.l_ctx)
    tspec = pl.BlockSpec((M_W, tm), lambda m, n: (0, m))
    tshape = jax.ShapeDtypeStruct((M_W, cfg.t), BF16)
    return pl.pallas_call(
        kern,
        grid=(cfg.t // tm, nt),
        in_specs=[pl.BlockSpec((tm, cfg.d), lambda m, n: (m, 0)),
                  pl.BlockSpec((1, 1, cfg.d), lambda m, n: (1 * rows_pad + row(m), 0, 0)),
                  pl.BlockSpec((1, 1, cfg.d), lambda m, n: (0 * rows_pad + row(m), 0, 0)),
                  pl.BlockSpec((1, cfg.d), lambda m, n: (0, 0)),
                  pl.BlockSpec((cfg.d, tn), lambda m, n: (0, n)),
                  pl.BlockSpec((cfg.d, LANES), lambda m, n: (0, 0)),
                  pl.BlockSpec((1, LANES), lambda m, n: (0, 0)),
                  pl.BlockSpec((1, LANES), lambda m, n: (0, 0)),
                  pl.BlockSpec((1, 3, tn), lambda m, n: (jnp.minimum(n, 1), 0, 0)),
                  pl.BlockSpec((1, 1, tn), lambda m, n: (jnp.minimum(n, 1), 0, 0))],
        out_specs=[pl.BlockSpec((tm, tn), lambda m, n: (m, n)),
                   pl.BlockSpec((tm, LANES), lambda m, n: (m, 0)),
                   pl.BlockSpec((LANES, tm), lambda m, n: (0, m)),
                   tspec, tspec, tspec],
        out_shape=[jax.ShapeDtypeStruct((cfg.t, N_MAIN), F32),
                   jax.ShapeDtypeStruct((cfg.t, LANES), F32),
                   jax.ShapeDtypeStruct((LANES, cfg.t), F32),
                   tshape, tshape, tshape],
        scratch_shapes=[pltpu.VMEM((tm, cfg.d), BF16)],
        compiler_params=_cparams(("arbitrary", "arbitrary")),
        name="inproj",
    )(x, mod, mod, nw, w_main, w_small, gbias, gmul, cw, cb)


def _mlstm_kernel(q_ref, k_ref, v_ref, kt_ref, g_ref, gt_ref, tri_ref, c0_ref, m0_ref,
                  y_ref, cout_ref, mout_ref, c_sc, m_sc, *, cfg):
    c, tb = cfg.c, cfg.tb
    d = pl.program_id(0)
    i = pl.program_id(1)
    _, is_ctx, first, last = _scan_pos(cfg, d, i)

    @pl.when(jnp.logical_and(first, is_ctx))
    def _():
        c_sc[...] = jnp.zeros_like(c_sc)
        m_sc[...] = jnp.zeros_like(m_sc)

    @pl.when(jnp.logical_and(first, jnp.logical_not(is_ctx)))
    def _():
        c_sc[...] = c0_ref[...]
        m_sc[...] = m0_ref[...]

    def run(rev):
        dd = 1 if rev else 0
        sel = tri_ref[dd]
        tpos = lax.broadcasted_iota(jnp.int32, (c, c), 0)
        spos = lax.broadcasted_iota(jnp.int32, (c, c), 1)
        mask = (spos >= tpos) if rev else (spos <= tpos)
        unit = (lax.broadcasted_iota(jnp.int32, (c, LANES), 1) == 0).astype(F32)
        end = 0 if rev else c - 1
        nch = tb // c
        for ci in range(nch):
            r0 = (nch - 1 - ci if rev else ci) * c
            gc = g_ref[r0:r0 + c, :]
            gtc = gt_ref[:, r0:r0 + c]
            bcol = _sum01(sel, gc)
            brow = _sum01_nt(gtc[GC_LF:GC_LF + 8], sel)
            for h in range(M_HEADS):
                col = dd * M_HEADS + h
                hs = slice(h * M_DH, (h + 1) * M_DH)
                q = q_ref[r0:r0 + c, hs]
                k = k_ref[r0:r0 + c, hs] * (M_DH ** -0.5)
                vaug = jnp.concatenate([v_ref[r0:r0 + c, hs], unit], axis=1)
                b_c = bcol[:, GC_LF + col:GC_LF + col + 1]
                b_r = brow[col:col + 1, :]
                i_r = gtc[GC_IG + col:GC_IG + col + 1, :]
                m_prev = m_sc[h][:, 0:1]
                dlog = jnp.where(mask, b_c - b_r + i_r, NEG_BIG)
                inter = b_c + m_prev
                mt = jnp.maximum(inter, jnp.max(dlog, axis=-1, keepdims=True))
                s = _dot_nt(q, k) * jnp.exp(dlog - mt)
                wi = jnp.exp(inter - mt)
                caug = c_sc[h]
                numden = _dot(s, vaug) + wi * _dot(q, caug)
                den = numden[:, M_DH:M_DH + 1]
                hc = numden[:, :M_DH] / jnp.maximum(jnp.abs(den), jnp.exp(-mt))
                y_ref[r0:r0 + c, hs] = hc
                m_end = mt[end:end + 1, :]
                b_end = b_c[end:end + 1, :]
                wk = jnp.exp(b_end - b_r + i_r - m_end) * (M_DH ** -0.5)
                carry = jnp.exp(b_end + m_prev - m_end)
                kt = kt_ref[hs, r0:r0 + c].astype(F32)
                c_sc[h] = carry * caug + _dot(kt * wk, vaug)
                m_sc[h] = jnp.broadcast_to(m_end, (1, LANES))

    @pl.when(d == 0)
    def _():
        run(False)

    @pl.when(d == 1)
    def _():
        run(True)

    @pl.when(jnp.logical_and(last, is_ctx))
    def _():
        cout_ref[...] = c_sc[...]
        mout_ref[...] = m_sc[...]


def _mlstm(cfg, p, kt, g, gt, tri, c0, m0):
    tb = cfg.tb
    blk = functools.partial(_blk, cfg)
    smp = functools.partial(_smp_seq, cfg)
    ctx = functools.partial(_ctx_seq, cfg)
    qi = OFF_M // M_W
    st = (None, None, M_HEADS, M_DH, 2 * M_DH)
    ms = (None, None, M_HEADS, 1, LANES)
    return pl.pallas_call(
        functools.partial(_mlstm_kernel, cfg=cfg),
        grid=(2, cfg.nb),
        in_specs=[pl.BlockSpec((tb, M_W), lambda d, i: (blk(d, i), qi)),
                  pl.BlockSpec((tb, M_W), lambda d, i: (blk(d, i), qi + 1)),
                  pl.BlockSpec((tb, M_W), lambda d, i: (blk(d, i), qi + 2)),
                  pl.BlockSpec((M_W, tb), lambda d, i: (0, blk(d, i))),
                  pl.BlockSpec((tb, LANES), lambda d, i: (blk(d, i), 0)),
                  pl.BlockSpec((LANES, tb), lambda d, i: (0, blk(d, i))),
                  pl.BlockSpec((2, cfg.c, cfg.c), lambda d, i: (0, 0, 0)),
                  pl.BlockSpec(st, lambda d, i: (smp(d, i), d, 0, 0, 0)),
                  pl.BlockSpec(ms, lambda d, i: (smp(d, i), d, 0, 0, 0))],
        out_specs=[pl.BlockSpec((None, tb, M_W), lambda d, i: (d, blk(d, i), 0)),
                   pl.BlockSpec(st, lambda d, i: (ctx(d, i), d, 0, 0, 0)),
                   pl.BlockSpec(ms, lambda d, i: (ctx(d, i), d, 0, 0, 0))],
        out_shape=[jax.ShapeDtypeStruct((2, cfg.t, M_W), F32),
                   jax.ShapeDtypeStruct((cfg.n_ctx, 2, M_HEADS, M_DH, 2 * M_DH), F32),
                   jax.ShapeDtypeStruct((cfg.n_ctx, 2, M_HEADS, 1, LANES), F32)],
        scratch_shapes=[pltpu.VMEM((M_HEADS, M_DH, 2 * M_DH), F32),
                        pltpu.VMEM((M_HEADS, 1, LANES), F32)],
        compiler_params=_cparams(("arbitrary", "arbitrary")),
        name="mlstm",
    )(p, p, p, kt, g, gt, tri, c0, m0)


def _mid_rows(g, lev, rev):
    c, w = g.shape
    half = 1 << lev
    blk = 2 * half
    m = half if rev else half - 1
    if blk >= 8:
        g3 = g.reshape(c // blk, blk, w)
        return jnp.broadcast_to(g3[:, m:m + 1, :], g3.shape).reshape(c, w)
    g3 = g.reshape(c // 8, 8, w)
    sub = lax.broadcasted_iota(jnp.int32, g3.shape, 1)
    out = jnp.broadcast_to(g3[:, m:m + 1, :], g3.shape)
    for b0 in range(blk, 8, blk):
        out = jnp.where(sub >= b0, jnp.broadcast_to(g3[:, b0 + m:b0 + m + 1, :], g3.shape), out)
    return out.reshape(c, w)


def _hgrn_kernel(q_ref, v_ref, vt_ref, f_ref, lb_ref, tri_ref, s0_ref, y_ref, sout_ref, s_sc, *, cfg, levels):
    c, tb = cfg.c, cfg.tb
    d = pl.program_id(0)
    i = pl.program_id(1)
    _, is_ctx, first, last = _scan_pos(cfg, d, i)

    @pl.when(jnp.logical_and(first, is_ctx))
    def _():
        s_sc[...] = jnp.zeros_like(s_sc)

    @pl.when(jnp.logical_and(first, jnp.logical_not(is_ctx)))
    def _():
        for h in range(H_HEADS):
            s_sc[h] = s0_ref[h].T

    def run(rev):
        dd = 1 if rev else 0
        sel = tri_ref[dd]
        tpos = lax.broadcasted_iota(jnp.int32, (c, c), 0)
        spos = lax.broadcasted_iota(jnp.int32, (c, c), 1)
        before = (spos > tpos) if rev else (spos < tpos)
        pair = jnp.where(before, tpos ^ spos, 0)
        ones = jnp.ones((H_DK, LANES), BF16)
        end = 0 if rev else c - 1
        nch = tb // c
        lb = lb_ref[0]
        for ci in range(nch):
            r0 = (nch - 1 - ci if rev else ci) * c
            f = lb + (1.0 - lb) * _sigmoid(f_ref[r0:r0 + c, :])
            logf = jnp.log(f)
            kk_all = 1.0 - f
            g_all = _sum01(sel, logf)
            for h in range(H_HEADS):
                hs = slice(h * H_DK, (h + 1) * H_DK)
                q = q_ref[r0:r0 + c, hs]
                v = v_ref[r0:r0 + c, hs]
                kk = kk_all[:, hs]
                gcum = g_all[:, hs]
                att = jnp.zeros((c, c), F32)
                for lev in range(levels):
                    x = jnp.exp(-jnp.abs(gcum - _mid_rows(gcum, lev, rev)))
                    a = _dot_nt(q * x, kk * x)
                    att = jnp.where((pair >> lev) == 1, a, att)
                diag = jnp.dot((q * kk).astype(BF16), ones, preferred_element_type=F32)
                st = s_sc[h]
                o = _dot(att, v) + diag * v + _dot_nt(q * jnp.exp(gcum), st)
                y_ref[r0:r0 + c, hs] = o
                g_end = gcum[end:end + 1, :]
                vt = vt_ref[hs, r0:r0 + c]
                s_sc[h] = jnp.exp(g_end) * st + _dot(vt, kk * jnp.exp(g_end - gcum))

    @pl.when(d == 0)
    def _():
        run(False)

    @pl.when(d == 1)
    def _():
        run(True)

    @pl.when(jnp.logical_and(last, is_ctx))
    def _():
        for h in range(H_HEADS):
            sout_ref[h] = s_sc[h].T


def _hgrn(cfg, p, vt, lb, tri, s0):
    tb = cfg.tb
    blk = functools.partial(_blk, cfg)
    smp = functools.partial(_smp_seq, cfg)
    ctx = functools.partial(_ctx_seq, cfg)
    qi = OFF_H // H_W
    st = (None, None, H_HEADS, H_DK, H_DV)
    return pl.pallas_call(
        functools.partial(_hgrn_kernel, cfg=cfg, levels=int(np.log2(cfg.c))),
        grid=(2, cfg.nb),
        in_specs=[pl.BlockSpec((tb, H_W), lambda d, i: (blk(d, i), qi)),
                  pl.BlockSpec((tb, H_W), lambda d, i: (blk(d, i), qi + 1)),
                  pl.BlockSpec((H_W, tb), lambda d, i: (0, blk(d, i))),
                  pl.BlockSpec((tb, H_W), lambda d, i: (blk(d, i), qi + 3 + d)),
                  pl.BlockSpec((1, 1, H_W), lambda d, i: (d, 0, 0)),
                  pl.BlockSpec((2, cfg.c, cfg.c), lambda d, i: (0, 0, 0)),
                  pl.BlockSpec(st, lambda d, i: (smp(d, i), d, 0, 0, 0))],
        out_specs=[pl.BlockSpec((None, tb, H_W), lambda d, i: (d, blk(d, i), 0)),
                   pl.BlockSpec(st, lambda d, i: (ctx(d, i), d, 0, 0, 0))],
        out_shape=[jax.ShapeDtypeStruct((2, cfg.t, H_W), F32),
                   jax.ShapeDtypeStruct((cfg.n_ctx, 2, H_HEADS, H_DK, H_DV), F32)],
        scratch_shapes=[pltpu.VMEM((H_HEADS, H_DV, H_DK), F32)],
        compiler_params=_cparams(("arbitrary", "arbitrary")),
        name="hgrn",
    )(p, p, vt, p, lb, tri, s0)


def _expand2(x, e):
    x1 = x.astype(BF16)
    x2 = (x - x1.astype(F32)).astype(BF16)
    return jnp.dot(x1, e, preferred_element_type=F32) + jnp.dot(x2, e, preferred_element_type=F32)


def _ssd_kernel(x_ref, cm_ref, bmt_ref, g_ref, gt_ref, tri_ref, ex_ref, sd_ref, h0_ref, y_ref, hout_ref, h_sc,
                *, cfg):
    c, tb = cfg.c, cfg.tb
    d = pl.program_id(0)
    i = pl.program_id(1)
    _, is_ctx, first, last = _scan_pos(cfg, d, i)
    n_rep = S_HEADS // S_GROUPS
    gw = n_rep * S_P

    @pl.when(jnp.logical_and(first, is_ctx))
    def _():
        h_sc[...] = jnp.zeros_like(h_sc)

    @pl.when(jnp.logical_and(first, jnp.logical_not(is_ctx)))
    def _():
        for grp in range(S_GROUPS):
            h_sc[grp] = h0_ref[grp * n_rep:(grp + 1) * n_rep].reshape(gw, S_N).T

    def run(rev):
        dd = 1 if rev else 0
        sel = tri_ref[dd]
        ex = ex_ref[dd]
        tpos = lax.broadcasted_iota(jnp.int32, (c, c), 0)
        spos = lax.broadcasted_iota(jnp.int32, (c, c), 1)
        mask = (spos >= tpos) if rev else (spos <= tpos)
        lane = lax.broadcasted_iota(jnp.int32, (c, LANES), 1)
        la0 = GC_DTA + dd * S_HEADS
        is_la = (lane - la0).astype(jnp.uint32) < S_HEADS
        head_of_lane = lax.broadcasted_iota(jnp.int32, (c, gw), 1) // S_P
        end = 0 if rev else c - 1
        nch = tb // c
        for ci in range(nch):
            r0 = (nch - 1 - ci if rev else ci) * c
            gc = g_ref[r0:r0 + c, :]
            gtc = gt_ref[:, r0:r0 + c]
            lam = jnp.where(is_la, _sum01(sel, gc), 0.0)
            larow = _sum01_nt(gtc[la0:la0 + S_HEADS], sel)
            dtrow = gtc[GC_DT + dd * S_HEADS:GC_DT + (dd + 1) * S_HEADS]
            la_end = lam[end:end + 1, :]
            ea = _expand2(jnp.exp(lam), ex)
            wk = jnp.exp(la_end - lam) * pltpu.roll(gc, GC_DTA - GC_DT, 1)
            xall = x_ref[r0:r0 + c, :]
            xw = (xall * _expand2(wk, ex)).astype(BF16)
            for grp in range(S_GROUPS):
                gs = slice(grp * gw, (grp + 1) * gw)
                cmg = cm_ref[r0:r0 + c, grp * S_N:(grp + 1) * S_N].astype(BF16)
                bmt = bmt_ref[grp * S_N:(grp + 1) * S_N, r0:r0 + c]
                cb = jnp.dot(cmg, bmt, preferred_element_type=F32)
                xg = xall[:, gs]
                lhs, rhs = [], []
                for r in range(n_rep):
                    h = grp * n_rep + r
                    la_c = lam[:, la0 + h:la0 + h + 1]
                    seg = jnp.where(mask, jnp.exp(jnp.where(mask, la_c - larow[h:h + 1, :], 0.0))
                                    * dtrow[h:h + 1, :], 0.0)
                    lhs.append((seg * cb).astype(BF16))
                    rhs.append(jnp.where(head_of_lane == r, xg, 0.0).astype(BF16))
                ht = h_sc[grp]
                y = (jnp.dot(jnp.concatenate(lhs, axis=1), jnp.concatenate(rhs, axis=0),
                             preferred_element_type=F32)
                     + ea[:, gs] * jnp.dot(cmg, ht.astype(BF16), preferred_element_type=F32))
                if not rev:
                    y = y + sd_ref[:, gs] * xg
                y_ref[r0:r0 + c, gs] = y
                h_sc[grp] = ea[end:end + 1, gs] * ht + jnp.dot(bmt, xw[:, gs], preferred_element_type=F32)

    @pl.when(d == 0)
    def _():
        run(False)

    @pl.when(d == 1)
    def _():
        run(True)

    @pl.when(jnp.logical_and(last, is_ctx))
    def _():
        for grp in range(S_GROUPS):
            hout_ref[grp * n_rep:(grp + 1) * n_rep] = h_sc[grp].T.reshape(n_rep, S_P, S_N)


def _expand_consts():
    e = np.zeros((2, LANES, S_W), np.float32)
    for dd in range(2):
        for h in range(S_HEADS):
            e[dd, GC_DTA + dd * S_HEADS + h, h * S_P:(h + 1) * S_P] = 1.0
    return jnp.asarray(e, dtype=BF16)


def _ssd(cfg, p, bmt, g, gt, tri, ex, sd, h0):
    tb = cfg.tb
    blk = functools.partial(_blk, cfg)
    smp = functools.partial(_smp_seq, cfg)
    ctx = functools.partial(_ctx_seq, cfg)
    gn = S_GROUPS * S_N
    st = (None, None, S_HEADS, S_P, S_N)
    return pl.pallas_call(
        functools.partial(_ssd_kernel, cfg=cfg),
        grid=(2, cfg.nb),
        in_specs=[pl.BlockSpec((tb, S_W), lambda d, i: (blk(d, i), OFF_XBC // S_W)),
                  pl.BlockSpec((tb, gn), lambda d, i: (blk(d, i), (OFF_XBC + S_W) // gn + 1)),
                  pl.BlockSpec((gn, tb), lambda d, i: (0, blk(d, i))),
                  pl.BlockSpec((tb, LANES), lambda d, i: (blk(d, i), 0)),
                  pl.BlockSpec((LANES, tb), lambda d, i: (0, blk(d, i))),
                  pl.BlockSpec((2, cfg.c, cfg.c), lambda d, i: (0, 0, 0)),
                  pl.BlockSpec((2, LANES, S_W), lambda d, i: (0, 0, 0)),
                  pl.BlockSpec((1, S_W), lambda d, i: (0, 0)),
                  pl.BlockSpec(st, lambda d, i: (smp(d, i), d, 0, 0, 0))],
        out_specs=[pl.BlockSpec((None, tb, S_W), lambda d, i: (d, blk(d, i), 0)),
                   pl.BlockSpec(st, lambda d, i: (ctx(d, i), d, 0, 0, 0))],
        out_shape=[jax.ShapeDtypeStruct((2, cfg.t, S_W), F32),
                   jax.ShapeDtypeStruct((cfg.n_ctx, 2, S_HEADS, S_P, S_N), F32)],
        scratch_shapes=[pltpu.VMEM((S_GROUPS, S_N, (S_HEADS // S_GROUPS) * S_P), F32)],
        compiler_params=_cparams(("arbitrary", "arbitrary")),
        name="ssd",
    )(p, p, bmt, g, gt, tri, ex, sd, h0)


def _head_norm(y, heads, width):
    outs = []
    for h in range(heads):
        s = y[:, h * width:(h + 1) * width]
        outs.append(s * lax.rsqrt(jnp.mean(s * s, axis=-1, keepdims=True) + EPS))
    return jnp.concatenate(outs, axis=1)


def _merge_kernel(x_ref, gate_ref, mo_ref, hg_ref, sz_ref, bg_ref, ym_ref, yh_ref, ys_ref,
                  mn_ref, hn_ref, sn_ref, wbm_ref, wbh_ref, wbs_ref, wo_ref, o_ref, *, d_model):
    y_m = _head_norm(ym_ref[0] + ym_ref[1], M_HEADS, M_DH) * mn_ref[...] * _sigmoid(mo_ref[...])
    y_h = _head_norm(yh_ref[0] + yh_ref[1], H_HEADS, H_DV) * hn_ref[...] * _silu(hg_ref[...])
    ys = (ys_ref[0] + ys_ref[1]) * _silu(sz_ref[...])
    y_s = ys * lax.rsqrt(jnp.mean(ys * ys, axis=-1, keepdims=True) + EPS) * sn_ref[...]
    bg = bg_ref[...]
    dm = d_model
    merged = (_sigmoid(bg[:, :dm]) * jnp.dot(y_m.astype(BF16), wbm_ref[...], preferred_element_type=F32)
              + _sigmoid(bg[:, dm:2 * dm]) * jnp.dot(y_h.astype(BF16), wbh_ref[...], preferred_element_type=F32)
              + _sigmoid(bg[:, 2 * dm:]) * jnp.dot(y_s.astype(BF16), wbs_ref[...], preferred_element_type=F32))
    out = jnp.dot(merged.astype(BF16), wo_ref[...], preferred_element_type=F32)
    o_ref[...] = x_ref[...] + gate_ref[0] * out


def _merge(cfg, x, mod, p, ym, yh, ys, mn, hn, sn, wbm, wbh, wbs, wo):
    tm = cfg.tm_merge
    row = cfg.mod_row(tm)
    rows_pad = mod.shape[0] // 6
    d = cfg.d
    const = lambda shape: pl.BlockSpec(shape, lambda m: (0,) * len(shape))
    return pl.pallas_call(
        functools.partial(_merge_kernel, d_model=d),
        grid=(cfg.t // tm,),
        in_specs=[pl.BlockSpec((tm, d), lambda m: (m, 0)),
                  pl.BlockSpec((1, 1, d), lambda m: (2 * rows_pad + row(m), 0, 0)),
                  pl.BlockSpec((tm, M_W), lambda m: (m, (OFF_M + 3 * M_W) // M_W)),
                  pl.BlockSpec((tm, H_W), lambda m: (m, (OFF_H + 2 * H_W) // H_W)),
                  pl.BlockSpec((tm, S_W), lambda m: (m, OFF_SZ // S_W)),
                  pl.BlockSpec((tm, 3 * d), lambda m: (m, OFF_BG // (3 * d))),
                  pl.BlockSpec((2, tm, M_W), lambda m: (0, m, 0)),
                  pl.BlockSpec((2, tm, H_W), lambda m: (0, m, 0)),
                  pl.BlockSpec((2, tm, S_W), lambda m: (0, m, 0)),
                  const((1, M_W)), const((1, H_W)), const((1, S_W)),
                  const((M_W, d)), const((H_W, d)), const((S_W, d)), const((d, d))],
        out_specs=pl.BlockSpec((tm, d), lambda m: (m, 0)),
        out_shape=jax.ShapeDtypeStruct((cfg.t, d), F32),
        compiler_params=_cparams(("arbitrary",)),
        name="merge",
    )(x, mod, p, p, p, p, ym, yh, ys, mn, hn, sn, wbm, wbh, wbs, wo)


def _ffn_up_kernel(x_ref, sc_ref, sh_ref, nw_ref, wa_ref, wb_ref, o_ref, *, d_ff, tc):
    x = x_ref[...]
    y = x * lax.rsqrt(jnp.mean(x * x, axis=-1, keepdims=True) + EPS) * nw_ref[...]
    h = (y * (1.0 + sc_ref[0]) + sh_ref[0]).astype(BF16)
    for j in range(d_ff // tc):
        sl = slice(j * tc, (j + 1) * tc)
        a = jnp.dot(h, wa_ref[:, sl], preferred_element_type=F32)
        b = jnp.dot(h, wb_ref[:, sl], preferred_element_type=F32)
        o_ref[:, sl] = (_silu(a) * b).astype(BF16)


def _ffn_up(cfg, x, mod, nw, w_gu):
    tm = cfg.tm_ffn
    row = cfg.mod_row(tm)
    rows_pad = mod.shape[0] // 6
    d, d_ff = cfg.d, cfg.d_ff
    return pl.pallas_call(
        functools.partial(_ffn_up_kernel, d_ff=d_ff, tc=256),
        grid=(cfg.t // tm,),
        in_specs=[pl.BlockSpec((tm, d), lambda m: (m, 0)),
                  pl.BlockSpec((1, 1, d), lambda m: (4 * rows_pad + row(m), 0, 0)),
                  pl.BlockSpec((1, 1, d), lambda m: (3 * rows_pad + row(m), 0, 0)),
                  pl.BlockSpec((1, d), lambda m: (0, 0)),
                  pl.BlockSpec((d, d_ff), lambda m: (0, 0)),
                  pl.BlockSpec((d, d_ff), lambda m: (0, 1))],
        out_specs=pl.BlockSpec((tm, d_ff), lambda m: (m, 0)),
        out_shape=jax.ShapeDtypeStruct((cfg.t, d_ff), BF16),
        compiler_params=_cparams(("arbitrary",)),
        name="ffn_up",
    )(x, mod, mod, nw, w_gu, w_gu)


def _ffn_down_kernel(x_ref, gate_ref, a_ref, w_ref, o_ref):
    o_ref[...] = x_ref[...] + gate_ref[0] * jnp.dot(a_ref[...], w_ref[...], preferred_element_type=F32)


def _ffn_down_final_kernel(x_ref, gate_ref, a_ref, w_ref, nf_ref, op_ref, os_ref, *, nct):
    m = pl.program_id(0)
    x = x_ref[...] + gate_ref[0] * jnp.dot(a_ref[...], w_ref[...], preferred_element_type=F32)
    y = x * lax.rsqrt(jnp.mean(x * x, axis=-1, keepdims=True) + EPS) * nf_ref[...]

    @pl.when(m < nct)
    def _():
        op_ref[...] = y

    @pl.when(m >= nct)
    def _():
        os_ref[...] = y


def _ffn_down(cfg, x, mod, act, w_down, norm_f=None):
    tm = cfg.tm_ffn
    row = cfg.mod_row(tm)
    rows_pad = mod.shape[0] // 6
    d, d_ff = cfg.d, cfg.d_ff
    in_specs = [pl.BlockSpec((tm, d), lambda m: (m, 0)),
                pl.BlockSpec((1, 1, d), lambda m: (5 * rows_pad + row(m), 0, 0)),
                pl.BlockSpec((tm, d_ff), lambda m: (m, 0)),
                pl.BlockSpec((d_ff, d), lambda m: (0, 0))]
    if norm_f is None:
        return pl.pallas_call(
            _ffn_down_kernel,
            grid=(cfg.t // tm,),
            in_specs=in_specs,
            out_specs=pl.BlockSpec((tm, d), lambda m: (m, 0)),
            out_shape=jax.ShapeDtypeStruct((cfg.t, d), F32),
            compiler_params=_cparams(("arbitrary",)),
            name="ffn_down",
        )(x, mod, act, w_down)
    nct = cfg.t_ctx // tm
    return pl.pallas_call(
        functools.partial(_ffn_down_final_kernel, nct=nct),
        grid=(cfg.t // tm,),
        in_specs=in_specs + [pl.BlockSpec((1, d), lambda m: (0, 0))],
        out_specs=[pl.BlockSpec((tm, d), lambda m: (jnp.minimum(m, nct - 1), 0)),
                   pl.BlockSpec((tm, d), lambda m: (jnp.maximum(m - nct, 0), 0))],
        out_shape=[jax.ShapeDtypeStruct((cfg.t_ctx, d), F32),
                   jax.ShapeDtypeStruct((cfg.t_smp, d), F32)],
        compiler_params=_cparams(("arbitrary",)),
        name="ffn_down_final",
    )(x, mod, act, w_down, norm_f)


def _grid_pos_embed(rows, d_model):
    quarter = d_model // 4
    freq = POS_BASE ** (-jnp.arange(quarter, dtype=F32) / quarter)
    r = jnp.arange(rows, dtype=F32)[:, None] * freq
    cl = jnp.arange(GRID_W, dtype=F32)[:, None] * freq
    row_e = jnp.concatenate([jnp.sin(r), jnp.cos(r)], axis=-1)
    col_e = jnp.concatenate([jnp.sin(cl), jnp.cos(cl)], axis=-1)
    emb = jnp.concatenate([jnp.broadcast_to(row_e[:, None], (rows, GRID_W, d_model // 2)),
                           jnp.broadcast_to(col_e[None], (rows, GRID_W, d_model // 2))], axis=-1)
    return emb.reshape(rows * GRID_W, d_model)


def kernel(x_prompt, x_sample, state_mlstm_c, state_mlstm_n, state_mlstm_m, state_hgrn, state_ssm, c, c_ctx, w_ada, b_ada, norm1, norm2, w_in, m_bi, m_bf, m_norm, h_lb, h_norm, s_conv_w, s_conv_b, s_dt_bias, s_a_log, s_d, s_norm, w_bm, w_bh, w_bs, w_out, w_gu, w_down, norm_f):
    n_ctx, l_ctx, d = x_prompt.shape
    n_smp, l_smp, _ = x_sample.shape
    depth = w_in.shape[0]
    d_ff = w_down.shape[1]
    cfg = _Cfg(n_ctx, l_ctx, n_smp, l_smp, d, d_ff)

    tri = _tri_consts(cfg.c)
    ex = _expand_consts()

    rows_pad = -(-(n_smp + 1) // 8) * 8
    cvec = jnp.zeros((rows_pad, d), F32).at[:n_smp].set(c).at[n_smp].set(c_ctx)
    mod_all = _ada(cvec, w_ada, b_ada)
    mod_all = mod_all.reshape(depth, rows_pad, 6, d).transpose(0, 2, 1, 3).reshape(depth, 6 * rows_pad, 1, d)

    p_lb = jax.nn.softmax(h_lb.astype(F32), axis=1)
    lower = jnp.cumsum(p_lb, axis=1) - p_lb[:, :1]

    pos = _grid_pos_embed(l_smp // GRID_W, d)
    x = _embed(cfg, x_prompt.reshape(cfg.t_ctx, d), x_sample.reshape(cfg.t_smp, d), pos)

    a_neg = -jnp.exp(s_a_log.astype(F32))
    new_c, new_n, new_m, new_h, new_s = [], [], [], [], []
    y_p = y_s = None
    for i in range(depth):
        w = w_in[i]
        o_mi, o_hq, o_sz, o_xbc, o_dt, o_bg = 2048, 2064, 4624, 5648, 7696, 7728
        w_main = jnp.concatenate([w[:, o_xbc:o_dt], w[:, o_sz:o_xbc], w[:, o_bg:], w[:, :o_mi],
                                  w[:, o_hq:o_sz]], axis=1).astype(BF16)
        w_small = jnp.concatenate([w[:, o_mi:o_hq], w[:, o_dt:o_bg], w[:, o_dt:o_bg],
                                   jnp.zeros((d, LANES - GC_DTA - 2 * S_HEADS), F32)], axis=1).astype(BF16)
        dtb = s_dt_bias[i].reshape(-1)
        gbias = jnp.concatenate([m_bi[i].reshape(-1), m_bf[i].reshape(-1), dtb, dtb,
                                 jnp.zeros((LANES - GC_DTA - 2 * S_HEADS,), F32)]).reshape(1, LANES)
        gmul = jnp.concatenate([jnp.ones((GC_DTA,), F32), a_neg[i].reshape(-1),
                                jnp.zeros((LANES - GC_DTA - 2 * S_HEADS,), F32)]).reshape(1, LANES)
        ncw = -(-CONV_CH // PROJ_TN) * PROJ_TN
        cw = jnp.zeros((3, ncw), F32).at[:, :CONV_CH].set(s_conv_w[i]).reshape(3, ncw // PROJ_TN, PROJ_TN)
        cw = cw.transpose(1, 0, 2)
        cb = jnp.zeros((ncw,), F32).at[:CONV_CH].set(s_conv_b[i]).reshape(ncw // PROJ_TN, 1, PROJ_TN)
        mod = mod_all[i]

        p, g, gt, bmt, kt, vt = _inproj(cfg, x, mod, norm1[i].reshape(1, d), w_main, w_small, gbias, gmul,
                                        cw, cb)

        c0 = jnp.concatenate([state_mlstm_c[:, i], state_mlstm_n[:, i][..., None],
                              jnp.zeros(state_mlstm_c[:, i].shape[:-1] + (M_DH - 1,), F32)], axis=-1)
        m0 = jnp.broadcast_to(state_mlstm_m[:, i][..., None, None], (n_smp, 2, M_HEADS, 1, LANES))
        ym, c_fin, m_fin = _mlstm(cfg, p, kt, g, gt, tri, c0, m0)
        yh, h_fin = _hgrn(cfg, p, vt, lower[:, i].reshape(2, 1, H_W), tri, state_hgrn[:, i])
        sd = jnp.repeat(s_d[i].astype(F32), S_P).reshape(1, S_W)
        ys, s_fin = _ssd(cfg, p, bmt, g, gt, tri, ex, sd, state_ssm[:, i])

        x = _merge(cfg, x, mod, p, ym, yh, ys, m_norm[i].reshape(1, M_W), h_norm[i].reshape(1, H_W),
                   s_norm[i].reshape(1, S_W), w_bm[i].astype(BF16), w_bh[i].astype(BF16),
                   w_bs[i].astype(BF16), w_out[i].astype(BF16))
        act = _ffn_up(cfg, x, mod, norm2[i].reshape(1, d), w_gu[i].astype(BF16))
        if i + 1 < depth:
            x = _ffn_down(cfg, x, mod, act, w_down[i].astype(BF16))
        else:
            y_p, y_s = _ffn_down(cfg, x, mod, act, w_down[i].astype(BF16), norm_f.reshape(1, d))

        new_c.append(c_fin[..., :M_DH])
        new_n.append(c_fin[..., M_DH])
        new_m.append(m_fin[..., 0, 0])
        new_h.append(h_fin)
        new_s.append(s_fin)

    return (y_p.reshape(n_ctx, l_ctx, d), y_s.reshape(n_smp, l_smp, d),
            jnp.stack(new_c, axis=1), jnp.stack(new_n, axis=1), jnp.stack(new_m, axis=1),
            jnp.stack(new_h, axis=1), jnp.stack(new_s, axis=1))
```

```python
import functools

import numpy as np
import jax
import jax.numpy as jnp
from jax import lax
from jax.experimental import pallas as pl
from jax.experimental.pallas import tpu as pltpu

F32 = jnp.float32
BF16 = jnp.bfloat16

GRID_W = 64
EPS = 1e-6
POS_BASE = 10000.0
NEG_BIG = -1e30
M_HEADS, M_DH = 4, 128
H_HEADS, H_DK, H_DV = 4, 128, 128
S_HEADS, S_P, S_GROUPS, S_N = 16, 64, 4, 128
M_W = M_HEADS * M_DH
H_W = H_HEADS * H_DV
S_W = S_HEADS * S_P
CONV_CH = S_W + 2 * S_GROUPS * S_N

LANES = 128
MXU_N = 256
VMEM_LIMIT = 56 * 1024 * 1024
SCAN_CHUNK = 128
SCAN_BLOCK = 256
PROJ_TN = 1536

OFF_XBC, OFF_SZ, OFF_BG, OFF_M, OFF_H = 0, 2048, 3072, 6144, 8192
N_MAIN = 10752
GC_IG, GC_LF, GC_DT, GC_DTA = 0, 8, 16, 48


def _cparams(sem):
    return pltpu.CompilerParams(dimension_semantics=sem, vmem_limit_bytes=VMEM_LIMIT)


def _sigmoid(x):
    return 1.0 / (1.0 + jnp.exp(-x))


def _silu(x):
    return x * _sigmoid(x)


def _softplus(x):
    return jnp.maximum(x, 0.0) + jnp.log(1.0 + jnp.exp(-jnp.abs(x)))


def _dot(a, b):
    return jnp.dot(a.astype(BF16), b.astype(BF16), preferred_element_type=F32)


def _dot_nt(a, b):
    return lax.dot_general(a.astype(BF16), b.astype(BF16), (((1,), (1,)), ((), ())),
                           preferred_element_type=F32)


def _split3(x):
    x1 = x.astype(BF16)
    r1 = x - x1.astype(F32)
    x2 = r1.astype(BF16)
    x3 = (r1 - x2.astype(F32)).astype(BF16)
    return x1, x2, x3


def _sum01(sel, x):
    x1, x2, x3 = _split3(x)
    d = functools.partial(jnp.dot, preferred_element_type=F32)
    return d(sel, x1) + d(sel, x2) + d(sel, x3)


def _sum01_nt(x, sel):
    x1, x2, x3 = _split3(x)
    d = lambda a: lax.dot_general(a, sel, (((1,), (1,)), ((), ())), preferred_element_type=F32)
    return d(x1) + d(x2) + d(x3)


def _expand2(x, e):
    x1 = x.astype(BF16)
    x2 = (x - x1.astype(F32)).astype(BF16)
    return jnp.dot(x1, e, preferred_element_type=F32) + jnp.dot(x2, e, preferred_element_type=F32)


class _Cfg:
    def __init__(self, n_ctx, l_ctx, n_smp, l_smp, d_model, d_ff):
        self.n_ctx, self.l_ctx, self.n_smp, self.l_smp = n_ctx, l_ctx, n_smp, l_smp
        self.d, self.d_ff = d_model, d_ff
        self.t_ctx = n_ctx * l_ctx
        self.t_smp = n_smp * l_smp
        self.t = self.t_ctx + self.t_smp
        self.tb = SCAN_BLOCK
        self.c = SCAN_CHUNK
        assert l_ctx % self.tb == 0 and l_smp % self.tb == 0 and self.tb % self.c == 0
        assert l_ctx & (l_ctx - 1) == 0 and l_smp % GRID_W == 0
        self.nbc = l_ctx // self.tb
        self.nbs = l_smp // self.tb
        self.gc0 = self.t_ctx // self.tb
        self.nb = self.t // self.tb
        self.tm = 1024 if (self.t_ctx % 1024 == 0 and self.t_smp % 1024 == 0 and l_smp % 1024 == 0) else 256
        assert self.t_ctx % self.tm == 0 and l_smp % self.tm == 0 and self.tm % l_ctx == 0
        self.tm_merge = 256
        self.tm_ffn = 512 if self.tm == 1024 else 256

    def mod_row(self, tm):
        nct = self.t_ctx // tm
        per = self.l_smp // tm

        def f(m):
            return jnp.where(m < nct, self.n_smp, jnp.maximum(m - nct, 0) // per)
        return f

    def blk(self, d, j):
        return j + d * (self.nb - 1 - 2 * j)

    def scan_flags(self, d, j):
        g = self.blk(d, j)
        is_ctx = g < self.gc0
        gs = jnp.maximum(g - self.gc0, 0)
        pos = jnp.where(is_ctx, lax.rem(g, self.nbc), lax.rem(gs, self.nbs))
        nblk = jnp.where(is_ctx, self.nbc, self.nbs)
        spos = pos + d * (nblk - 1 - 2 * pos)
        return is_ctx, spos == 0, spos == nblk - 1

    def smp_seq(self, d, j):
        return jnp.clip((self.blk(d, j) - self.gc0) // self.nbs, 0, self.n_smp - 1)

    def ctx_seq(self, d, j):
        return jnp.minimum(self.blk(d, j) // self.nbc, self.n_ctx - 1)


def _tri_consts(c):
    t = np.arange(c)
    fwd = (t[None, :] <= t[:, None]).astype(np.float32)
    return jnp.asarray(np.stack([fwd, fwd.T]), dtype=BF16)


def _expand_consts():
    e = np.zeros((2, LANES, S_W), np.float32)
    for dd in range(2):
        for h in range(S_HEADS):
            e[dd, GC_DTA + dd * S_HEADS + h, h * S_P:(h + 1) * S_P] = 1.0
    return jnp.asarray(e, dtype=BF16)


def _embed_kernel(xp_ref, xs_ref, pos_ref, o_ref, *, nct):
    m = pl.program_id(0)

    @pl.when(m < nct)
    def _():
        o_ref[...] = xp_ref[...]

    @pl.when(m >= nct)
    def _():
        o_ref[...] = xs_ref[...] + pos_ref[...]


def _embed(cfg, xp2, xs2, pos):
    tm = cfg.tm
    nct = cfg.t_ctx // tm
    per = cfg.l_smp // tm
    return pl.pallas_call(
        functools.partial(_embed_kernel, nct=nct),
        grid=(cfg.t // tm,),
        in_specs=[pl.BlockSpec((tm, cfg.d), lambda m: (jnp.minimum(m, nct - 1), 0)),
                  pl.BlockSpec((tm, cfg.d), lambda m: (jnp.maximum(m - nct, 0), 0)),
                  pl.BlockSpec((tm, cfg.d), lambda m: (lax.rem(jnp.maximum(m - nct, 0), per), 0))],
        out_specs=pl.BlockSpec((tm, cfg.d), lambda m: (m, 0)),
        out_shape=jax.ShapeDtypeStruct((cfg.t, cfg.d), F32),
        compiler_params=_cparams(("arbitrary",)),
        name="embed",
    )(xp2, xs2, pos)


def _ada_kernel(c_ref, w_ref, b_ref, o_ref):
    o_ref[0] = _dot(_silu(c_ref[...]), w_ref[0]) + b_ref[0]


def _ada(cvec, w_ada, b_ada):
    depth, d, n = w_ada.shape
    rows = cvec.shape[0]
    tn = 1536
    return pl.pallas_call(
        _ada_kernel,
        grid=(depth, n // tn),
        in_specs=[pl.BlockSpec((rows, d), lambda l, j: (0, 0)),
                  pl.BlockSpec((1, d, tn), lambda l, j: (l, 0, j)),
                  pl.BlockSpec((1, 1, tn), lambda l, j: (l, 0, j))],
        out_specs=pl.BlockSpec((1, rows, tn), lambda l, j: (l, 0, j)),
        out_shape=jax.ShapeDtypeStruct((depth, rows, n), F32),
        compiler_params=_cparams(("arbitrary", "arbitrary")),
        name="ada",
    )(cvec, w_ada, b_ada.reshape(depth, 1, n))


def _inproj_kernel(x_ref, sc_ref, sh_ref, nw_ref, w_ref, ws_ref, gb_ref, ga_ref, cw_ref, cb_ref,
                   p_ref, g_ref, gt_ref, bmt_ref, kt_ref, vt_ref, xn_ref, *, tm, tn, nct, seg_ctx):
    m = pl.program_id(0)
    n = pl.program_id(1)
    gn = S_GROUPS * S_N
    nt = N_MAIN // tn

    def prologue():
        x = x_ref[...]
        y = x * lax.rsqrt(jnp.mean(x * x, axis=-1, keepdims=True) + EPS) * nw_ref[...]
        h = (y * (1.0 + sc_ref[0]) + sh_ref[0]).astype(BF16)
        xn_ref[...] = h
        raw = jnp.dot(h, ws_ref[...], preferred_element_type=F32) + gb_ref[...]
        lane = lax.broadcasted_iota(jnp.int32, raw.shape, 1)
        sp = _softplus(raw)
        g = jnp.where(lane < GC_LF, raw,
                      jnp.where(lane < GC_DT, -_softplus(-raw),
                                jnp.where(lane < GC_DTA, sp, sp * ga_ref[...])))
        g_ref[...] = g
        gt_ref[...] = g.T

    def matmul_cols(c0, c1):
        return jnp.dot(xn_ref[...], w_ref[:, c0:c1], preferred_element_type=F32)

    def conv_tile(tile, ncols):
        row = lax.broadcasted_iota(jnp.int32, (tm, LANES), 0)
        seg_mask = jnp.where(m < nct, seg_ctx - 1, GRID_W - 1)
        inseg = row & seg_mask
        has_prev = inseg != 0
        has_next = inseg != seg_mask
        cw = cw_ref[0]
        for c0 in range(0, ncols, MXU_N):
            acc = matmul_cols(c0, c0 + MXU_N)
            for cb in range(MXU_N // LANES):
                lo = c0 + cb * LANES
                sl = slice(lo, lo + LANES)
                u = acc[:, cb * LANES:(cb + 1) * LANES]
                up = jnp.where(has_prev, pltpu.roll(u, 1, 0), 0.0)
                un = jnp.where(has_next, pltpu.roll(u, tm - 1, 0), 0.0)
                v = _silu(cw[0:1, sl] * up + cw[1:2, sl] * u + cw[2:3, sl] * un + cb_ref[0][:, sl])
                p_ref[:, sl] = v
                col = tile * tn + lo - S_W
                if 0 <= col < gn:
                    bmt_ref[col:col + LANES, :] = v.T.astype(BF16)
        if ncols < tn:
            p_ref[:, ncols:] = matmul_cols(ncols, tn)

    def plain_tile(transposed):
        acc = matmul_cols(0, tn)
        p_ref[...] = acc
        for dst_ref, off in transposed:
            for cb in range(M_W // LANES):
                blk = acc[:, off + cb * LANES:off + (cb + 1) * LANES]
                dst_ref[cb * LANES:(cb + 1) * LANES, :] = blk.T.astype(BF16)

    n_conv = -(-CONV_CH // tn)
    special = {}
    for dst_ref, col in ((kt_ref, OFF_M + M_W), (vt_ref, OFF_H + H_W)):
        tile, off = divmod(col, tn)
        assert off + M_W <= tn and tile >= n_conv
        special.setdefault(tile, []).append((dst_ref, off))

    for tile in range(nt):
        @pl.when(n == tile)
        def _(tile=tile):
            if tile == 0:
                prologue()
            if tile < n_conv:
                conv_tile(tile, min(tn, CONV_CH - tile * tn))
            else:
                plain_tile(special.get(tile, []))


def _inproj(cfg, x, mod, nw, w_main, w_small, gbias, gmul, cw, cb):
    tm, tn = cfg.tm, PROJ_TN
    nct = cfg.t_ctx // tm
    row = cfg.mod_row(tm)
    rows_pad = mod.shape[0] // 6
    nt = N_MAIN // tn
    kern = functools.partial(_inproj_kernel, tm=tm, tn=tn, nct=nct, seg_ctx=cfg.l_ctx)
    tspec = pl.BlockSpec((M_W, tm), lambda m, n: (0, m))
    tshape = jax.ShapeDtypeStruct((M_W, cfg.t), BF16)
    return pl.pallas_call(
        kern,
        grid=(cfg.t // tm, nt),
        in_specs=[pl.BlockSpec((tm, cfg.d), lambda m, n: (m, 0)),
                  pl.BlockSpec((1, 1, cfg.d), lambda m, n: (1 * rows_pad + row(m), 0, 0)),
                  pl.BlockSpec((1, 1, cfg.d), lambda m, n: (0 * rows_pad + row(m), 0, 0)),
                  pl.BlockSpec((1, cfg.d), lambda m, n: (0, 0)),
                  pl.BlockSpec((cfg.d, tn), lambda m, n: (0, n)),
                  pl.BlockSpec((cfg.d, LANES), lambda m, n: (0, 0)),
                  pl.BlockSpec((1, LANES), lambda m, n: (0, 0)),
                  pl.BlockSpec((1, LANES), lambda m, n: (0, 0)),
                  pl.BlockSpec((1, 3, tn), lambda m, n: (jnp.minimum(n, 1), 0, 0)),
                  pl.BlockSpec((1, 1, tn), lambda m, n: (jnp.minimum(n, 1), 0, 0))],
        out_specs=[pl.BlockSpec((tm, tn), lambda m, n: (m, n)),
                   pl.BlockSpec((tm, LANES), lambda m, n: (m, 0)),
                   pl.BlockSpec((LANES, tm), lambda m, n: (0, m)),
                   tspec, tspec, tspec],
        out_shape=[jax.ShapeDtypeStruct((cfg.t, N_MAIN), F32),
                   jax.ShapeDtypeStruct((cfg.t, LANES), F32),
                   jax.ShapeDtypeStruct((LANES, cfg.t), F32),
                   tshape, tshape, tshape],
        scratch_shapes=[pltpu.VMEM((tm, cfg.d), BF16)],
        compiler_params=_cparams(("arbitrary", "arbitrary")),
        name="inproj",
    )(x, mod, mod, nw, w_main, w_small, gbias, gmul, cw, cb)


def _bidir_kernel(*refs, cfg, n_dir, n_shared, n_state, n_scratch, run, load_state, store_state):
    it = iter(refs)
    take = lambda k: [next(it) for _ in range(k)]
    din = [take(n_dir), take(n_dir)]
    shared = take(n_shared)
    sin = [take(n_state), take(n_state)]
    yout = [take(1)[0], take(1)[0]]
    sout = [take(n_state), take(n_state)]
    scr = [take(n_scratch), take(n_scratch)]
    j = pl.program_id(0)
    flags = [cfg.scan_flags(d, j) for d in (0, 1)]

    for d in (0, 1):
        is_ctx, first, _ = flags[d]

        @pl.when(jnp.logical_and(first, is_ctx))
        def _(d=d):
            for s in scr[d]:
                s[...] = jnp.zeros_like(s)

        @pl.when(jnp.logical_and(first, jnp.logical_not(is_ctx)))
        def _(d=d):
            load_state(sin[d], scr[d])

    run(shared, [(bool(d), din[d], yout[d], scr[d]) for d in (0, 1)])

    for d in (0, 1):
        is_ctx, _, last = flags[d]

        @pl.when(jnp.logical_and(last, is_ctx))
        def _(d=d):
            store_state(scr[d], sout[d])


def _bidir_call(cfg, name, y_width, dir_inputs, shared_inputs, state_inputs, state_shapes, scratch_shapes,
                run, load_state, store_state):
    tb = cfg.tb
    in_specs, args = [], []
    for d in (0, 1):
        for arr, bshape, ifn in dir_inputs:
            in_specs.append(pl.BlockSpec(bshape, lambda j, d=d, ifn=ifn: ifn(d, cfg.blk(d, j))))
            args.append(arr)
    for arr in shared_inputs:
        in_specs.append(pl.BlockSpec(arr.shape, lambda j, nd=arr.ndim: (0,) * nd))
        args.append(arr)
    for d in (0, 1):
        for arr, shp in zip(state_inputs, state_shapes):
            in_specs.append(pl.BlockSpec((None, None) + shp,
                                         lambda j, d=d, k=len(shp): (cfg.smp_seq(d, j), d) + (0,) * k))
            args.append(arr)
    out_specs, out_shape = [], []
    for d in (0, 1):
        out_specs.append(pl.BlockSpec((tb, y_width), lambda j, d=d: (cfg.blk(d, j), 0)))
        out_shape.append(jax.ShapeDtypeStruct((cfg.t, y_width), BF16))
    for d in (0, 1):
        for shp in state_shapes:
            out_specs.append(pl.BlockSpec((None,) + shp, lambda j, d=d, k=len(shp): (cfg.ctx_seq(d, j),) + (0,) * k))
            out_shape.append(jax.ShapeDtypeStruct((cfg.n_ctx,) + shp, F32))
    kern = functools.partial(_bidir_kernel, cfg=cfg, n_dir=len(dir_inputs), n_shared=len(shared_inputs),
                             n_state=len(state_inputs), n_scratch=len(scratch_shapes), run=run,
                             load_state=load_state, store_state=store_state)
    outs = pl.pallas_call(
        kern,
        grid=(cfg.nb,),
        in_specs=in_specs,
        out_specs=out_specs,
        out_shape=out_shape,
        scratch_shapes=[pltpu.VMEM(s, F32) for s in scratch_shapes] * 2,
        compiler_params=_cparams(("arbitrary",)),
        name=name,
    )(*args)
    ns = len(state_shapes)
    finals = [jnp.stack([outs[2 + k], outs[2 + ns + k]], axis=1) for k in range(ns)]
    return outs[0], outs[1], finals


def _per_side(run_one):
    def run(shared, sides):
        for rev, din, y_ref, scr in sides:
            run_one(rev, din, shared, y_ref, scr)
    return run


def _chunk_rows(cfg, rev):
    nch = cfg.tb // cfg.c
    return [(nch - 1 - ci if rev else ci) * cfg.c for ci in range(nch)]


def _scan_masks(c, rev):
    tpos = lax.broadcasted_iota(jnp.int32, (c, c), 0)
    spos = lax.broadcasted_iota(jnp.int32, (c, c), 1)
    return tpos, spos, ((spos >= tpos) if rev else (spos <= tpos))


def _mlstm_run(cfg, shared, sides):
    (tri_ref,) = shared
    c = cfg.c
    unit = (lax.broadcasted_iota(jnp.int32, (c, LANES), 1) == 0).astype(F32)
    nch = cfg.tb // c
    heads = range(M_HEADS)

    pre = {}
    for si, (rev, din, _, _) in enumerate(sides):
        g_ref, gt_ref = din[4], din[5]
        sel = tri_ref[int(rev)]
        for ci, r0 in enumerate(_chunk_rows(cfg, rev)):
            gc = g_ref[r0:r0 + c, :]
            gtc = gt_ref[:, r0:r0 + c]
            pre[si, ci] = dict(r0=r0, gtc=gtc,
                               bcol=_sum01(sel, gc),
                               brow=_sum01_nt(gtc[GC_LF:GC_LF + 8], sel))
    probs = [(si, ci, h) for si in range(len(sides)) for ci in range(nch) for h in heads]
    st = {}
    for si, ci, h in probs:
        q_ref, k_ref = sides[si][1][0], sides[si][1][1]
        r0 = pre[si, ci]["r0"]
        hs = slice(h * M_DH, (h + 1) * M_DH)
        q = q_ref[r0:r0 + c, hs].astype(BF16)
        k = (k_ref[r0:r0 + c, hs] * (M_DH ** -0.5)).astype(BF16)
        st[si, ci, h] = dict(q=q, qk=_dot_nt(q, k))
    for si, ci, h in probs:
        rev = sides[si][0]
        _, _, mask = _scan_masks(c, rev)
        col = int(rev) * M_HEADS + h
        e, p = st[si, ci, h], pre[si, ci]
        e["b_c"] = p["bcol"][:, GC_LF + col:GC_LF + col + 1]
        e["b_r"] = p["brow"][col:col + 1, :]
        e["i_r"] = p["gtc"][GC_IG + col:GC_IG + col + 1, :]
        e["dlog"] = jnp.where(mask, e["b_c"] - e["b_r"] + e["i_r"], NEG_BIG)
        e["rmax"] = jnp.max(e["dlog"], axis=-1, keepdims=True)

    for ci in range(nch):
        step = [(si, h) for si in range(len(sides)) for h in heads]
        for si, h in step:
            e = st[si, ci, h]
            m_sc = sides[si][3][1]
            e["m_prev"] = m_sc[h][:, 0:1]
            inter = e["b_c"] + e["m_prev"]
            e["mt"] = jnp.maximum(inter, e["rmax"])
            e["s"] = (e["qk"] * jnp.exp(e["dlog"] - e["mt"])).astype(BF16)
            e["wi"] = jnp.exp(inter - e["mt"])
        for si, h in step:
            e = st[si, ci, h]
            v_ref, c_sc = sides[si][1][2], sides[si][3][0]
            r0 = pre[si, ci]["r0"]
            hs = slice(h * M_DH, (h + 1) * M_DH)
            e["vaug"] = jnp.concatenate([v_ref[r0:r0 + c, hs], unit], axis=1).astype(BF16)
            e["caug"] = c_sc[h]
            e["numden"] = (jnp.dot(e["s"], e["vaug"], preferred_element_type=F32)
                           + e["wi"] * _dot(e["q"], e["caug"]))
        for si, h in step:
            e = st[si, ci, h]
            rev, din, y_ref, _ = sides[si]
            r0 = pre[si, ci]["r0"]
            hs = slice(h * M_DH, (h + 1) * M_DH)
            end = 0 if rev else c - 1
            den = e["numden"][:, M_DH:M_DH + 1]
            hc = e["numden"][:, :M_DH] / jnp.maximum(jnp.abs(den), jnp.exp(-e["mt"]))
            y_ref[r0:r0 + c, hs] = hc.astype(y_ref.dtype)
            e["m_end"] = e["mt"][end:end + 1, :]
            b_end = e["b_c"][end:end + 1, :]
            wk = jnp.exp(b_end - e["b_r"] + e["i_r"] - e["m_end"]) * (M_DH ** -0.5)
            e["carry"] = jnp.exp(b_end + e["m_prev"] - e["m_end"])
            e["ktw"] = (din[3][hs, r0:r0 + c].astype(F32) * wk).astype(BF16)
        for si, h in step:
            e = st[si, ci, h]
            c_sc, m_sc = sides[si][3]
            c_sc[h] = e["carry"] * e["caug"] + jnp.dot(e["ktw"], e["vaug"], preferred_element_type=F32)
            m_sc[h] = jnp.broadcast_to(e["m_end"], (1, LANES))


def _copy_state(src, dst):
    for s, d in zip(src, dst):
        d[...] = s[...]


def _mlstm(cfg, p, kt, g, gt, tri, c0, m0):
    tb = cfg.tb
    qi = OFF_M // M_W
    dir_inputs = [(p, (tb, M_W), lambda d, b: (b, qi)),
                  (p, (tb, M_W), lambda d, b: (b, qi + 1)),
                  (p, (tb, M_W), lambda d, b: (b, qi + 2)),
                  (kt, (M_W, tb), lambda d, b: (0, b)),
                  (g, (tb, LANES), lambda d, b: (b, 0)),
                  (gt, (LANES, tb), lambda d, b: (0, b))]
    shapes = [(M_HEADS, M_DH, 2 * M_DH), (M_HEADS, 1, LANES)]
    return _bidir_call(cfg, "mlstm", M_W, dir_inputs, [tri], [c0, m0], shapes, shapes,
                       functools.partial(_mlstm_run, cfg), _copy_state, _copy_state)


def _mid_rows(g, lev, rev):
    c, w = g.shape
    half = 1 << lev
    blk = 2 * half
    m = half if rev else half - 1
    if blk >= 8:
        g3 = g.reshape(c // blk, blk, w)
        return jnp.broadcast_to(g3[:, m:m + 1, :], g3.shape).reshape(c, w)
    g3 = g.reshape(c // 8, 8, w)
    sub = lax.broadcasted_iota(jnp.int32, g3.shape, 1)
    out = jnp.broadcast_to(g3[:, m:m + 1, :], g3.shape)
    for b0 in range(blk, 8, blk):
        out = jnp.where(sub >= b0, jnp.broadcast_to(g3[:, b0 + m:b0 + m + 1, :], g3.shape), out)
    return out.reshape(c, w)


def _hgrn_run(cfg, rev, din, shared, y_ref, scr):
    q_ref, v_ref, vt_ref, f_ref, lb_ref = din
    (tri_ref,) = shared
    (s_sc,) = scr
    c = cfg.c
    levels = int(np.log2(c))
    dd = 1 if rev else 0
    sel = tri_ref[dd]
    tpos, spos, _ = _scan_masks(c, rev)
    before = (spos > tpos) if rev else (spos < tpos)
    pair = jnp.where(before, tpos ^ spos, 0)
    ones = jnp.ones((H_DK, LANES), BF16)
    end = 0 if rev else c - 1
    lb = lb_ref[0]
    for r0 in _chunk_rows(cfg, rev):
        f = lb + (1.0 - lb) * _sigmoid(f_ref[r0:r0 + c, :])
        kk_all = 1.0 - f
        g_all = _sum01(sel, jnp.log(f))
        for h in range(H_HEADS):
            hs = slice(h * H_DK, (h + 1) * H_DK)
            q = q_ref[r0:r0 + c, hs]
            v = v_ref[r0:r0 + c, hs]
            kk = kk_all[:, hs]
            gcum = g_all[:, hs]
            att = jnp.zeros((c, c), F32)
            for lev in range(levels):
                x = jnp.exp(-jnp.abs(gcum - _mid_rows(gcum, lev, rev)))
                a = _dot_nt(q * x, kk * x)
                att = jnp.where((pair >> lev) == 1, a, att)
            diag = jnp.dot((q * kk).astype(BF16), ones, preferred_element_type=F32)
            st = s_sc[h]
            o = _dot(att, v) + diag * v + _dot_nt(q * jnp.exp(gcum), st)
            y_ref[r0:r0 + c, hs] = o.astype(y_ref.dtype)
            g_end = gcum[end:end + 1, :]
            vt = vt_ref[hs, r0:r0 + c]
            s_sc[h] = jnp.exp(g_end) * st + _dot(vt, kk * jnp.exp(g_end - gcum))


def _hgrn_load(sin, scr):
    for h in range(H_HEADS):
        scr[0][h] = sin[0][h].T


def _hgrn_store(scr, sout):
    for h in range(H_HEADS):
        sout[0][h] = scr[0][h].T


def _hgrn(cfg, p, vt, lb, tri, s0):
    tb = cfg.tb
    qi = OFF_H // H_W
    dir_inputs = [(p, (tb, H_W), lambda d, b: (b, qi)),
                  (p, (tb, H_W), lambda d, b: (b, qi + 1)),
                  (vt, (H_W, tb), lambda d, b: (0, b)),
                  (p, (tb, H_W), lambda d, b: (b, qi + 3 + d)),
                  (lb, (1, 1, H_W), lambda d, b: (d, 0, 0))]
    return _bidir_call(cfg, "hgrn", H_W, dir_inputs, [tri], [s0], [(H_HEADS, H_DK, H_DV)],
                       [(H_HEADS, H_DV, H_DK)], _per_side(functools.partial(_hgrn_run, cfg)), _hgrn_load,
                       _hgrn_store)


def _ssd_run(cfg, rev, din, shared, y_ref, scr):
    x_ref, cm_ref, bmt_ref, g_ref, gt_ref = din
    tri_ref, ex_ref, sd_ref = shared
    (h_sc,) = scr
    c = cfg.c
    n_rep = S_HEADS // S_GROUPS
    gw = n_rep * S_P
    dd = 1 if rev else 0
    sel = tri_ref[dd]
    ex = ex_ref[dd]
    _, _, mask = _scan_masks(c, rev)
    lane = lax.broadcasted_iota(jnp.int32, (c, LANES), 1)
    la0 = GC_DTA + dd * S_HEADS
    is_la = (lane - la0).astype(jnp.uint32) < S_HEADS
    head_of_lane = lax.broadcasted_iota(jnp.int32, (c, gw), 1) // S_P
    end = 0 if rev else c - 1
    for r0 in _chunk_rows(cfg, rev):
        gc = g_ref[r0:r0 + c, :]
        gtc = gt_ref[:, r0:r0 + c]
        lam = jnp.where(is_la, _sum01(sel, gc), 0.0)
        larow = _sum01_nt(gtc[la0:la0 + S_HEADS], sel)
        dtrow = gtc[GC_DT + dd * S_HEADS:GC_DT + (dd + 1) * S_HEADS]
        la_end = lam[end:end + 1, :]
        ea = _expand2(jnp.exp(lam), ex)
        wk = jnp.exp(la_end - lam) * pltpu.roll(gc, GC_DTA - GC_DT, 1)
        xall = x_ref[r0:r0 + c, :]
        xw = (xall * _expand2(wk, ex)).astype(BF16)
        for grp in range(S_GROUPS):
            gs = slice(grp * gw, (grp + 1) * gw)
            cmg = cm_ref[r0:r0 + c, grp * S_N:(grp + 1) * S_N].astype(BF16)
            bmt = bmt_ref[grp * S_N:(grp + 1) * S_N, r0:r0 + c]
            cb = jnp.dot(cmg, bmt, preferred_element_type=F32)
            xg = xall[:, gs]
            lhs, rhs = [], []
            for r in range(n_rep):
                h = grp * n_rep + r
                la_c = lam[:, la0 + h:la0 + h + 1]
                seg = jnp.where(mask, jnp.exp(jnp.where(mask, la_c - larow[h:h + 1, :], 0.0))
                                * dtrow[h:h + 1, :], 0.0)
                lhs.append((seg * cb).astype(BF16))
                rhs.append(jnp.where(head_of_lane == r, xg, 0.0).astype(BF16))
            ht = h_sc[grp]
            y = (jnp.dot(jnp.concatenate(lhs, axis=1), jnp.concatenate(rhs, axis=0),
                         preferred_element_type=F32)
                 + ea[:, gs] * jnp.dot(cmg, ht.astype(BF16), preferred_element_type=F32))
            if not rev:
                y = y + sd_ref[:, gs] * xg
            y_ref[r0:r0 + c, gs] = y.astype(y_ref.dtype)
            h_sc[grp] = ea[end:end + 1, gs] * ht + jnp.dot(bmt, xw[:, gs], preferred_element_type=F32)


def _ssd_load(sin, scr):
    n_rep = S_HEADS // S_GROUPS
    for grp in range(S_GROUPS):
        scr[0][grp] = sin[0][grp * n_rep:(grp + 1) * n_rep].reshape(n_rep * S_P, S_N).T


def _ssd_store(scr, sout):
    n_rep = S_HEADS // S_GROUPS
    for grp in range(S_GROUPS):
        sout[0][grp * n_rep:(grp + 1) * n_rep] = scr[0][grp].T.reshape(n_rep, S_P, S_N)


def _ssd(cfg, p, bmt, g, gt, tri, ex, sd, h0):
    tb = cfg.tb
    gn = S_GROUPS * S_N
    dir_inputs = [(p, (tb, S_W), lambda d, b: (b, OFF_XBC // S_W)),
                  (p, (tb, gn), lambda d, b: (b, (OFF_XBC + S_W) // gn + 1)),
                  (bmt, (gn, tb), lambda d, b: (0, b)),
                  (g, (tb, LANES), lambda d, b: (b, 0)),
                  (gt, (LANES, tb), lambda d, b: (0, b))]
    return _bidir_call(cfg, "ssd", S_W, dir_inputs, [tri, ex, sd], [h0], [(S_HEADS, S_P, S_N)],
                       [(S_GROUPS, S_N, (S_HEADS // S_GROUPS) * S_P)], _per_side(functools.partial(_ssd_run, cfg)),
                       _ssd_load, _ssd_store)


def _head_norm(y, heads, width):
    outs = []
    for h in range(heads):
        s = y[:, h * width:(h + 1) * width]
        outs.append(s * lax.rsqrt(jnp.mean(s * s, axis=-1, keepdims=True) + EPS))
    return jnp.concatenate(outs, axis=1)


def _merge_kernel(x_ref, gate_ref, mo_ref, hg_ref, sz_ref, bg_ref, ymf_ref, ymb_ref, yhf_ref, yhb_ref,
                  ysf_ref, ysb_ref, mn_ref, hn_ref, sn_ref, wbm_ref, wbh_ref, wbs_ref, wo_ref, o_ref,
                  *, d_model):
    add = lambda a, b: a[...].astype(F32) + b[...].astype(F32)
    y_m = _head_norm(add(ymf_ref, ymb_ref), M_HEADS, M_DH) * mn_ref[...] * _sigmoid(mo_ref[...])
    y_h = _head_norm(add(yhf_ref, yhb_ref), H_HEADS, H_DV) * hn_ref[...] * _silu(hg_ref[...])
    ys = add(ysf_ref, ysb_ref) * _silu(sz_ref[...])
    y_s = ys * lax.rsqrt(jnp.mean(ys * ys, axis=-1, keepdims=True) + EPS) * sn_ref[...]
    bg = bg_ref[...]
    dm = d_model
    merged = (_sigmoid(bg[:, :dm]) * jnp.dot(y_m.astype(BF16), wbm_ref[...], preferred_element_type=F32)
              + _sigmoid(bg[:, dm:2 * dm]) * jnp.dot(y_h.astype(BF16), wbh_ref[...], preferred_element_type=F32)
              + _sigmoid(bg[:, 2 * dm:]) * jnp.dot(y_s.astype(BF16), wbs_ref[...], preferred_element_type=F32))
    out = jnp.dot(merged.astype(BF16), wo_ref[...], preferred_element_type=F32)
    o_ref[...] = x_ref[...] + gate_ref[0] * out


def _merge(cfg, x, mod, p, ym, yh, ys, mn, hn, sn, wbm, wbh, wbs, wo):
    tm = cfg.tm_merge
    row = cfg.mod_row(tm)
    rows_pad = mod.shape[0] // 6
    d = cfg.d
    const = lambda shape: pl.BlockSpec(shape, lambda m: (0,) * len(shape))
    tok = lambda w: pl.BlockSpec((tm, w), lambda m: (m, 0))
    return pl.pallas_call(
        functools.partial(_merge_kernel, d_model=d),
        grid=(cfg.t // tm,),
        in_specs=[tok(d),
                  pl.BlockSpec((1, 1, d), lambda m: (2 * rows_pad + row(m), 0, 0)),
                  pl.BlockSpec((tm, M_W), lambda m: (m, (OFF_M + 3 * M_W) // M_W)),
                  pl.BlockSpec((tm, H_W), lambda m: (m, (OFF_H + 2 * H_W) // H_W)),
                  pl.BlockSpec((tm, S_W), lambda m: (m, OFF_SZ // S_W)),
                  pl.BlockSpec((tm, 3 * d), lambda m: (m, OFF_BG // (3 * d))),
                  tok(M_W), tok(M_W), tok(H_W), tok(H_W), tok(S_W), tok(S_W),
                  const((1, M_W)), const((1, H_W)), const((1, S_W)),
                  const((M_W, d)), const((H_W, d)), const((S_W, d)), const((d, d))],
        out_specs=tok(d),
        out_shape=jax.ShapeDtypeStruct((cfg.t, d), F32),
        compiler_params=_cparams(("arbitrary",)),
        name="merge",
    )(x, mod, p, p, p, p, ym[0], ym[1], yh[0], yh[1], ys[0], ys[1], mn, hn, sn, wbm, wbh, wbs, wo)


def _ffn_up_kernel(x_ref, sc_ref, sh_ref, nw_ref, wa_ref, wb_ref, o_ref, *, d_ff, tc):
    x = x_ref[...]
    y = x * lax.rsqrt(jnp.mean(x * x, axis=-1, keepdims=True) + EPS) * nw_ref[...]
    h = (y * (1.0 + sc_ref[0]) + sh_ref[0]).astype(BF16)
    for j in range(d_ff // tc):
        sl = slice(j * tc, (j + 1) * tc)
        a = jnp.dot(h, wa_ref[:, sl], preferred_element_type=F32)
        b = jnp.dot(h, wb_ref[:, sl], preferred_element_type=F32)
        o_ref[:, sl] = (_silu(a) * b).astype(BF16)


def _ffn_up(cfg, x, mod, nw, w_gu):
    tm = cfg.tm_ffn
    row = cfg.mod_row(tm)
    rows_pad = mod.shape[0] // 6
    d, d_ff = cfg.d, cfg.d_ff
    return pl.pallas_call(
        functools.partial(_ffn_up_kernel, d_ff=d_ff, tc=MXU_N),
        grid=(cfg.t // tm,),
        in_specs=[pl.BlockSpec((tm, d), lambda m: (m, 0)),
                  pl.BlockSpec((1, 1, d), lambda m: (4 * rows_pad + row(m), 0, 0)),
                  pl.BlockSpec((1, 1, d), lambda m: (3 * rows_pad + row(m), 0, 0)),
                  pl.BlockSpec((1, d), lambda m: (0, 0)),
                  pl.BlockSpec((d, d_ff), lambda m: (0, 0)),
                  pl.BlockSpec((d, d_ff), lambda m: (0, 1))],
        out_specs=pl.BlockSpec((tm, d_ff), lambda m: (m, 0)),
        out_shape=jax.ShapeDtypeStruct((cfg.t, d_ff), BF16),
        compiler_params=_cparams(("arbitrary",)),
        name="ffn_up",
    )(x, mod, mod, nw, w_gu, w_gu)


def _ffn_down_kernel(x_ref, gate_ref, a_ref, w_ref, o_ref):
    o_ref[...] = x_ref[...] + gate_ref[0] * jnp.dot(a_ref[...], w_ref[...], preferred_element_type=F32)


def _ffn_down_final_kernel(x_ref, gate_ref, a_ref, w_ref, nf_ref, op_ref, os_ref, *, nct):
    m = pl.program_id(0)
    x = x_ref[...] + gate_ref[0] * jnp.dot(a_ref[...], w_ref[...], preferred_element_type=F32)
    y = x * lax.rsqrt(jnp.mean(x * x, axis=-1, keepdims=True) + EPS) * nf_ref[...]

    @pl.when(m < nct)
    def _():
        op_ref[...] = y

    @pl.when(m >= nct)
    def _():
        os_ref[...] = y


def _ffn_down(cfg, x, mod, act, w_down, norm_f=None):
    tm = cfg.tm_ffn
    row = cfg.mod_row(tm)
    rows_pad = mod.shape[0] // 6
    d, d_ff = cfg.d, cfg.d_ff
    in_specs = [pl.BlockSpec((tm, d), lambda m: (m, 0)),
                pl.BlockSpec((1, 1, d), lambda m: (5 * rows_pad + row(m), 0, 0)),
                pl.BlockSpec((tm, d_ff), lambda m: (m, 0)),
                pl.BlockSpec((d_ff, d), lambda m: (0, 0))]
    if norm_f is None:
        return pl.pallas_call(
            _ffn_down_kernel,
            grid=(cfg.t // tm,),
            in_specs=in_specs,
            out_specs=pl.BlockSpec((tm, d), lambda m: (m, 0)),
            out_shape=jax.ShapeDtypeStruct((cfg.t, d), F32),
            compiler_params=_cparams(("arbitrary",)),
            name="ffn_down",
        )(x, mod, act, w_down)
    nct = cfg.t_ctx // tm
    return pl.pallas_call(
        functools.partial(_ffn_down_final_kernel, nct=nct),
        grid=(cfg.t // tm,),
        in_specs=in_specs + [pl.BlockSpec((1, d), lambda m: (0, 0))],
        out_specs=[pl.BlockSpec((tm, d), lambda m: (jnp.minimum(m, nct - 1), 0)),
                   pl.BlockSpec((tm, d), lambda m: (jnp.maximum(m - nct, 0), 0))],
        out_shape=[jax.ShapeDtypeStruct((cfg.t_ctx, d), F32),
                   jax.ShapeDtypeStruct((cfg.t_smp, d), F32)],
        compiler_params=_cparams(("arbitrary",)),
        name="ffn_down_final",
    )(x, mod, act, w_down, norm_f)


def _grid_pos_embed(rows, d_model):
    quarter = d_model // 4
    freq = POS_BASE ** (-jnp.arange(quarter, dtype=F32) / quarter)
    r = jnp.arange(rows, dtype=F32)[:, None] * freq
    cl = jnp.arange(GRID_W, dtype=F32)[:, None] * freq
    row_e = jnp.concatenate([jnp.sin(r), jnp.cos(r)], axis=-1)
    col_e = jnp.concatenate([jnp.sin(cl), jnp.cos(cl)], axis=-1)
    emb = jnp.concatenate([jnp.broadcast_to(row_e[:, None], (rows, GRID_W, d_model // 2)),
                           jnp.broadcast_to(col_e[None], (rows, GRID_W, d_model // 2))], axis=-1)
    return emb.reshape(rows * GRID_W, d_model)


def kernel(x_prompt, x_sample, state_mlstm_c, state_mlstm_n, state_mlstm_m, state_hgrn, state_ssm, c, c_ctx, w_ada, b_ada, norm1, norm2, w_in, m_bi, m_bf, m_norm, h_lb, h_norm, s_conv_w, s_conv_b, s_dt_bias, s_a_log, s_d, s_norm, w_bm, w_bh, w_bs, w_out, w_gu, w_down, norm_f):
    n_ctx, l_ctx, d = x_prompt.shape
    n_smp, l_smp, _ = x_sample.shape
    depth = w_in.shape[0]
    d_ff = w_down.shape[1]
    cfg = _Cfg(n_ctx, l_ctx, n_smp, l_smp, d, d_ff)

    tri = _tri_consts(cfg.c)
    ex = _expand_consts()

    rows_pad = -(-(n_smp + 1) // 8) * 8
    cvec = jnp.zeros((rows_pad, d), F32).at[:n_smp].set(c).at[n_smp].set(c_ctx)
    mod_all = _ada(cvec, w_ada, b_ada)
    mod_all = mod_all.reshape(depth, rows_pad, 6, d).transpose(0, 2, 1, 3).reshape(depth, 6 * rows_pad, 1, d)

    p_lb = jax.nn.softmax(h_lb.astype(F32), axis=1)
    lower = jnp.cumsum(p_lb, axis=1) - p_lb[:, :1]

    pos = _grid_pos_embed(l_smp // GRID_W, d)
    x = _embed(cfg, x_prompt.reshape(cfg.t_ctx, d), x_sample.reshape(cfg.t_smp, d), pos)

    a_neg = -jnp.exp(s_a_log.astype(F32))
    new_c, new_n, new_m, new_h, new_s = [], [], [], [], []
    y_p = y_s = None
    for i in range(depth):
        w = w_in[i]
        o_mi, o_hq, o_sz, o_xbc, o_dt, o_bg = 2048, 2064, 4624, 5648, 7696, 7728
        w_main = jnp.concatenate([w[:, o_xbc:o_dt], w[:, o_sz:o_xbc], w[:, o_bg:], w[:, :o_mi],
                                  w[:, o_hq:o_sz]], axis=1).astype(BF16)
        w_small = jnp.concatenate([w[:, o_mi:o_hq], w[:, o_dt:o_bg], w[:, o_dt:o_bg],
                                   jnp.zeros((d, LANES - GC_DTA - 2 * S_HEADS), F32)], axis=1).astype(BF16)
        dtb = s_dt_bias[i].reshape(-1)
        gbias = jnp.concatenate([m_bi[i].reshape(-1), m_bf[i].reshape(-1), dtb, dtb,
                                 jnp.zeros((LANES - GC_DTA - 2 * S_HEADS,), F32)]).reshape(1, LANES)
        gmul = jnp.concatenate([jnp.ones((GC_DTA,), F32), a_neg[i].reshape(-1),
                                jnp.zeros((LANES - GC_DTA - 2 * S_HEADS,), F32)]).reshape(1, LANES)
        ncw = -(-CONV_CH // PROJ_TN) * PROJ_TN
        cw = jnp.zeros((3, ncw), F32).at[:, :CONV_CH].set(s_conv_w[i]).reshape(3, ncw // PROJ_TN, PROJ_TN)
        cw = cw.transpose(1, 0, 2)
        cb = jnp.zeros((ncw,), F32).at[:CONV_CH].set(s_conv_b[i]).reshape(ncw // PROJ_TN, 1, PROJ_TN)
        mod = mod_all[i]

        p, g, gt, bmt, kt, vt = _inproj(cfg, x, mod, norm1[i].reshape(1, d), w_main, w_small, gbias, gmul,
                                        cw, cb)

        c0 = jnp.concatenate([state_mlstm_c[:, i], state_mlstm_n[:, i][..., None],
                              jnp.zeros(state_mlstm_c[:, i].shape[:-1] + (M_DH - 1,), F32)], axis=-1)
        m0 = jnp.broadcast_to(state_mlstm_m[:, i][..., None, None], (n_smp, 2, M_HEADS, 1, LANES))
        ymf, ymb, (c_fin, m_fin) = _mlstm(cfg, p, kt, g, gt, tri, c0, m0)
        yhf, yhb, (h_fin,) = _hgrn(cfg, p, vt, lower[:, i].reshape(2, 1, H_W), tri, state_hgrn[:, i])
        sd = jnp.repeat(s_d[i].astype(F32), S_P).reshape(1, S_W)
        ysf, ysb, (s_fin,) = _ssd(cfg, p, bmt, g, gt, tri, ex, sd, state_ssm[:, i])

        x = _merge(cfg, x, mod, p, (ymf, ymb), (yhf, yhb), (ysf, ysb), m_norm[i].reshape(1, M_W),
                   h_norm[i].reshape(1, H_W), s_norm[i].reshape(1, S_W), w_bm[i].astype(BF16),
                   w_bh[i].astype(BF16), w_bs[i].astype(BF16), w_out[i].astype(BF16))
        act = _ffn_up(cfg, x, mod, norm2[i].reshape(1, d), w_gu[i].astype(BF16))
        if i + 1 < depth:
            x = _ffn_down(cfg, x, mod, act, w_down[i].astype(BF16))
        else:
            y_p, y_s = _ffn_down(cfg, x, mod, act, w_down[i].astype(BF16), norm_f.reshape(1, d))

        new_c.append(c_fin[..., :M_DH])
        new_n.append(c_fin[..., M_DH])
        new_m.append(m_fin[..., 0, 0])
        new_h.append(h_fin)
        new_s.append(s_fin)

    return (y_p.reshape(n_ctx, l_ctx, d), y_s.reshape(n_smp, l_smp, d),
            jnp.stack(new_c, axis=1), jnp.stack(new_n, axis=1), jnp.stack(new_m, axis=1),
            jnp.stack(new_h, axis=1), jnp.stack(new_s, axis=1))
```

```python
import functools

import numpy as np
import jax
import jax.numpy as jnp
from jax import lax
from jax.experimental import pallas as pl
from jax.experimental.pallas import tpu as pltpu

F32 = jnp.float32
BF16 = jnp.bfloat16

GRID_W = 64
EPS = 1e-6
POS_BASE = 10000.0
NEG_BIG = -1e30
M_HEADS, M_DH = 4, 128
H_HEADS, H_DK, H_DV = 4, 128, 128
S_HEADS, S_P, S_GROUPS, S_N = 16, 64, 4, 128
M_W = M_HEADS * M_DH
H_W = H_HEADS * H_DV
S_W = S_HEADS * S_P
CONV_CH = S_W + 2 * S_GROUPS * S_N

LANES = 128
MXU_N = 256
VMEM_LIMIT = 56 * 1024 * 1024
SCAN_CHUNK = 128
SCAN_BLOCK = 256
PROJ_TN = 1536

OFF_XBC, OFF_SZ, OFF_BG, OFF_M, OFF_H = 0, 2048, 3072, 6144, 8192
N_MAIN = 10752
GC_IG, GC_LF, GC_DT, GC_DTA = 0, 8, 16, 48


def _cparams(sem):
    return pltpu.CompilerParams(dimension_semantics=sem, vmem_limit_bytes=VMEM_LIMIT)


def _sigmoid(x):
    return 1.0 / (1.0 + jnp.exp(-x))


def _silu(x):
    return x * _sigmoid(x)


def _softplus(x):
    return jnp.maximum(x, 0.0) + jnp.log(1.0 + jnp.exp(-jnp.abs(x)))


def _dot(a, b):
    return jnp.dot(a.astype(BF16), b.astype(BF16), preferred_element_type=F32)


def _dot_nt(a, b):
    return lax.dot_general(a.astype(BF16), b.astype(BF16), (((1,), (1,)), ((), ())),
                           preferred_element_type=F32)


def _split3(x):
    x1 = x.astype(BF16)
    r1 = x - x1.astype(F32)
    x2 = r1.astype(BF16)
    x3 = (r1 - x2.astype(F32)).astype(BF16)
    return x1, x2, x3


def _sum01(sel, x):
    x1, x2, x3 = _split3(x)
    d = functools.partial(jnp.dot, preferred_element_type=F32)
    return d(sel, x1) + d(sel, x2) + d(sel, x3)


def _sum01_nt(x, sel):
    x1, x2, x3 = _split3(x)
    d = lambda a: lax.dot_general(a, sel, (((1,), (1,)), ((), ())), preferred_element_type=F32)
    return d(x1) + d(x2) + d(x3)


def _expand2(x, e):
    x1 = x.astype(BF16)
    x2 = (x - x1.astype(F32)).astype(BF16)
    return jnp.dot(x1, e, preferred_element_type=F32) + jnp.dot(x2, e, preferred_element_type=F32)


class _Cfg:
    def __init__(self, n_ctx, l_ctx, n_smp, l_smp, d_model, d_ff):
        self.n_ctx, self.l_ctx, self.n_smp, self.l_smp = n_ctx, l_ctx, n_smp, l_smp
        self.d, self.d_ff = d_model, d_ff
        self.t_ctx = n_ctx * l_ctx
        self.t_smp = n_smp * l_smp
        self.t = self.t_ctx + self.t_smp
        self.tb = SCAN_BLOCK
        self.c = SCAN_CHUNK
        assert l_ctx % self.tb == 0 and l_smp % self.tb == 0 and self.tb % self.c == 0
        assert l_ctx & (l_ctx - 1) == 0 and l_smp % GRID_W == 0
        self.nbc = l_ctx // self.tb
        self.nbs = l_smp // self.tb
        self.gc0 = self.t_ctx // self.tb
        self.nb = self.t // self.tb
        self.tm = 1024 if (self.t_ctx % 1024 == 0 and self.t_smp % 1024 == 0 and l_smp % 1024 == 0) else 256
        assert self.t_ctx % self.tm == 0 and l_smp % self.tm == 0 and self.tm % l_ctx == 0
        self.tm_merge = 256
        self.tm_ffn = 512 if self.tm == 1024 else 256

    def mod_row(self, tm):
        nct = self.t_ctx // tm
        per = self.l_smp // tm

        def f(m):
            return jnp.where(m < nct, self.n_smp, jnp.maximum(m - nct, 0) // per)
        return f

    def blk(self, d, j):
        return j + d * (self.nb - 1 - 2 * j)

    def scan_flags(self, d, j):
        g = self.blk(d, j)
        is_ctx = g < self.gc0
        gs = jnp.maximum(g - self.gc0, 0)
        pos = jnp.where(is_ctx, lax.rem(g, self.nbc), lax.rem(gs, self.nbs))
        nblk = jnp.where(is_ctx, self.nbc, self.nbs)
        spos = pos + d * (nblk - 1 - 2 * pos)
        return is_ctx, spos == 0, spos == nblk - 1

    def smp_seq(self, d, j):
        return jnp.clip((self.blk(d, j) - self.gc0) // self.nbs, 0, self.n_smp - 1)

    def ctx_seq(self, d, j):
        return jnp.minimum(self.blk(d, j) // self.nbc, self.n_ctx - 1)


def _tri_consts(c):
    t = np.arange(c)
    fwd = (t[None, :] <= t[:, None]).astype(np.float32)
    return jnp.asarray(np.stack([fwd, fwd.T]), dtype=BF16)


def _expand_consts():
    e = np.zeros((2, LANES, S_W), np.float32)
    for dd in range(2):
        for h in range(S_HEADS):
            e[dd, GC_DTA + dd * S_HEADS + h, h * S_P:(h + 1) * S_P] = 1.0
    return jnp.asarray(e, dtype=BF16)


def _embed_kernel(xp_ref, xs_ref, pos_ref, o_ref, *, nct):
    m = pl.program_id(0)

    @pl.when(m < nct)
    def _():
        o_ref[...] = xp_ref[...]

    @pl.when(m >= nct)
    def _():
        o_ref[...] = xs_ref[...] + pos_ref[...]


def _embed(cfg, xp2, xs2, pos):
    tm = cfg.tm
    nct = cfg.t_ctx // tm
    per = cfg.l_smp // tm
    return pl.pallas_call(
        functools.partial(_embed_kernel, nct=nct),
        grid=(cfg.t // tm,),
        in_specs=[pl.BlockSpec((tm, cfg.d), lambda m: (jnp.minimum(m, nct - 1), 0)),
                  pl.BlockSpec((tm, cfg.d), lambda m: (jnp.maximum(m - nct, 0), 0)),
                  pl.BlockSpec((tm, cfg.d), lambda m: (lax.rem(jnp.maximum(m - nct, 0), per), 0))],
        out_specs=pl.BlockSpec((tm, cfg.d), lambda m: (m, 0)),
        out_shape=jax.ShapeDtypeStruct((cfg.t, cfg.d), F32),
        compiler_params=_cparams(("arbitrary",)),
        name="embed",
    )(xp2, xs2, pos)


def _ada_kernel(c_ref, w_ref, b_ref, o_ref):
    o_ref[0] = _dot(_silu(c_ref[...]), w_ref[0]) + b_ref[0]


def _ada(cvec, w_ada, b_ada):
    depth, d, n = w_ada.shape
    rows = cvec.shape[0]
    tn = 1536
    return pl.pallas_call(
        _ada_kernel,
        grid=(depth, n // tn),
        in_specs=[pl.BlockSpec((rows, d), lambda l, j: (0, 0)),
                  pl.BlockSpec((1, d, tn), lambda l, j: (l, 0, j)),
                  pl.BlockSpec((1, 1, tn), lambda l, j: (l, 0, j))],
        out_specs=pl.BlockSpec((1, rows, tn), lambda l, j: (l, 0, j)),
        out_shape=jax.ShapeDtypeStruct((depth, rows, n), F32),
        compiler_params=_cparams(("arbitrary", "arbitrary")),
        name="ada",
    )(cvec, w_ada, b_ada.reshape(depth, 1, n))


def _inproj_kernel(x_ref, sc_ref, sh_ref, nw_ref, w_ref, ws_ref, gb_ref, ga_ref, cw_ref, cb_ref,
                   p_ref, g_ref, gt_ref, bmt_ref, kt_ref, vt_ref, xn_ref, *, tm, tn, nct, seg_ctx):
    m = pl.program_id(0)
    n = pl.program_id(1)
    gn = S_GROUPS * S_N
    nt = N_MAIN // tn

    def prologue():
        x = x_ref[...]
        y = x * lax.rsqrt(jnp.mean(x * x, axis=-1, keepdims=True) + EPS) * nw_ref[...]
        h = (y * (1.0 + sc_ref[0]) + sh_ref[0]).astype(BF16)
        xn_ref[...] = h
        raw = jnp.dot(h, ws_ref[...], preferred_element_type=F32) + gb_ref[...]
        lane = lax.broadcasted_iota(jnp.int32, raw.shape, 1)
        sp = _softplus(raw)
        g = jnp.where(lane < GC_LF, raw,
                      jnp.where(lane < GC_DT, -_softplus(-raw),
                                jnp.where(lane < GC_DTA, sp, sp * ga_ref[...])))
        g_ref[...] = g
        gt_ref[...] = g.T

    def matmul_cols(c0, c1):
        return jnp.dot(xn_ref[...], w_ref[:, c0:c1], preferred_element_type=F32)

    def conv_tile(tile, ncols):
        row = lax.broadcasted_iota(jnp.int32, (tm, LANES), 0)
        seg_mask = jnp.where(m < nct, seg_ctx - 1, GRID_W - 1)
        inseg = row & seg_mask
        has_prev = inseg != 0
        has_next = inseg != seg_mask
        cw = cw_ref[0]
        if ncols < tn:
            p_ref[:, ncols:] = matmul_cols(ncols, tn)
        accs = [matmul_cols(c0, c0 + MXU_N) for c0 in range(0, ncols, MXU_N)]
        for c0 in range(0, ncols, MXU_N):
            acc = accs[c0 // MXU_N]
            for cb in range(MXU_N // LANES):
                lo = c0 + cb * LANES
                sl = slice(lo, lo + LANES)
                u = acc[:, cb * LANES:(cb + 1) * LANES]
                up = jnp.where(has_prev, pltpu.roll(u, 1, 0), 0.0)
                un = jnp.where(has_next, pltpu.roll(u, tm - 1, 0), 0.0)
                v = _silu(cw[0:1, sl] * up + cw[1:2, sl] * u + cw[2:3, sl] * un + cb_ref[0][:, sl])
                p_ref[:, sl] = v
                col = tile * tn + lo - S_W
                if 0 <= col < gn:
                    bmt_ref[col:col + LANES, :] = v.T.astype(BF16)

    def plain_tile(transposed):
        acc = matmul_cols(0, tn)
        p_ref[...] = acc
        for dst_ref, off in transposed:
            for cb in range(M_W // LANES):
                blk = acc[:, off + cb * LANES:off + (cb + 1) * LANES]
                dst_ref[cb * LANES:(cb + 1) * LANES, :] = blk.T.astype(BF16)

    n_conv = -(-CONV_CH // tn)
    special = {}
    for dst_ref, col in ((kt_ref, OFF_M + M_W), (vt_ref, OFF_H + H_W)):
        tile, off = divmod(col, tn)
        assert off + M_W <= tn and tile >= n_conv
        special.setdefault(tile, []).append((dst_ref, off))

    for tile in range(nt):
        @pl.when(n == tile)
        def _(tile=tile):
            if tile == 0:
                prologue()
            if tile < n_conv:
                conv_tile(tile, min(tn, CONV_CH - tile * tn))
            else:
                plain_tile(special.get(tile, []))


def _inproj(cfg, x, mod, nw, w_main, w_small, gbias, gmul, cw, cb):
    tm, tn = cfg.tm, PROJ_TN
    nct = cfg.t_ctx // tm
    row = cfg.mod_row(tm)
    rows_pad = mod.shape[0] // 6
    nt = N_MAIN // tn
    kern = functools.partial(_inproj_kernel, tm=tm, tn=tn, nct=nct, seg_ctx=cfg.l_ctx)
    tspec = pl.BlockSpec((M_W, tm), lambda m, n: (0, m))
    tshape = jax.ShapeDtypeStruct((M_W, cfg.t), BF16)
    return pl.pallas_call(
        kern,
        grid=(cfg.t // tm, nt),
        in_specs=[pl.BlockSpec((tm, cfg.d), lambda m, n: (m, 0)),
                  pl.BlockSpec((1, 1, cfg.d), lambda m, n: (1 * rows_pad + row(m), 0, 0)),
                  pl.BlockSpec((1, 1, cfg.d), lambda m, n: (0 * rows_pad + row(m), 0, 0)),
                  pl.BlockSpec((1, cfg.d), lambda m, n: (0, 0)),
                  pl.BlockSpec((cfg.d, tn), lambda m, n: (0, n)),
                  pl.BlockSpec((cfg.d, LANES), lambda m, n: (0, 0)),
                  pl.BlockSpec((1, LANES), lambda m, n: (0, 0)),
                  pl.BlockSpec((1, LANES), lambda m, n: (0, 0)),
                  pl.BlockSpec((1, 3, tn), lambda m, n: (jnp.minimum(n, 1), 0, 0)),
                  pl.BlockSpec((1, 1, tn), lambda m, n: (jnp.minimum(n, 1), 0, 0))],
        out_specs=[pl.BlockSpec((tm, tn), lambda m, n: (m, n)),
                   pl.BlockSpec((tm, LANES), lambda m, n: (m, 0)),
                   pl.BlockSpec((LANES, tm), lambda m, n: (0, m)),
                   tspec, tspec, tspec],
        out_shape=[jax.ShapeDtypeStruct((cfg.t, N_MAIN), F32),
                   jax.ShapeDtypeStruct((cfg.t, LANES), F32),
                   jax.ShapeDtypeStruct((LANES, cfg.t), F32),
                   tshape, tshape, tshape],
        scratch_shapes=[pltpu.VMEM((tm, cfg.d), BF16)],
        compiler_params=_cparams(("arbitrary", "arbitrary")),
        name="inproj",
    )(x, mod, mod, nw, w_main, w_small, gbias, gmul, cw, cb)


def _bidir_kernel(*refs, cfg, n_dir, n_shared, n_state, n_scratch, run, load_state, store_state):
    it = iter(refs)
    take = lambda k: [next(it) for _ in range(k)]
    din = [take(n_dir), take(n_dir)]
    shared = take(n_shared)
    sin = [take(n_state), take(n_state)]
    yout = [take(1)[0], take(1)[0]]
    sout = [take(n_state), take(n_state)]
    scr = [take(n_scratch), take(n_scratch)]
    j = pl.program_id(0)
    flags = [cfg.scan_flags(d, j) for d in (0, 1)]

    for d in (0, 1):
        is_ctx, first, _ = flags[d]

        @pl.when(jnp.logical_and(first, is_ctx))
        def _(d=d):
            for s in scr[d]:
                s[...] = jnp.zeros_like(s)

        @pl.when(jnp.logical_and(first, jnp.logical_not(is_ctx)))
        def _(d=d):
            load_state(sin[d], scr[d])

    run(shared, [(bool(d), din[d], yout[d], scr[d]) for d in (0, 1)])

    for d in (0, 1):
        is_ctx, _, last = flags[d]

        @pl.when(jnp.logical_and(last, is_ctx))
        def _(d=d):
            store_state(scr[d], sout[d])


def _bidir_call(cfg, name, y_width, dir_inputs, shared_inputs, state_inputs, state_shapes, scratch_shapes,
                run, load_state, store_state):
    tb = cfg.tb
    in_specs, args = [], []
    for d in (0, 1):
        for arr, bshape, ifn in dir_inputs:
            in_specs.append(pl.BlockSpec(bshape, lambda j, d=d, ifn=ifn: ifn(d, cfg.blk(d, j))))
            args.append(arr)
    for arr in shared_inputs:
        in_specs.append(pl.BlockSpec(arr.shape, lambda j, nd=arr.ndim: (0,) * nd))
        args.append(arr)
    for d in (0, 1):
        for arr, shp in zip(state_inputs, state_shapes):
            in_specs.append(pl.BlockSpec((None, None) + shp,
                                         lambda j, d=d, k=len(shp): (cfg.smp_seq(d, j), d) + (0,) * k))
            args.append(arr)
    out_specs, out_shape = [], []
    for d in (0, 1):
        out_specs.append(pl.BlockSpec((tb, y_width), lambda j, d=d: (cfg.blk(d, j), 0)))
        out_shape.append(jax.ShapeDtypeStruct((cfg.t, y_width), BF16))
    for d in (0, 1):
        for shp in state_shapes:
            out_specs.append(pl.BlockSpec((None,) + shp, lambda j, d=d, k=len(shp): (cfg.ctx_seq(d, j),) + (0,) * k))
            out_shape.append(jax.ShapeDtypeStruct((cfg.n_ctx,) + shp, F32))
    kern = functools.partial(_bidir_kernel, cfg=cfg, n_dir=len(dir_inputs), n_shared=len(shared_inputs),
                             n_state=len(state_inputs), n_scratch=len(scratch_shapes), run=run,
                             load_state=load_state, store_state=store_state)
    outs = pl.pallas_call(
        kern,
        grid=(cfg.nb,),
        in_specs=in_specs,
        out_specs=out_specs,
        out_shape=out_shape,
        scratch_shapes=[pltpu.VMEM(s, F32) for s in scratch_shapes] * 2,
        compiler_params=_cparams(("arbitrary",)),
        name=name,
    )(*args)
    ns = len(state_shapes)
    finals = [jnp.stack([outs[2 + k], outs[2 + ns + k]], axis=1) for k in range(ns)]
    return outs[0], outs[1], finals


def _chunk_rows(cfg, rev):
    nch = cfg.tb // cfg.c
    return [(nch - 1 - ci if rev else ci) * cfg.c for ci in range(nch)]


def _scan_masks(c, rev):
    tpos = lax.broadcasted_iota(jnp.int32, (c, c), 0)
    spos = lax.broadcasted_iota(jnp.int32, (c, c), 1)
    return tpos, spos, ((spos >= tpos) if rev else (spos <= tpos))


def _mlstm_run(cfg, shared, sides):
    (tri_ref,) = shared
    c = cfg.c
    unit = (lax.broadcasted_iota(jnp.int32, (c, LANES), 1) == 0).astype(F32)
    nch = cfg.tb // c
    heads = range(M_HEADS)

    pre = {}
    for si, (rev, din, _, _) in enumerate(sides):
        g_ref, gt_ref = din[4], din[5]
        sel = tri_ref[int(rev)]
        for ci, r0 in enumerate(_chunk_rows(cfg, rev)):
            gc = g_ref[r0:r0 + c, :]
            gtc = gt_ref[:, r0:r0 + c]
            pre[si, ci] = dict(r0=r0, gtc=gtc,
                               bcol=_sum01(sel, gc),
                               brow=_sum01_nt(gtc[GC_LF:GC_LF + 8], sel))
    probs = [(si, ci, h) for si in range(len(sides)) for ci in range(nch) for h in heads]
    st = {}
    for si, ci, h in probs:
        q_ref, k_ref = sides[si][1][0], sides[si][1][1]
        r0 = pre[si, ci]["r0"]
        hs = slice(h * M_DH, (h + 1) * M_DH)
        q = q_ref[r0:r0 + c, hs].astype(BF16)
        k = (k_ref[r0:r0 + c, hs] * (M_DH ** -0.5)).astype(BF16)
        st[si, ci, h] = dict(q=q, qk=_dot_nt(q, k))
    for si, ci, h in probs:
        rev = sides[si][0]
        _, _, mask = _scan_masks(c, rev)
        col = int(rev) * M_HEADS + h
        e, p = st[si, ci, h], pre[si, ci]
        e["b_c"] = p["bcol"][:, GC_LF + col:GC_LF + col + 1]
        e["b_r"] = p["brow"][col:col + 1, :]
        e["i_r"] = p["gtc"][GC_IG + col:GC_IG + col + 1, :]
        e["dlog"] = jnp.where(mask, e["b_c"] - e["b_r"] + e["i_r"], NEG_BIG)
        e["rmax"] = jnp.max(e["dlog"], axis=-1, keepdims=True)

    for ci in range(nch):
        step = [(si, h) for si in range(len(sides)) for h in heads]
        for si, h in step:
            e = st[si, ci, h]
            m_sc = sides[si][3][1]
            e["m_prev"] = m_sc[h][:, 0:1]
            inter = e["b_c"] + e["m_prev"]
            e["mt"] = jnp.maximum(inter, e["rmax"])
            e["s"] = (e["qk"] * jnp.exp(e["dlog"] - e["mt"])).astype(BF16)
            e["wi"] = jnp.exp(inter - e["mt"])
        for si, h in step:
            e = st[si, ci, h]
            v_ref, c_sc = sides[si][1][2], sides[si][3][0]
            r0 = pre[si, ci]["r0"]
            hs = slice(h * M_DH, (h + 1) * M_DH)
            e["vaug"] = jnp.concatenate([v_ref[r0:r0 + c, hs], unit], axis=1).astype(BF16)
            e["caug"] = c_sc[h]
            e["numden"] = (jnp.dot(e["s"], e["vaug"], preferred_element_type=F32)
                           + e["wi"] * _dot(e["q"], e["caug"]))
        for si, h in step:
            e = st[si, ci, h]
            rev, din, y_ref, _ = sides[si]
            r0 = pre[si, ci]["r0"]
            hs = slice(h * M_DH, (h + 1) * M_DH)
            end = 0 if rev else c - 1
            den = e["numden"][:, M_DH:M_DH + 1]
            hc = e["numden"][:, :M_DH] / jnp.maximum(jnp.abs(den), jnp.exp(-e["mt"]))
            y_ref[r0:r0 + c, hs] = hc.astype(y_ref.dtype)
            e["m_end"] = e["mt"][end:end + 1, :]
            b_end = e["b_c"][end:end + 1, :]
            wk = jnp.exp(b_end - e["b_r"] + e["i_r"] - e["m_end"]) * (M_DH ** -0.5)
            e["carry"] = jnp.exp(b_end + e["m_prev"] - e["m_end"])
            e["ktw"] = (din[3][hs, r0:r0 + c].astype(F32) * wk).astype(BF16)
        for si, h in step:
            e = st[si, ci, h]
            c_sc, m_sc = sides[si][3]
            c_sc[h] = e["carry"] * e["caug"] + jnp.dot(e["ktw"], e["vaug"], preferred_element_type=F32)
            m_sc[h] = jnp.broadcast_to(e["m_end"], (1, LANES))


def _copy_state(src, dst):
    for s, d in zip(src, dst):
        d[...] = s[...]


def _mlstm(cfg, p, kt, g, gt, tri, c0, m0):
    tb = cfg.tb
    qi = OFF_M // M_W
    dir_inputs = [(p, (tb, M_W), lambda d, b: (b, qi)),
                  (p, (tb, M_W), lambda d, b: (b, qi + 1)),
                  (p, (tb, M_W), lambda d, b: (b, qi + 2)),
                  (kt, (M_W, tb), lambda d, b: (0, b)),
                  (g, (tb, LANES), lambda d, b: (b, 0)),
                  (gt, (LANES, tb), lambda d, b: (0, b))]
    shapes = [(M_HEADS, M_DH, 2 * M_DH), (M_HEADS, 1, LANES)]
    return _bidir_call(cfg, "mlstm", M_W, dir_inputs, [tri], [c0, m0], shapes, shapes,
                       functools.partial(_mlstm_run, cfg), _copy_state, _copy_state)


def _mid_rows(g, lev, rev):
    c, w = g.shape
    half = 1 << lev
    blk = 2 * half
    m = half if rev else half - 1
    if blk >= 8:
        g3 = g.reshape(c // blk, blk, w)
        return jnp.broadcast_to(g3[:, m:m + 1, :], g3.shape).reshape(c, w)
    g3 = g.reshape(c // 8, 8, w)
    sub = lax.broadcasted_iota(jnp.int32, g3.shape, 1)
    out = jnp.broadcast_to(g3[:, m:m + 1, :], g3.shape)
    for b0 in range(blk, 8, blk):
        out = jnp.where(sub >= b0, jnp.broadcast_to(g3[:, b0 + m:b0 + m + 1, :], g3.shape), out)
    return out.reshape(c, w)


def _hgrn_run(cfg, shared, sides):
    (tri_ref,) = shared
    c = cfg.c
    levels = int(np.log2(c))
    ones = jnp.ones((H_DK, LANES), BF16)
    nch = cfg.tb // c
    heads = range(H_HEADS)
    dn = (((1,), (1,)), ((), ()))

    pre, pair = {}, {}
    for si, (rev, din, _, _) in enumerate(sides):
        f_ref, lb_ref = din[3], din[4]
        sel = tri_ref[int(rev)]
        lb = lb_ref[0]
        tpos, spos, _ = _scan_masks(c, rev)
        before = (spos > tpos) if rev else (spos < tpos)
        pair[si] = jnp.where(before, tpos ^ spos, 0)
        for ci, r0 in enumerate(_chunk_rows(cfg, rev)):
            f = lb + (1.0 - lb) * _sigmoid(f_ref[r0:r0 + c, :])
            pre[si, ci] = dict(r0=r0, kk=1.0 - f, g=_sum01(sel, jnp.log(f)))
    probs = [(si, ci, h) for si in range(len(sides)) for ci in range(nch) for h in heads]
    st = {}
    for si, ci, h in probs:
        p = pre[si, ci]
        hs = slice(h * H_DK, (h + 1) * H_DK)
        q = sides[si][1][0][p["r0"]:p["r0"] + c, hs]
        kk = p["kk"][:, hs]
        st[si, ci, h] = dict(q=q, kk=kk, qb=q.astype(BF16), kb=kk.astype(BF16), g=p["g"][:, hs],
                             att=jnp.zeros((c, c), F32))
    for lev in range(levels):
        for si, ci, h in probs:
            e = st[si, ci, h]
            x = jnp.exp(-jnp.abs(e["g"] - _mid_rows(e["g"], lev, sides[si][0]))).astype(BF16)
            a = lax.dot_general(e["qb"] * x, e["kb"] * x, dn, preferred_element_type=F32)
            e["att"] = jnp.where((pair[si] >> lev) == 1, a, e["att"])
    for si, ci, h in probs:
        e = st[si, ci, h]
        e["diag"] = jnp.dot((e["q"] * e["kk"]).astype(BF16), ones, preferred_element_type=F32)

    for ci in range(nch):
        step = [(si, h) for si in range(len(sides)) for h in heads]
        for si, h in step:
            e = st[si, ci, h]
            rev, din, y_ref, (s_sc,) = sides[si]
            r0 = pre[si, ci]["r0"]
            hs = slice(h * H_DK, (h + 1) * H_DK)
            v = din[1][r0:r0 + c, hs]
            e["st"] = s_sc[h]
            o = _dot(e["att"], v) + e["diag"] * v + _dot_nt(e["q"] * jnp.exp(e["g"]), e["st"])
            y_ref[r0:r0 + c, hs] = o.astype(y_ref.dtype)
        for si, h in step:
            e = st[si, ci, h]
            rev, din, _, (s_sc,) = sides[si]
            r0 = pre[si, ci]["r0"]
            hs = slice(h * H_DK, (h + 1) * H_DK)
            end = 0 if rev else c - 1
            g_end = e["g"][end:end + 1, :]
            vt = din[2][hs, r0:r0 + c]
            s_sc[h] = jnp.exp(g_end) * e["st"] + _dot(vt, e["kk"] * jnp.exp(g_end - e["g"]))


def _hgrn_load(sin, scr):
    for h in range(H_HEADS):
        scr[0][h] = sin[0][h].T


def _hgrn_store(scr, sout):
    for h in range(H_HEADS):
        sout[0][h] = scr[0][h].T


def _hgrn(cfg, p, vt, lb, tri, s0):
    tb = cfg.tb
    qi = OFF_H // H_W
    dir_inputs = [(p, (tb, H_W), lambda d, b: (b, qi)),
                  (p, (tb, H_W), lambda d, b: (b, qi + 1)),
                  (vt, (H_W, tb), lambda d, b: (0, b)),
                  (p, (tb, H_W), lambda d, b: (b, qi + 3 + d)),
                  (lb, (1, 1, H_W), lambda d, b: (d, 0, 0))]
    return _bidir_call(cfg, "hgrn", H_W, dir_inputs, [tri], [s0], [(H_HEADS, H_DK, H_DV)],
                       [(H_HEADS, H_DV, H_DK)], functools.partial(_hgrn_run, cfg), _hgrn_load, _hgrn_store)


def _ssd_run(cfg, shared, sides):
    tri_ref, ex_ref, sd_ref = shared
    c = cfg.c
    n_rep = S_HEADS // S_GROUPS
    gw = n_rep * S_P
    nch = cfg.tb // c
    lane = lax.broadcasted_iota(jnp.int32, (c, LANES), 1)
    head_of_lane = lax.broadcasted_iota(jnp.int32, (c, gw), 1) // S_P

    pre, mask = {}, {}
    for si, (rev, din, _, _) in enumerate(sides):
        x_ref, cm_ref, bmt_ref, g_ref, gt_ref = din
        dd = int(rev)
        sel, ex = tri_ref[dd], ex_ref[dd]
        mask[si] = _scan_masks(c, rev)[2]
        la0 = GC_DTA + dd * S_HEADS
        is_la = (lane - la0).astype(jnp.uint32) < S_HEADS
        end = 0 if rev else c - 1
        for ci, r0 in enumerate(_chunk_rows(cfg, rev)):
            gc = g_ref[r0:r0 + c, :]
            gtc = gt_ref[:, r0:r0 + c]
            lam = jnp.where(is_la, _sum01(sel, gc), 0.0)
            ela = jnp.exp(lam)
            ea = jnp.dot(ela.astype(BF16), ex, preferred_element_type=F32)
            e8 = 0 if rev else c - 8
            ea_end = _expand2(ela[e8:e8 + 8, :], ex)[end - e8:end - e8 + 1, :]
            wk = jnp.exp(lam[end:end + 1, :] - lam) * pltpu.roll(gc, GC_DTA - GC_DT, 1)
            ewk = jnp.dot(wk.astype(BF16), ex, preferred_element_type=F32)
            xall = x_ref[r0:r0 + c, :]
            pre[si, ci] = dict(r0=r0, la0=la0, lam=lam, ea=ea, xall=xall, ea_end=ea_end,
                               larow=_sum01_nt(gtc[la0:la0 + S_HEADS], sel),
                               dtrow=gtc[GC_DT + dd * S_HEADS:GC_DT + (dd + 1) * S_HEADS],
                               xw=(xall * ewk).astype(BF16))
    probs = [(si, ci, grp) for si in range(len(sides)) for ci in range(nch) for grp in range(S_GROUPS)]
    st = {}
    for si, ci, grp in probs:
        cm_ref, bmt_ref = sides[si][1][1], sides[si][1][2]
        r0 = pre[si, ci]["r0"]
        cmg = cm_ref[r0:r0 + c, grp * S_N:(grp + 1) * S_N].astype(BF16)
        bmt = bmt_ref[grp * S_N:(grp + 1) * S_N, r0:r0 + c]
        st[si, ci, grp] = dict(cmg=cmg, bmt=bmt, cb=jnp.dot(cmg, bmt, preferred_element_type=F32))
    for si, ci, grp in probs:
        e, p = st[si, ci, grp], pre[si, ci]
        xg = p["xall"][:, grp * gw:(grp + 1) * gw]
        lhs, rhs = [], []
        for r in range(n_rep):
            h = grp * n_rep + r
            la_c = p["lam"][:, p["la0"] + h:p["la0"] + h + 1]
            seg = jnp.where(mask[si], jnp.exp(jnp.where(mask[si], la_c - p["larow"][h:h + 1, :], 0.0))
                            * p["dtrow"][h:h + 1, :], 0.0)
            lhs.append((seg * e["cb"]).astype(BF16))
            rhs.append(jnp.where(head_of_lane == r, xg, 0.0).astype(BF16))
        y = jnp.dot(jnp.concatenate(lhs, axis=1), jnp.concatenate(rhs, axis=0), preferred_element_type=F32)
        if not sides[si][0]:
            y = y + sd_ref[:, grp * gw:(grp + 1) * gw] * xg
        e["y"] = y

    for ci in range(nch):
        step = [(si, grp) for si in range(len(sides)) for grp in range(S_GROUPS)]
        for si, grp in step:
            e, p = st[si, ci, grp], pre[si, ci]
            _, _, y_ref, (h_sc,) = sides[si]
            gs = slice(grp * gw, (grp + 1) * gw)
            e["ht"] = h_sc[grp]
            y = e["y"] + p["ea"][:, gs] * jnp.dot(e["cmg"], e["ht"].astype(BF16), preferred_element_type=F32)
            y_ref[p["r0"]:p["r0"] + c, gs] = y.astype(y_ref.dtype)
        for si, grp in step:
            e, p = st[si, ci, grp], pre[si, ci]
            (h_sc,) = sides[si][3]
            gs = slice(grp * gw, (grp + 1) * gw)
            h_sc[grp] = p["ea_end"][:, gs] * e["ht"] + jnp.dot(e["bmt"], p["xw"][:, gs],
                                                               preferred_element_type=F32)


def _ssd_load(sin, scr):
    n_rep = S_HEADS // S_GROUPS
    for grp in range(S_GROUPS):
        scr[0][grp] = sin[0][grp * n_rep:(grp + 1) * n_rep].reshape(n_rep * S_P, S_N).T


def _ssd_store(scr, sout):
    n_rep = S_HEADS // S_GROUPS
    for grp in range(S_GROUPS):
        sout[0][grp * n_rep:(grp + 1) * n_rep] = scr[0][grp].T.reshape(n_rep, S_P, S_N)


def _ssd(cfg, p, bmt, g, gt, tri, ex, sd, h0):
    tb = cfg.tb
    gn = S_GROUPS * S_N
    dir_inputs = [(p, (tb, S_W), lambda d, b: (b, OFF_XBC // S_W)),
                  (p, (tb, gn), lambda d, b: (b, (OFF_XBC + S_W) // gn + 1)),
                  (bmt, (gn, tb), lambda d, b: (0, b)),
                  (g, (tb, LANES), lambda d, b: (b, 0)),
                  (gt, (LANES, tb), lambda d, b: (0, b))]
    return _bidir_call(cfg, "ssd", S_W, dir_inputs, [tri, ex, sd], [h0], [(S_HEADS, S_P, S_N)],
                       [(S_GROUPS, S_N, (S_HEADS // S_GROUPS) * S_P)], functools.partial(_ssd_run, cfg),
                       _ssd_load, _ssd_store)


def _head_norm(y, heads, width):
    outs = []
    for h in range(heads):
        s = y[:, h * width:(h + 1) * width]
        outs.append(s * lax.rsqrt(jnp.mean(s * s, axis=-1, keepdims=True) + EPS))
    return jnp.concatenate(outs, axis=1)


def _merge_kernel(x_ref, gate_ref, mo_ref, hg_ref, sz_ref, bg_ref, ymf_ref, ymb_ref, yhf_ref, yhb_ref,
                  ysf_ref, ysb_ref, mn_ref, hn_ref, sn_ref, wbm_ref, wbh_ref, wbs_ref, wo_ref, o_ref,
                  *, d_model):
    add = lambda a, b: a[...].astype(F32) + b[...].astype(F32)
    y_m = _head_norm(add(ymf_ref, ymb_ref), M_HEADS, M_DH) * mn_ref[...] * _sigmoid(mo_ref[...])
    y_h = _head_norm(add(yhf_ref, yhb_ref), H_HEADS, H_DV) * hn_ref[...] * _silu(hg_ref[...])
    ys = add(ysf_ref, ysb_ref) * _silu(sz_ref[...])
    y_s = ys * lax.rsqrt(jnp.mean(ys * ys, axis=-1, keepdims=True) + EPS) * sn_ref[...]
    bg = bg_ref[...]
    dm = d_model
    merged = (_sigmoid(bg[:, :dm]) * jnp.dot(y_m.astype(BF16), wbm_ref[...], preferred_element_type=F32)
              + _sigmoid(bg[:, dm:2 * dm]) * jnp.dot(y_h.astype(BF16), wbh_ref[...], preferred_element_type=F32)
              + _sigmoid(bg[:, 2 * dm:]) * jnp.dot(y_s.astype(BF16), wbs_ref[...], preferred_element_type=F32))
    out = jnp.dot(merged.astype(BF16), wo_ref[...], preferred_element_type=F32)
    o_ref[...] = x_ref[...] + gate_ref[0] * out


def _merge(cfg, x, mod, p, ym, yh, ys, mn, hn, sn, wbm, wbh, wbs, wo):
    tm = cfg.tm_merge
    row = cfg.mod_row(tm)
    rows_pad = mod.shape[0] // 6
    d = cfg.d
    const = lambda shape: pl.BlockSpec(shape, lambda m: (0,) * len(shape))
    tok = lambda w: pl.BlockSpec((tm, w), lambda m: (m, 0))
    return pl.pallas_call(
        functools.partial(_merge_kernel, d_model=d),
        grid=(cfg.t // tm,),
        in_specs=[tok(d),
                  pl.BlockSpec((1, 1, d), lambda m: (2 * rows_pad + row(m), 0, 0)),
                  pl.BlockSpec((tm, M_W), lambda m: (m, (OFF_M + 3 * M_W) // M_W)),
                  pl.BlockSpec((tm, H_W), lambda m: (m, (OFF_H + 2 * H_W) // H_W)),
                  pl.BlockSpec((tm, S_W), lambda m: (m, OFF_SZ // S_W)),
                  pl.BlockSpec((tm, 3 * d), lambda m: (m, OFF_BG // (3 * d))),
                  tok(M_W), tok(M_W), tok(H_W), tok(H_W), tok(S_W), tok(S_W),
                  const((1, M_W)), const((1, H_W)), const((1, S_W)),
                  const((M_W, d)), const((H_W, d)), const((S_W, d)), const((d, d))],
        out_specs=tok(d),
        out_shape=jax.ShapeDtypeStruct((cfg.t, d), F32),
        compiler_params=_cparams(("arbitrary",)),
        name="merge",
    )(x, mod, p, p, p, p, ym[0], ym[1], yh[0], yh[1], ys[0], ys[1], mn, hn, sn, wbm, wbh, wbs, wo)


def _ffn_up_kernel(x_ref, sc_ref, sh_ref, nw_ref, wa_ref, wb_ref, o_ref, *, d_ff, tc):
    x = x_ref[...]
    y = x * lax.rsqrt(jnp.mean(x * x, axis=-1, keepdims=True) + EPS) * nw_ref[...]
    h = (y * (1.0 + sc_ref[0]) + sh_ref[0]).astype(BF16)
    for j in range(d_ff // tc):
        sl = slice(j * tc, (j + 1) * tc)
        a = jnp.dot(h, wa_ref[:, sl], preferred_element_type=F32)
        b = jnp.dot(h, wb_ref[:, sl], preferred_element_type=F32)
        o_ref[:, sl] = (_silu(a) * b).astype(BF16)


def _ffn_up(cfg, x, mod, nw, w_gu):
    tm = cfg.tm_ffn
    row = cfg.mod_row(tm)
    rows_pad = mod.shape[0] // 6
    d, d_ff = cfg.d, cfg.d_ff
    return pl.pallas_call(
        functools.partial(_ffn_up_kernel, d_ff=d_ff, tc=MXU_N),
        grid=(cfg.t // tm,),
        in_specs=[pl.BlockSpec((tm, d), lambda m: (m, 0)),
                  pl.BlockSpec((1, 1, d), lambda m: (4 * rows_pad + row(m), 0, 0)),
                  pl.BlockSpec((1, 1, d), lambda m: (3 * rows_pad + row(m), 0, 0)),
                  pl.BlockSpec((1, d), lambda m: (0, 0)),
                  pl.BlockSpec((d, d_ff), lambda m: (0, 0)),
                  pl.BlockSpec((d, d_ff), lambda m: (0, 1))],
        out_specs=pl.BlockSpec((tm, d_ff), lambda m: (m, 0)),
        out_shape=jax.ShapeDtypeStruct((cfg.t, d_ff), BF16),
        compiler_params=_cparams(("arbitrary",)),
        name="ffn_up",
    )(x, mod, mod, nw, w_gu, w_gu)


def _ffn_down_kernel(x_ref, gate_ref, a_ref, w_ref, o_ref):
    o_ref[...] = x_ref[...] + gate_ref[0] * jnp.dot(a_ref[...], w_ref[...], preferred_element_type=F32)


def _ffn_down_final_kernel(x_ref, gate_ref, a_ref, w_ref, nf_ref, op_ref, os_ref, *, nct):
    m = pl.program_id(0)
    x = x_ref[...] + gate_ref[0] * jnp.dot(a_ref[...], w_ref[...], preferred_element_type=F32)
    y = x * lax.rsqrt(jnp.mean(x * x, axis=-1, keepdims=True) + EPS) * nf_ref[...]

    @pl.when(m < nct)
    def _():
        op_ref[...] = y

    @pl.when(m >= nct)
    def _():
        os_ref[...] = y


def _ffn_down(cfg, x, mod, act, w_down, norm_f=None):
    tm = cfg.tm_ffn
    row = cfg.mod_row(tm)
    rows_pad = mod.shape[0] // 6
    d, d_ff = cfg.d, cfg.d_ff
    in_specs = [pl.BlockSpec((tm, d), lambda m: (m, 0)),
                pl.BlockSpec((1, 1, d), lambda m: (5 * rows_pad + row(m), 0, 0)),
                pl.BlockSpec((tm, d_ff), lambda m: (m, 0)),
                pl.BlockSpec((d_ff, d), lambda m: (0, 0))]
    if norm_f is None:
        return pl.pallas_call(
            _ffn_down_kernel,
            grid=(cfg.t // tm,),
            in_specs=in_specs,
            out_specs=pl.BlockSpec((tm, d), lambda m: (m, 0)),
            out_shape=jax.ShapeDtypeStruct((cfg.t, d), F32),
            compiler_params=_cparams(("arbitrary",)),
            name="ffn_down",
        )(x, mod, act, w_down)
    nct = cfg.t_ctx // tm
    return pl.pallas_call(
        functools.partial(_ffn_down_final_kernel, nct=nct),
        grid=(cfg.t // tm,),
        in_specs=in_specs + [pl.BlockSpec((1, d), lambda m: (0, 0))],
        out_specs=[pl.BlockSpec((tm, d), lambda m: (jnp.minimum(m, nct - 1), 0)),
                   pl.BlockSpec((tm, d), lambda m: (jnp.maximum(m - nct, 0), 0))],
        out_shape=[jax.ShapeDtypeStruct((cfg.t_ctx, d), F32),
                   jax.ShapeDtypeStruct((cfg.t_smp, d), F32)],
        compiler_params=_cparams(("arbitrary",)),
        name="ffn_down_final",
    )(x, mod, act, w_down, norm_f)


def _grid_pos_embed(rows, d_model):
    quarter = d_model // 4
    freq = POS_BASE ** (-jnp.arange(quarter, dtype=F32) / quarter)
    r = jnp.arange(rows, dtype=F32)[:, None] * freq
    cl = jnp.arange(GRID_W, dtype=F32)[:, None] * freq
    row_e = jnp.concatenate([jnp.sin(r), jnp.cos(r)], axis=-1)
    col_e = jnp.concatenate([jnp.sin(cl), jnp.cos(cl)], axis=-1)
    emb = jnp.concatenate([jnp.broadcast_to(row_e[:, None], (rows, GRID_W, d_model // 2)),
                           jnp.broadcast_to(col_e[None], (rows, GRID_W, d_model // 2))], axis=-1)
    return emb.reshape(rows * GRID_W, d_model)


def kernel(x_prompt, x_sample, state_mlstm_c, state_mlstm_n, state_mlstm_m, state_hgrn, state_ssm, c, c_ctx, w_ada, b_ada, norm1, norm2, w_in, m_bi, m_bf, m_norm, h_lb, h_norm, s_conv_w, s_conv_b, s_dt_bias, s_a_log, s_d, s_norm, w_bm, w_bh, w_bs, w_out, w_gu, w_down, norm_f):
    n_ctx, l_ctx, d = x_prompt.shape
    n_smp, l_smp, _ = x_sample.shape
    depth = w_in.shape[0]
    d_ff = w_down.shape[1]
    cfg = _Cfg(n_ctx, l_ctx, n_smp, l_smp, d, d_ff)

    tri = _tri_consts(cfg.c)
    ex = _expand_consts()

    rows_pad = -(-(n_smp + 1) // 8) * 8
    cvec = jnp.zeros((rows_pad, d), F32).at[:n_smp].set(c).at[n_smp].set(c_ctx)
    mod_all = _ada(cvec, w_ada, b_ada)
    mod_all = mod_all.reshape(depth, rows_pad, 6, d).transpose(0, 2, 1, 3).reshape(depth, 6 * rows_pad, 1, d)

    p_lb = jax.nn.softmax(h_lb.astype(F32), axis=1)
    lower = jnp.cumsum(p_lb, axis=1) - p_lb[:, :1]

    pos = _grid_pos_embed(l_smp // GRID_W, d)
    x = _embed(cfg, x_prompt.reshape(cfg.t_ctx, d), x_sample.reshape(cfg.t_smp, d), pos)

    a_neg = -jnp.exp(s_a_log.astype(F32))
    new_c, new_n, new_m, new_h, new_s = [], [], [], [], []
    y_p = y_s = None
    for i in range(depth):
        w = w_in[i]
        o_mi, o_hq, o_sz, o_xbc, o_dt, o_bg = 2048, 2064, 4624, 5648, 7696, 7728
        w_main = jnp.concatenate([w[:, o_xbc:o_dt], w[:, o_sz:o_xbc], w[:, o_bg:], w[:, :o_mi],
                                  w[:, o_hq:o_sz]], axis=1).astype(BF16)
        w_small = jnp.concatenate([w[:, o_mi:o_hq], w[:, o_dt:o_bg], w[:, o_dt:o_bg],
                                   jnp.zeros((d, LANES - GC_DTA - 2 * S_HEADS), F32)], axis=1).astype(BF16)
        dtb = s_dt_bias[i].reshape(-1)
        gbias = jnp.concatenate([m_bi[i].reshape(-1), m_bf[i].reshape(-1), dtb, dtb,
                                 jnp.zeros((LANES - GC_DTA - 2 * S_HEADS,), F32)]).reshape(1, LANES)
        gmul = jnp.concatenate([jnp.ones((GC_DTA,), F32), a_neg[i].reshape(-1),
                                jnp.zeros((LANES - GC_DTA - 2 * S_HEADS,), F32)]).reshape(1, LANES)
        ncw = -(-CONV_CH // PROJ_TN) * PROJ_TN
        cw = jnp.zeros((3, ncw), F32).at[:, :CONV_CH].set(s_conv_w[i]).reshape(3, ncw // PROJ_TN, PROJ_TN)
        cw = cw.transpose(1, 0, 2)
        cb = jnp.zeros((ncw,), F32).at[:CONV_CH].set(s_conv_b[i]).reshape(ncw // PROJ_TN, 1, PROJ_TN)
        mod = mod_all[i]

        p, g, gt, bmt, kt, vt = _inproj(cfg, x, mod, norm1[i].reshape(1, d), w_main, w_small, gbias, gmul,
                                        cw, cb)

        c0 = jnp.concatenate([state_mlstm_c[:, i], state_mlstm_n[:, i][..., None],
                              jnp.zeros(state_mlstm_c[:, i].shape[:-1] + (M_DH - 1,), F32)], axis=-1)
        m0 = jnp.broadcast_to(state_mlstm_m[:, i][..., None, None], (n_smp, 2, M_HEADS, 1, LANES))
        ymf, ymb, (c_fin, m_fin) = _mlstm(cfg, p, kt, g, gt, tri, c0, m0)
        yhf, yhb, (h_fin,) = _hgrn(cfg, p, vt, lower[:, i].reshape(2, 1, H_W), tri, state_hgrn[:, i])
        sd = jnp.repeat(s_d[i].astype(F32), S_P).reshape(1, S_W)
        ysf, ysb, (s_fin,) = _ssd(cfg, p, bmt, g, gt, tri, ex, sd, state_ssm[:, i])

        x = _merge(cfg, x, mod, p, (ymf, ymb), (yhf, yhb), (ysf, ysb), m_norm[i].reshape(1, M_W),
                   h_norm[i].reshape(1, H_W), s_norm[i].reshape(1, S_W), w_bm[i].astype(BF16),
                   w_bh[i].astype(BF16), w_bs[i].astype(BF16), w_out[i].astype(BF16))
        act = _ffn_up(cfg, x, mod, norm2[i].reshape(1, d), w_gu[i].astype(BF16))
        if i + 1 < depth:
            x = _ffn_down(cfg, x, mod, act, w_down[i].astype(BF16))
        else:
            y_p, y_s = _ffn_down(cfg, x, mod, act, w_down[i].astype(BF16), norm_f.reshape(1, d))

        new_c.append(c_fin[..., :M_DH])
        new_n.append(c_fin[..., M_DH])
        new_m.append(m_fin[..., 0, 0])
        new_h.append(h_fin)
        new_s.append(s_fin)

    return (y_p.reshape(n_ctx, l_ctx, d), y_s.reshape(n_smp, l_smp, d),
            jnp.stack(new_c, axis=1), jnp.stack(new_n, axis=1), jnp.stack(new_m, axis=1),
            jnp.stack(new_h, axis=1), jnp.stack(new_s, axis=1))
```

```python
import functools

import numpy as np
import jax
import jax.numpy as jnp
from jax import lax
from jax.experimental import pallas as pl
from jax.experimental.pallas import tpu as pltpu

F32 = jnp.float32
BF16 = jnp.bfloat16

GRID_W = 64
EPS = 1e-6
POS_BASE = 10000.0
NEG_BIG = -1e30
M_HEADS, M_DH = 4, 128
H_HEADS, H_DK, H_DV = 4, 128, 128
S_HEADS, S_P, S_GROUPS, S_N = 16, 64, 4, 128
M_W = M_HEADS * M_DH
H_W = H_HEADS * H_DV
S_W = S_HEADS * S_P
CONV_CH = S_W + 2 * S_GROUPS * S_N

LANES = 128
MXU_N = 256
VMEM_LIMIT = 56 * 1024 * 1024
SCAN_CHUNK = 128
SCAN_BLOCK = 256
PROJ_TN = 1536

OFF_XBC, OFF_SZ, OFF_BG, OFF_M, OFF_H = 0, 2048, 3072, 6144, 8192
N_MAIN = 10752
GC_IG, GC_LF, GC_DT, GC_DTA = 0, 8, 16, 48


def _cparams(sem):
    return pltpu.CompilerParams(dimension_semantics=sem, vmem_limit_bytes=VMEM_LIMIT)


def _sigmoid(x):
    return 1.0 / (1.0 + jnp.exp(-x))


def _silu(x):
    return x * _sigmoid(x)


def _softplus(x):
    return jnp.maximum(x, 0.0) + jnp.log(1.0 + jnp.exp(-jnp.abs(x)))


def _dot(a, b):
    return jnp.dot(a.astype(BF16), b.astype(BF16), preferred_element_type=F32)


def _dot_nt(a, b):
    return lax.dot_general(a.astype(BF16), b.astype(BF16), (((1,), (1,)), ((), ())),
                           preferred_element_type=F32)


def _split3(x):
    x1 = x.astype(BF16)
    r1 = x - x1.astype(F32)
    x2 = r1.astype(BF16)
    x3 = (r1 - x2.astype(F32)).astype(BF16)
    return x1, x2, x3


def _sum01(sel, x):
    x1, x2, x3 = _split3(x)
    d = functools.partial(jnp.dot, preferred_element_type=F32)
    return d(sel, x1) + d(sel, x2) + d(sel, x3)


def _sum01_nt(x, sel):
    x1, x2, x3 = _split3(x)
    d = lambda a: lax.dot_general(a, sel, (((1,), (1,)), ((), ())), preferred_element_type=F32)
    return d(x1) + d(x2) + d(x3)


def _expand2(x, e):
    x1 = x.astype(BF16)
    x2 = (x - x1.astype(F32)).astype(BF16)
    return jnp.dot(x1, e, preferred_element_type=F32) + jnp.dot(x2, e, preferred_element_type=F32)


class _Cfg:
    def __init__(self, n_ctx, l_ctx, n_smp, l_smp, d_model, d_ff):
        self.n_ctx, self.l_ctx, self.n_smp, self.l_smp = n_ctx, l_ctx, n_smp, l_smp
        self.d, self.d_ff = d_model, d_ff
        self.t_ctx = n_ctx * l_ctx
        self.t_smp = n_smp * l_smp
        self.t = self.t_ctx + self.t_smp
        self.tb = SCAN_BLOCK
        self.c = SCAN_CHUNK
        assert l_ctx % self.tb == 0 and l_smp % self.tb == 0 and self.tb % self.c == 0
        assert l_ctx & (l_ctx - 1) == 0 and l_smp % GRID_W == 0
        self.nbc = l_ctx // self.tb
        self.nbs = l_smp // self.tb
        self.gc0 = self.t_ctx // self.tb
        self.nb = self.t // self.tb
        self.tm = 1024 if (self.t_ctx % 1024 == 0 and self.t_smp % 1024 == 0 and l_smp % 1024 == 0) else 256
        assert self.t_ctx % self.tm == 0 and l_smp % self.tm == 0 and self.tm % l_ctx == 0
        self.tm_merge = 256
        self.tm_ffn = 512 if self.tm == 1024 else 256

    def mod_row(self, tm):
        nct = self.t_ctx // tm
        per = self.l_smp // tm

        def f(m):
            return jnp.where(m < nct, self.n_smp, jnp.maximum(m - nct, 0) // per)
        return f

    def blk(self, d, j):
        return j + d * (self.nb - 1 - 2 * j)

    def scan_flags(self, d, j):
        g = self.blk(d, j)
        is_ctx = g < self.gc0
        gs = jnp.maximum(g - self.gc0, 0)
        pos = jnp.where(is_ctx, lax.rem(g, self.nbc), lax.rem(gs, self.nbs))
        nblk = jnp.where(is_ctx, self.nbc, self.nbs)
        spos = pos + d * (nblk - 1 - 2 * pos)
        return is_ctx, spos == 0, spos == nblk - 1

    def smp_seq(self, d, j):
        return jnp.clip((self.blk(d, j) - self.gc0) // self.nbs, 0, self.n_smp - 1)

    def ctx_seq(self, d, j):
        return jnp.minimum(self.blk(d, j) // self.nbc, self.n_ctx - 1)


def _tri_consts(c):
    t = np.arange(c)
    fwd = (t[None, :] <= t[:, None]).astype(np.float32)
    return jnp.asarray(np.stack([fwd, fwd.T]), dtype=BF16)


def _expand_consts():
    e = np.zeros((2, LANES, S_W), np.float32)
    for dd in range(2):
        for h in range(S_HEADS):
            e[dd, GC_DTA + dd * S_HEADS + h, h * S_P:(h + 1) * S_P] = 1.0
    return jnp.asarray(e, dtype=BF16)


def _embed_kernel(xp_ref, xs_ref, pos_ref, o_ref, *, nct):
    m = pl.program_id(0)

    @pl.when(m < nct)
    def _():
        o_ref[...] = xp_ref[...]

    @pl.when(m >= nct)
    def _():
        o_ref[...] = xs_ref[...] + pos_ref[...]


def _embed(cfg, xp2, xs2, pos):
    tm = cfg.tm
    nct = cfg.t_ctx // tm
    per = cfg.l_smp // tm
    return pl.pallas_call(
        functools.partial(_embed_kernel, nct=nct),
        grid=(cfg.t // tm,),
        in_specs=[pl.BlockSpec((tm, cfg.d), lambda m: (jnp.minimum(m, nct - 1), 0)),
                  pl.BlockSpec((tm, cfg.d), lambda m: (jnp.maximum(m - nct, 0), 0)),
                  pl.BlockSpec((tm, cfg.d), lambda m: (lax.rem(jnp.maximum(m - nct, 0), per), 0))],
        out_specs=pl.BlockSpec((tm, cfg.d), lambda m: (m, 0)),
        out_shape=jax.ShapeDtypeStruct((cfg.t, cfg.d), F32),
        compiler_params=_cparams(("arbitrary",)),
        name="embed",
    )(xp2, xs2, pos)


def _ada_kernel(c_ref, w_ref, b_ref, o_ref):
    o_ref[0] = _dot(_silu(c_ref[...]), w_ref[0]) + b_ref[0]


def _ada(cvec, w_ada, b_ada):
    depth, d, n = w_ada.shape
    rows = cvec.shape[0]
    tn = 1536
    return pl.pallas_call(
        _ada_kernel,
        grid=(depth, n // tn),
        in_specs=[pl.BlockSpec((rows, d), lambda l, j: (0, 0)),
                  pl.BlockSpec((1, d, tn), lambda l, j: (l, 0, j)),
                  pl.BlockSpec((1, 1, tn), lambda l, j: (l, 0, j))],
        out_specs=pl.BlockSpec((1, rows, tn), lambda l, j: (l, 0, j)),
        out_shape=jax.ShapeDtypeStruct((depth, rows, n), F32),
        compiler_params=_cparams(("arbitrary", "arbitrary")),
        name="ada",
    )(cvec, w_ada, b_ada.reshape(depth, 1, n))


def _inproj_kernel(x_ref, sc_ref, sh_ref, nw_ref, w_ref, ws_ref, gb_ref, ga_ref, cw_ref, cb_ref,
                   p_ref, g_ref, gt_ref, bmt_ref, kt_ref, vt_ref, xn_ref, *, tm, tn, nct, seg_ctx):
    m = pl.program_id(0)
    n = pl.program_id(1)
    gn = S_GROUPS * S_N
    nt = N_MAIN // tn

    def prologue():
        x = x_ref[...]
        y = x * lax.rsqrt(jnp.mean(x * x, axis=-1, keepdims=True) + EPS) * nw_ref[...]
        h = (y * (1.0 + sc_ref[0]) + sh_ref[0]).astype(BF16)
        xn_ref[...] = h
        raw = jnp.dot(h, ws_ref[...], preferred_element_type=F32) + gb_ref[...]
        lane = lax.broadcasted_iota(jnp.int32, raw.shape, 1)
        sp = _softplus(raw)
        g = jnp.where(lane < GC_LF, raw,
                      jnp.where(lane < GC_DT, -_softplus(-raw),
                                jnp.where(lane < GC_DTA, sp, sp * ga_ref[...])))
        g_ref[...] = g
        gt_ref[...] = g.T

    def matmul_cols(c0, c1):
        return jnp.dot(xn_ref[...], w_ref[:, c0:c1], preferred_element_type=F32)

    def conv_tile(tile, ncols):
        row = lax.broadcasted_iota(jnp.int32, (tm, LANES), 0)
        seg_mask = jnp.where(m < nct, seg_ctx - 1, GRID_W - 1)
        inseg = row & seg_mask
        has_prev = inseg != 0
        has_next = inseg != seg_mask
        cw = cw_ref[0]
        if ncols < tn:
            p_ref[:, ncols:] = matmul_cols(ncols, tn)
        accs = [matmul_cols(c0, c0 + MXU_N) for c0 in range(0, ncols, MXU_N)]
        for c0 in range(0, ncols, MXU_N):
            acc = accs[c0 // MXU_N]
            for cb in range(MXU_N // LANES):
                lo = c0 + cb * LANES
                sl = slice(lo, lo + LANES)
                u = acc[:, cb * LANES:(cb + 1) * LANES]
                up = jnp.where(has_prev, pltpu.roll(u, 1, 0), 0.0)
                un = jnp.where(has_next, pltpu.roll(u, tm - 1, 0), 0.0)
                v = _silu(cw[0:1, sl] * up + cw[1:2, sl] * u + cw[2:3, sl] * un + cb_ref[0][:, sl])
                p_ref[:, sl] = v
                col = tile * tn + lo - S_W
                if 0 <= col < gn:
                    bmt_ref[col:col + LANES, :] = v.T.astype(BF16)

    def plain_tile(transposed):
        acc = matmul_cols(0, tn)
        p_ref[...] = acc
        for dst_ref, off in transposed:
            for cb in range(M_W // LANES):
                blk = acc[:, off + cb * LANES:off + (cb + 1) * LANES]
                dst_ref[cb * LANES:(cb + 1) * LANES, :] = blk.T.astype(BF16)

    n_conv = -(-CONV_CH // tn)
    special = {}
    for dst_ref, col in ((kt_ref, OFF_M + M_W), (vt_ref, OFF_H + H_W)):
        tile, off = divmod(col, tn)
        assert off + M_W <= tn and tile >= n_conv
        special.setdefault(tile, []).append((dst_ref, off))

    for tile in range(nt):
        @pl.when(n == tile)
        def _(tile=tile):
            if tile == 0:
                prologue()
            if tile < n_conv:
                conv_tile(tile, min(tn, CONV_CH - tile * tn))
            else:
                plain_tile(special.get(tile, []))


def _inproj(cfg, x, mod, nw, w_main, w_small, gbias, gmul, cw, cb):
    tm, tn = cfg.tm, PROJ_TN
    nct = cfg.t_ctx // tm
    row = cfg.mod_row(tm)
    rows_pad = mod.shape[0] // 6
    nt = N_MAIN // tn
    kern = functools.partial(_inproj_kernel, tm=tm, tn=tn, nct=nct, seg_ctx=cfg.l_ctx)
    tspec = pl.BlockSpec((M_W, tm), lambda m, n: (0, m))
    tshape = jax.ShapeDtypeStruct((M_W, cfg.t), BF16)
    return pl.pallas_call(
        kern,
        grid=(cfg.t // tm, nt),
        in_specs=[pl.BlockSpec((tm, cfg.d), lambda m, n: (m, 0)),
                  pl.BlockSpec((1, 1, cfg.d), lambda m, n: (1 * rows_pad + row(m), 0, 0)),
                  pl.BlockSpec((1, 1, cfg.d), lambda m, n: (0 * rows_pad + row(m), 0, 0)),
                  pl.BlockSpec((1, cfg.d), lambda m, n: (0, 0)),
                  pl.BlockSpec((cfg.d, tn), lambda m, n: (0, n)),
                  pl.BlockSpec((cfg.d, LANES), lambda m, n: (0, 0)),
                  pl.BlockSpec((1, LANES), lambda m, n: (0, 0)),
                  pl.BlockSpec((1, LANES), lambda m, n: (0, 0)),
                  pl.BlockSpec((1, 3, tn), lambda m, n: (jnp.minimum(n, 1), 0, 0)),
                  pl.BlockSpec((1, 1, tn), lambda m, n: (jnp.minimum(n, 1), 0, 0))],
        out_specs=[pl.BlockSpec((tm, tn), lambda m, n: (m, n)),
                   pl.BlockSpec((tm, LANES), lambda m, n: (m, 0)),
                   pl.BlockSpec((LANES, tm), lambda m, n: (0, m)),
                   tspec, tspec, tspec],
        out_shape=[jax.ShapeDtypeStruct((cfg.t, N_MAIN), F32),
                   jax.ShapeDtypeStruct((cfg.t, LANES), F32),
                   jax.ShapeDtypeStruct((LANES, cfg.t), F32),
                   tshape, tshape, tshape],
        scratch_shapes=[pltpu.VMEM((tm, cfg.d), BF16)],
        compiler_params=_cparams(("arbitrary", "arbitrary")),
        name="inproj",
    )(x, mod, mod, nw, w_main, w_small, gbias, gmul, cw, cb)


def _bidir_kernel(*refs, cfg, n_dir, n_shared, n_state, n_scratch, run, load_state, store_state):
    it = iter(refs)
    take = lambda k: [next(it) for _ in range(k)]
    din = [take(n_dir), take(n_dir)]
    shared = take(n_shared)
    sin = [take(n_state), take(n_state)]
    yout = [take(1)[0], take(1)[0]]
    sout = [take(n_state), take(n_state)]
    scr = [take(n_scratch), take(n_scratch)]
    j = pl.program_id(0)
    flags = [cfg.scan_flags(d, j) for d in (0, 1)]

    for d in (0, 1):
        is_ctx, first, _ = flags[d]

        @pl.when(jnp.logical_and(first, is_ctx))
        def _(d=d):
            for s in scr[d]:
                s[...] = jnp.zeros_like(s)

        @pl.when(jnp.logical_and(first, jnp.logical_not(is_ctx)))
        def _(d=d):
            load_state(sin[d], scr[d])

    run(shared, [(bool(d), din[d], yout[d], scr[d]) for d in (0, 1)])

    for d in (0, 1):
        is_ctx, _, last = flags[d]

        @pl.when(jnp.logical_and(last, is_ctx))
        def _(d=d):
            store_state(scr[d], sout[d])


def _bidir_call(cfg, name, y_width, dir_inputs, shared_inputs, state_inputs, state_shapes, scratch_shapes,
                run, load_state, store_state):
    tb = cfg.tb
    in_specs, args = [], []
    for d in (0, 1):
        for arr, bshape, ifn in dir_inputs:
            in_specs.append(pl.BlockSpec(bshape, lambda j, d=d, ifn=ifn: ifn(d, cfg.blk(d, j))))
            args.append(arr)
    for arr in shared_inputs:
        in_specs.append(pl.BlockSpec(arr.shape, lambda j, nd=arr.ndim: (0,) * nd))
        args.append(arr)
    for d in (0, 1):
        for arr, shp in zip(state_inputs, state_shapes):
            in_specs.append(pl.BlockSpec((None, None) + shp,
                                         lambda j, d=d, k=len(shp): (cfg.smp_seq(d, j), d) + (0,) * k))
            args.append(arr)
    out_specs, out_shape = [], []
    for d in (0, 1):
        out_specs.append(pl.BlockSpec((tb, y_width), lambda j, d=d: (cfg.blk(d, j), 0)))
        out_shape.append(jax.ShapeDtypeStruct((cfg.t, y_width), BF16))
    for d in (0, 1):
        for shp in state_shapes:
            out_specs.append(pl.BlockSpec((None,) + shp, lambda j, d=d, k=len(shp): (cfg.ctx_seq(d, j),) + (0,) * k))
            out_shape.append(jax.ShapeDtypeStruct((cfg.n_ctx,) + shp, F32))
    kern = functools.partial(_bidir_kernel, cfg=cfg, n_dir=len(dir_inputs), n_shared=len(shared_inputs),
                             n_state=len(state_inputs), n_scratch=len(scratch_shapes), run=run,
                             load_state=load_state, store_state=store_state)
    outs = pl.pallas_call(
        kern,
        grid=(cfg.nb,),
        in_specs=in_specs,
        out_specs=out_specs,
        out_shape=out_shape,
        scratch_shapes=[pltpu.VMEM(s, F32) for s in scratch_shapes] * 2,
        compiler_params=_cparams(("arbitrary",)),
        name=name,
    )(*args)
    ns = len(state_shapes)
    finals = [(outs[2 + k], outs[2 + ns + k]) for k in range(ns)]
    return outs[0], outs[1], finals


def _chunk_rows(cfg, rev):
    nch = cfg.tb // cfg.c
    return [(nch - 1 - ci if rev else ci) * cfg.c for ci in range(nch)]


def _scan_masks(c, rev):
    tpos = lax.broadcasted_iota(jnp.int32, (c, c), 0)
    spos = lax.broadcasted_iota(jnp.int32, (c, c), 1)
    return tpos, spos, ((spos >= tpos) if rev else (spos <= tpos))


def _mlstm_run(cfg, shared, sides):
    (tri_ref,) = shared
    c = cfg.c
    unit = (lax.broadcasted_iota(jnp.int32, (c, LANES), 1) == 0).astype(F32)
    nch = cfg.tb // c
    heads = range(M_HEADS)

    row = lax.broadcasted_iota(jnp.int32, (c, LANES), 0)
    pre = {}
    for si, (rev, din, _, _) in enumerate(sides):
        g_ref, gt_ref = din[4], din[5]
        sel = tri_ref[int(rev)]
        for ci, r0 in enumerate(_chunk_rows(cfg, rev)):
            gc = g_ref[r0:r0 + c, :]
            gtc = gt_ref[:, r0:r0 + c]
            bcol = _sum01(sel, gc)
            brow = _sum01_nt(gtc[GC_LF:GC_LF + 8], sel)
            ba = pltpu.roll(bcol, LANES - (GC_LF - GC_IG), 1)
            cm = gc - ba
            k = 1
            while k < c:
                if rev:
                    cm = jnp.maximum(cm, jnp.where(row < c - k, pltpu.roll(cm, c - k, 0), NEG_BIG))
                else:
                    cm = jnp.maximum(cm, jnp.where(row >= k, pltpu.roll(cm, k, 0), NEG_BIG))
                k *= 2
            pre[si, ci] = dict(r0=r0, ba=ba, cm=cm,
                               urow=gtc[GC_IG:GC_IG + 8] - brow)
    probs = [(si, ci, h) for si in range(len(sides)) for ci in range(nch) for h in heads]
    st = {}
    for si, ci, h in probs:
        q_ref, k_ref = sides[si][1][0], sides[si][1][1]
        r0 = pre[si, ci]["r0"]
        hs = slice(h * M_DH, (h + 1) * M_DH)
        q = q_ref[r0:r0 + c, hs]
        k = (k_ref[r0:r0 + c, hs] * (M_DH ** -0.5)).astype(BF16)
        st[si, ci, h] = dict(q=q, qk=_dot_nt(q, k))
    masks = [_scan_masks(c, rev)[2] for rev, _, _, _ in sides]
    lane1 = lax.broadcasted_iota(jnp.int32, (1, LANES), 1)

    for ci in range(nch):
        step = [(si, h) for si in range(len(sides)) for h in heads]
        col_q = {}
        for si, (rev, _, _, (_, m_sc)) in enumerate(sides):
            p = pre[si, ci]
            end = 0 if rev else c - 1
            mrow = jnp.zeros((1, LANES), F32)
            for h in heads:
                mrow = jnp.where(lane1 == int(rev) * M_HEADS + h, m_sc[h], mrow)
            z = jnp.maximum(p["cm"], mrow)
            mt = p["ba"] + z
            m_end = mt[end:end + 1, :]
            b_end = p["ba"][end:end + 1, :]
            col_q[si] = dict(z=z, wi=jnp.exp(mrow - z), emt=jnp.exp(-mt), m_end=m_end,
                             bm=b_end - m_end, carry=jnp.exp(b_end + mrow - m_end))
        for si, h in step:
            e, cq = st[si, ci, h], col_q[si]
            col = int(sides[si][0]) * M_HEADS + h
            e["u_r"] = pre[si, ci]["urow"][col:col + 1, :]
            s = e["qk"] * jnp.exp(jnp.where(masks[si], e["u_r"] - cq["z"][:, col:col + 1], NEG_BIG))
            qi = e["q"] * cq["wi"][:, col:col + 1]
            e["lhs"] = jnp.concatenate([s, qi], axis=1).astype(BF16)
        for si, h in step:
            e = st[si, ci, h]
            v_ref, c_sc = sides[si][1][2], sides[si][3][0]
            r0 = pre[si, ci]["r0"]
            hs = slice(h * M_DH, (h + 1) * M_DH)
            e["vaug"] = jnp.concatenate([v_ref[r0:r0 + c, hs], unit], axis=1).astype(BF16)
            e["caug"] = c_sc[h]
            rhs = jnp.concatenate([e["vaug"], e["caug"].astype(BF16)], axis=0)
            e["numden"] = jnp.dot(e["lhs"], rhs, preferred_element_type=F32)
        for si, h in step:
            e = st[si, ci, h]
            rev, din, y_ref, _ = sides[si]
            r0 = pre[si, ci]["r0"]
            hs = slice(h * M_DH, (h + 1) * M_DH)
            cq = col_q[si]
            col = int(rev) * M_HEADS + h
            den = e["numden"][:, M_DH:M_DH + 1]
            hc = e["numden"][:, :M_DH] / jnp.maximum(jnp.abs(den), cq["emt"][:, col:col + 1])
            y_ref[r0:r0 + c, hs] = hc.astype(y_ref.dtype)
            wk = jnp.exp(e["u_r"] + cq["bm"][:, col:col + 1]) * (M_DH ** -0.5)
            e["ktw"] = (din[3][hs, r0:r0 + c].astype(F32) * wk).astype(BF16)
        for si, h in step:
            e, cq = st[si, ci, h], col_q[si]
            c_sc, m_sc = sides[si][3]
            col = int(sides[si][0]) * M_HEADS + h
            c_sc[h] = (cq["carry"][:, col:col + 1] * e["caug"]
                       + jnp.dot(e["ktw"], e["vaug"], preferred_element_type=F32))
            m_sc[h] = jnp.broadcast_to(cq["m_end"][:, col:col + 1], (1, LANES))


def _copy_state(src, dst):
    for s, d in zip(src, dst):
        d[...] = s[...]


def _mlstm(cfg, p, kt, g, gt, tri, c0, m0):
    tb = cfg.tb
    qi = OFF_M // M_W
    dir_inputs = [(p, (tb, M_W), lambda d, b: (b, qi)),
                  (p, (tb, M_W), lambda d, b: (b, qi + 1)),
                  (p, (tb, M_W), lambda d, b: (b, qi + 2)),
                  (kt, (M_W, tb), lambda d, b: (0, b)),
                  (g, (tb, LANES), lambda d, b: (b, 0)),
                  (gt, (LANES, tb), lambda d, b: (0, b))]
    shapes = [(M_HEADS, M_DH, 2 * M_DH), (M_HEADS, 1, LANES)]
    return _bidir_call(cfg, "mlstm", M_W, dir_inputs, [tri], [c0, m0], shapes, shapes,
                       functools.partial(_mlstm_run, cfg), _copy_state, _copy_state)


def _mid_rows(g, lev, rev):
    c, w = g.shape
    half = 1 << lev
    blk = 2 * half
    m = half if rev else half - 1
    if blk >= 8:
        g3 = g.reshape(c // blk, blk, w)
        return jnp.broadcast_to(g3[:, m:m + 1, :], g3.shape).reshape(c, w)
    g3 = g.reshape(c // 8, 8, w)
    sub = lax.broadcasted_iota(jnp.int32, g3.shape, 1)
    out = jnp.broadcast_to(g3[:, m:m + 1, :], g3.shape)
    for b0 in range(blk, 8, blk):
        out = jnp.where(sub >= b0, jnp.broadcast_to(g3[:, b0 + m:b0 + m + 1, :], g3.shape), out)
    return out.reshape(c, w)


def _hgrn_run(cfg, shared, sides):
    (tri_ref,) = shared
    c = cfg.c
    levels = int(np.log2(c))
    ones = jnp.ones((H_DK, LANES), BF16)
    nch = cfg.tb // c
    heads = range(H_HEADS)
    dn = (((1,), (1,)), ((), ()))

    pre, pair = {}, {}
    for si, (rev, din, _, _) in enumerate(sides):
        f_ref, lb_ref = din[3], din[4]
        sel = tri_ref[int(rev)]
        lb = lb_ref[0]
        tpos, spos, _ = _scan_masks(c, rev)
        before = (spos > tpos) if rev else (spos < tpos)
        pair[si] = jnp.where(before, tpos ^ spos, 0)
        for ci, r0 in enumerate(_chunk_rows(cfg, rev)):
            f = lb + (1.0 - lb) * _sigmoid(f_ref[r0:r0 + c, :])
            pre[si, ci] = dict(r0=r0, kk=1.0 - f, g=_sum01(sel, jnp.log(f)))
    probs = [(si, ci, h) for si in range(len(sides)) for ci in range(nch) for h in heads]
    st = {}
    for si, ci, h in probs:
        p = pre[si, ci]
        hs = slice(h * H_DK, (h + 1) * H_DK)
        q = sides[si][1][0][p["r0"]:p["r0"] + c, hs]
        kk = p["kk"][:, hs]
        st[si, ci, h] = dict(q=q, kk=kk, qb=q.astype(BF16), kb=kk.astype(BF16), g=p["g"][:, hs],
                             att=jnp.zeros((c, c), F32))
    for lev in range(levels):
        for si, ci, h in probs:
            e = st[si, ci, h]
            x = jnp.exp(-jnp.abs(e["g"] - _mid_rows(e["g"], lev, sides[si][0]))).astype(BF16)
            a = lax.dot_general(e["qb"] * x, e["kb"] * x, dn, preferred_element_type=F32)
            e["att"] = jnp.where((pair[si] >> lev) == 1, a, e["att"])
    for si, ci, h in probs:
        e = st[si, ci, h]
        e["diag"] = jnp.dot((e["q"] * e["kk"]).astype(BF16), ones, preferred_element_type=F32)

    for ci in range(nch):
        step = [(si, h) for si in range(len(sides)) for h in heads]
        for si, h in step:
            e = st[si, ci, h]
            rev, din, y_ref, (s_sc,) = sides[si]
            r0 = pre[si, ci]["r0"]
            hs = slice(h * H_DK, (h + 1) * H_DK)
            v = din[1][r0:r0 + c, hs]
            e["st"] = s_sc[h]
            o = _dot(e["att"], v) + e["diag"] * v + _dot_nt(e["q"] * jnp.exp(e["g"]), e["st"])
            y_ref[r0:r0 + c, hs] = o.astype(y_ref.dtype)
        for si, h in step:
            e = st[si, ci, h]
            rev, din, _, (s_sc,) = sides[si]
            r0 = pre[si, ci]["r0"]
            hs = slice(h * H_DK, (h + 1) * H_DK)
            end = 0 if rev else c - 1
            g_end = e["g"][end:end + 1, :]
            vt = din[2][hs, r0:r0 + c]
            s_sc[h] = jnp.exp(g_end) * e["st"] + _dot(vt, e["kk"] * jnp.exp(g_end - e["g"]))


def _hgrn_load(sin, scr):
    for h in range(H_HEADS):
        scr[0][h] = sin[0][h].T


def _hgrn_store(scr, sout):
    for h in range(H_HEADS):
        sout[0][h] = scr[0][h].T


def _hgrn(cfg, p, vt, lb, tri, s0):
    tb = cfg.tb
    qi = OFF_H // H_W
    dir_inputs = [(p, (tb, H_W), lambda d, b: (b, qi)),
                  (p, (tb, H_W), lambda d, b: (b, qi + 1)),
                  (vt, (H_W, tb), lambda d, b: (0, b)),
                  (p, (tb, H_W), lambda d, b: (b, qi + 3 + d)),
                  (lb, (1, 1, H_W), lambda d, b: (d, 0, 0))]
    return _bidir_call(cfg, "hgrn", H_W, dir_inputs, [tri], [s0], [(H_HEADS, H_DK, H_DV)],
                       [(H_HEADS, H_DV, H_DK)], functools.partial(_hgrn_run, cfg), _hgrn_load, _hgrn_store)


def _ssd_run(cfg, shared, sides):
    tri_ref, ex_ref, sd_ref = shared
    c = cfg.c
    n_rep = S_HEADS // S_GROUPS
    gw = n_rep * S_P
    nch = cfg.tb // c
    lane = lax.broadcasted_iota(jnp.int32, (c, LANES), 1)
    head_of_lane = lax.broadcasted_iota(jnp.int32, (c, gw), 1) // S_P

    pre, mask = {}, {}
    for si, (rev, din, _, _) in enumerate(sides):
        x_ref, cm_ref, bmt_ref, g_ref, gt_ref = din
        dd = int(rev)
        sel, ex = tri_ref[dd], ex_ref[dd]
        mask[si] = _scan_masks(c, rev)[2]
        la0 = GC_DTA + dd * S_HEADS
        is_la = (lane - la0).astype(jnp.uint32) < S_HEADS
        end = 0 if rev else c - 1
        for ci, r0 in enumerate(_chunk_rows(cfg, rev)):
            gc = g_ref[r0:r0 + c, :]
            gtc = gt_ref[:, r0:r0 + c]
            lam = jnp.where(is_la, _sum01(sel, gc), 0.0)
            ela = jnp.exp(lam)
            ea = jnp.dot(ela.astype(BF16), ex, preferred_element_type=F32)
            e8 = 0 if rev else c - 8
            ea_end = _expand2(ela[e8:e8 + 8, :], ex)[end - e8:end - e8 + 1, :]
            wk = jnp.exp(lam[end:end + 1, :] - lam) * pltpu.roll(gc, GC_DTA - GC_DT, 1)
            ewk = jnp.dot(wk.astype(BF16), ex, preferred_element_type=F32)
            xall = x_ref[r0:r0 + c, :]
            pre[si, ci] = dict(r0=r0, la0=la0, lam=lam, ea=ea, xall=xall, ea_end=ea_end,
                               larow=_sum01_nt(gtc[la0:la0 + S_HEADS], sel),
                               dtrow=gtc[GC_DT + dd * S_HEADS:GC_DT + (dd + 1) * S_HEADS],
                               xw=(xall * ewk).astype(BF16))
    probs = [(si, ci, grp) for si in range(len(sides)) for ci in range(nch) for grp in range(S_GROUPS)]
    st = {}
    for si, ci, grp in probs:
        cm_ref, bmt_ref = sides[si][1][1], sides[si][1][2]
        r0 = pre[si, ci]["r0"]
        cmg = cm_ref[r0:r0 + c, grp * S_N:(grp + 1) * S_N].astype(BF16)
        bmt = bmt_ref[grp * S_N:(grp + 1) * S_N, r0:r0 + c]
        st[si, ci, grp] = dict(cmg=cmg, bmt=bmt, cb=jnp.dot(cmg, bmt, preferred_element_type=F32))
    for si, ci, grp in probs:
        e, p = st[si, ci, grp], pre[si, ci]
        xg = p["xall"][:, grp * gw:(grp + 1) * gw]
        lhs, rhs = [], []
        for r in range(n_rep):
            h = grp * n_rep + r
            la_c = p["lam"][:, p["la0"] + h:p["la0"] + h + 1]
            seg = jnp.where(mask[si], jnp.exp(jnp.where(mask[si], la_c - p["larow"][h:h + 1, :], 0.0))
                            * p["dtrow"][h:h + 1, :], 0.0)
            lhs.append((seg * e["cb"]).astype(BF16))
            rhs.append(jnp.where(head_of_lane == r, xg, 0.0).astype(BF16))
        y = jnp.dot(jnp.concatenate(lhs, axis=1), jnp.concatenate(rhs, axis=0), preferred_element_type=F32)
        if not sides[si][0]:
            y = y + sd_ref[:, grp * gw:(grp + 1) * gw] * xg
        e["y"] = y

    for ci in range(nch):
        step = [(si, grp) for si in range(len(sides)) for grp in range(S_GROUPS)]
        for si, grp in step:
            e, p = st[si, ci, grp], pre[si, ci]
            _, _, y_ref, (h_sc,) = sides[si]
            gs = slice(grp * gw, (grp + 1) * gw)
            e["ht"] = h_sc[grp]
            y = e["y"] + p["ea"][:, gs] * jnp.dot(e["cmg"], e["ht"].astype(BF16), preferred_element_type=F32)
            y_ref[p["r0"]:p["r0"] + c, gs] = y.astype(y_ref.dtype)
        for si, grp in step:
            e, p = st[si, ci, grp], pre[si, ci]
            (h_sc,) = sides[si][3]
            gs = slice(grp * gw, (grp + 1) * gw)
            h_sc[grp] = p["ea_end"][:, gs] * e["ht"] + jnp.dot(e["bmt"], p["xw"][:, gs],
                                                               preferred_element_type=F32)


def _ssd_load(sin, scr):
    n_rep = S_HEADS // S_GROUPS
    for grp in range(S_GROUPS):
        scr[0][grp] = sin[0][grp * n_rep:(grp + 1) * n_rep].reshape(n_rep * S_P, S_N).T


def _ssd_store(scr, sout):
    n_rep = S_HEADS // S_GROUPS
    for grp in range(S_GROUPS):
        sout[0][grp * n_rep:(grp + 1) * n_rep] = scr[0][grp].T.reshape(n_rep, S_P, S_N)


def _ssd(cfg, p, bmt, g, gt, tri, ex, sd, h0):
    tb = cfg.tb
    gn = S_GROUPS * S_N
    dir_inputs = [(p, (tb, S_W), lambda d, b: (b, OFF_XBC // S_W)),
                  (p, (tb, gn), lambda d, b: (b, (OFF_XBC + S_W) // gn + 1)),
                  (bmt, (gn, tb), lambda d, b: (0, b)),
                  (g, (tb, LANES), lambda d, b: (b, 0)),
                  (gt, (LANES, tb), lambda d, b: (0, b))]
    return _bidir_call(cfg, "ssd", S_W, dir_inputs, [tri, ex, sd], [h0], [(S_HEADS, S_P, S_N)],
                       [(S_GROUPS, S_N, (S_HEADS // S_GROUPS) * S_P)], functools.partial(_ssd_run, cfg),
                       _ssd_load, _ssd_store)


def _head_norm(y, heads, width):
    outs = []
    for h in range(heads):
        s = y[:, h * width:(h + 1) * width]
        outs.append(s * lax.rsqrt(jnp.mean(s * s, axis=-1, keepdims=True) + EPS))
    return jnp.concatenate(outs, axis=1)


def _merge_kernel(x_ref, gate_ref, mo_ref, hg_ref, sz_ref, bg_ref, ymf_ref, ymb_ref, yhf_ref, yhb_ref,
                  ysf_ref, ysb_ref, mn_ref, hn_ref, sn_ref, wbm_ref, wbh_ref, wbs_ref, wo_ref, o_ref,
                  *, d_model):
    add = lambda a, b: a[...].astype(F32) + b[...].astype(F32)
    y_m = _head_norm(add(ymf_ref, ymb_ref), M_HEADS, M_DH) * mn_ref[...] * _sigmoid(mo_ref[...])
    y_h = _head_norm(add(yhf_ref, yhb_ref), H_HEADS, H_DV) * hn_ref[...] * _silu(hg_ref[...])
    ys = add(ysf_ref, ysb_ref) * _silu(sz_ref[...])
    y_s = ys * lax.rsqrt(jnp.mean(ys * ys, axis=-1, keepdims=True) + EPS) * sn_ref[...]
    bg = bg_ref[...]
    dm = d_model
    merged = (_sigmoid(bg[:, :dm]) * jnp.dot(y_m.astype(BF16), wbm_ref[...], preferred_element_type=F32)
              + _sigmoid(bg[:, dm:2 * dm]) * jnp.dot(y_h.astype(BF16), wbh_ref[...], preferred_element_type=F32)
              + _sigmoid(bg[:, 2 * dm:]) * jnp.dot(y_s.astype(BF16), wbs_ref[...], preferred_element_type=F32))
    out = jnp.dot(merged.astype(BF16), wo_ref[...], preferred_element_type=F32)
    o_ref[...] = x_ref[...] + gate_ref[0] * out


def _merge(cfg, x, mod, p, ym, yh, ys, mn, hn, sn, wbm, wbh, wbs, wo):
    tm = cfg.tm_merge
    row = cfg.mod_row(tm)
    rows_pad = mod.shape[0] // 6
    d = cfg.d
    const = lambda shape: pl.BlockSpec(shape, lambda m: (0,) * len(shape))
    tok = lambda w: pl.BlockSpec((tm, w), lambda m: (m, 0))
    return pl.pallas_call(
        functools.partial(_merge_kernel, d_model=d),
        grid=(cfg.t // tm,),
        in_specs=[tok(d),
                  pl.BlockSpec((1, 1, d), lambda m: (2 * rows_pad + row(m), 0, 0)),
                  pl.BlockSpec((tm, M_W), lambda m: (m, (OFF_M + 3 * M_W) // M_W)),
                  pl.BlockSpec((tm, H_W), lambda m: (m, (OFF_H + 2 * H_W) // H_W)),
                  pl.BlockSpec((tm, S_W), lambda m: (m, OFF_SZ // S_W)),
                  pl.BlockSpec((tm, 3 * d), lambda m: (m, OFF_BG // (3 * d))),
                  tok(M_W), tok(M_W), tok(H_W), tok(H_W), tok(S_W), tok(S_W),
                  const((1, M_W)), const((1, H_W)), const((1, S_W)),
                  const((M_W, d)), const((H_W, d)), const((S_W, d)), const((d, d))],
        out_specs=tok(d),
        out_shape=jax.ShapeDtypeStruct((cfg.t, d), F32),
        compiler_params=_cparams(("arbitrary",)),
        name="merge",
    )(x, mod, p, p, p, p, ym[0], ym[1], yh[0], yh[1], ys[0], ys[1], mn, hn, sn, wbm, wbh, wbs, wo)


def _ffn_up_kernel(x_ref, sc_ref, sh_ref, nw_ref, wa_ref, wb_ref, o_ref, *, d_ff, tc):
    x = x_ref[...]
    y = x * lax.rsqrt(jnp.mean(x * x, axis=-1, keepdims=True) + EPS) * nw_ref[...]
    h = (y * (1.0 + sc_ref[0]) + sh_ref[0]).astype(BF16)
    for j in range(d_ff // tc):
        sl = slice(j * tc, (j + 1) * tc)
        a = jnp.dot(h, wa_ref[:, sl], preferred_element_type=F32)
        b = jnp.dot(h, wb_ref[:, sl], preferred_element_type=F32)
        o_ref[:, sl] = (_silu(a) * b).astype(BF16)


def _ffn_up(cfg, x, mod, nw, w_gu):
    tm = cfg.tm_ffn
    row = cfg.mod_row(tm)
    rows_pad = mod.shape[0] // 6
    d, d_ff = cfg.d, cfg.d_ff
    return pl.pallas_call(
        functools.partial(_ffn_up_kernel, d_ff=d_ff, tc=MXU_N),
        grid=(cfg.t // tm,),
        in_specs=[pl.BlockSpec((tm, d), lambda m: (m, 0)),
                  pl.BlockSpec((1, 1, d), lambda m: (4 * rows_pad + row(m), 0, 0)),
                  pl.BlockSpec((1, 1, d), lambda m: (3 * rows_pad + row(m), 0, 0)),
                  pl.BlockSpec((1, d), lambda m: (0, 0)),
                  pl.BlockSpec((d, d_ff), lambda m: (0, 0)),
                  pl.BlockSpec((d, d_ff), lambda m: (0, 1))],
        out_specs=pl.BlockSpec((tm, d_ff), lambda m: (m, 0)),
        out_shape=jax.ShapeDtypeStruct((cfg.t, d_ff), BF16),
        compiler_params=_cparams(("arbitrary",)),
        name="ffn_up",
    )(x, mod, mod, nw, w_gu, w_gu)


def _ffn_down_kernel(x_ref, gate_ref, a_ref, w_ref, o_ref):
    o_ref[...] = x_ref[...] + gate_ref[0] * jnp.dot(a_ref[...], w_ref[...], preferred_element_type=F32)


def _ffn_down_final_kernel(x_ref, gate_ref, a_ref, w_ref, nf_ref, op_ref, os_ref, *, nct):
    m = pl.program_id(0)
    x = x_ref[...] + gate_ref[0] * jnp.dot(a_ref[...], w_ref[...], preferred_element_type=F32)
    y = x * lax.rsqrt(jnp.mean(x * x, axis=-1, keepdims=True) + EPS) * nf_ref[...]

    @pl.when(m < nct)
    def _():
        op_ref[...] = y

    @pl.when(m >= nct)
    def _():
        os_ref[...] = y


def _ffn_down(cfg, x, mod, act, w_down, norm_f=None):
    tm = cfg.tm_ffn
    row = cfg.mod_row(tm)
    rows_pad = mod.shape[0] // 6
    d, d_ff = cfg.d, cfg.d_ff
    in_specs = [pl.BlockSpec((tm, d), lambda m: (m, 0)),
                pl.BlockSpec((1, 1, d), lambda m: (5 * rows_pad + row(m), 0, 0)),
                pl.BlockSpec((tm, d_ff), lambda m: (m, 0)),
                pl.BlockSpec((d_ff, d), lambda m: (0, 0))]
    if norm_f is None:
        return pl.pallas_call(
            _ffn_down_kernel,
            grid=(cfg.t // tm,),
            in_specs=in_specs,
            out_specs=pl.BlockSpec((tm, d), lambda m: (m, 0)),
            out_shape=jax.ShapeDtypeStruct((cfg.t, d), F32),
            compiler_params=_cparams(("arbitrary",)),
            name="ffn_down",
        )(x, mod, act, w_down)
    nct = cfg.t_ctx // tm
    return pl.pallas_call(
        functools.partial(_ffn_down_final_kernel, nct=nct),
        grid=(cfg.t // tm,),
        in_specs=in_specs + [pl.BlockSpec((1, d), lambda m: (0, 0))],
        out_specs=[pl.BlockSpec((tm, d), lambda m: (jnp.minimum(m, nct - 1), 0)),
                   pl.BlockSpec((tm, d), lambda m: (jnp.maximum(m - nct, 0), 0))],
        out_shape=[jax.ShapeDtypeStruct((cfg.t_ctx, d), F32),
                   jax.ShapeDtypeStruct((cfg.t_smp, d), F32)],
        compiler_params=_cparams(("arbitrary",)),
        name="ffn_down_final",
    )(x, mod, act, w_down, norm_f)


def _grid_pos_embed(rows, d_model):
    quarter = d_model // 4
    freq = POS_BASE ** (-jnp.arange(quarter, dtype=F32) / quarter)
    r = jnp.arange(rows, dtype=F32)[:, None] * freq
    cl = jnp.arange(GRID_W, dtype=F32)[:, None] * freq
    row_e = jnp.concatenate([jnp.sin(r), jnp.cos(r)], axis=-1)
    col_e = jnp.concatenate([jnp.sin(cl), jnp.cos(cl)], axis=-1)
    emb = jnp.concatenate([jnp.broadcast_to(row_e[:, None], (rows, GRID_W, d_model // 2)),
                           jnp.broadcast_to(col_e[None], (rows, GRID_W, d_model // 2))], axis=-1)
    return emb.reshape(rows * GRID_W, d_model)


def kernel(x_prompt, x_sample, state_mlstm_c, state_mlstm_n, state_mlstm_m, state_hgrn, state_ssm, c, c_ctx, w_ada, b_ada, norm1, norm2, w_in, m_bi, m_bf, m_norm, h_lb, h_norm, s_conv_w, s_conv_b, s_dt_bias, s_a_log, s_d, s_norm, w_bm, w_bh, w_bs, w_out, w_gu, w_down, norm_f):
    n_ctx, l_ctx, d = x_prompt.shape
    n_smp, l_smp, _ = x_sample.shape
    depth = w_in.shape[0]
    d_ff = w_down.shape[1]
    cfg = _Cfg(n_ctx, l_ctx, n_smp, l_smp, d, d_ff)

    tri = _tri_consts(cfg.c)
    ex = _expand_consts()

    rows_pad = -(-(n_smp + 1) // 8) * 8
    cvec = jnp.zeros((rows_pad, d), F32).at[:n_smp].set(c).at[n_smp].set(c_ctx)
    mod_all = _ada(cvec, w_ada, b_ada)
    mod_all = mod_all.reshape(depth, rows_pad, 6, d).transpose(0, 2, 1, 3).reshape(depth, 6 * rows_pad, 1, d)

    p_lb = jax.nn.softmax(h_lb.astype(F32), axis=1)
    lower = jnp.cumsum(p_lb, axis=1) - p_lb[:, :1]

    pos = _grid_pos_embed(l_smp // GRID_W, d)
    x = _embed(cfg, x_prompt.reshape(cfg.t_ctx, d), x_sample.reshape(cfg.t_smp, d), pos)

    a_neg = -jnp.exp(s_a_log.astype(F32))
    new_c, new_n, new_m, new_h, new_s = [], [], [], [], []
    y_p = y_s = None
    for i in range(depth):
        w = w_in[i]
        o_mi, o_hq, o_sz, o_xbc, o_dt, o_bg = 2048, 2064, 4624, 5648, 7696, 7728
        w_main = jnp.concatenate([w[:, o_xbc:o_dt], w[:, o_sz:o_xbc], w[:, o_bg:], w[:, :o_mi],
                                  w[:, o_hq:o_sz]], axis=1).astype(BF16)
        w_small = jnp.concatenate([w[:, o_mi:o_hq], w[:, o_dt:o_bg], w[:, o_dt:o_bg],
                                   jnp.zeros((d, LANES - GC_DTA - 2 * S_HEADS), F32)], axis=1).astype(BF16)
        dtb = s_dt_bias[i].reshape(-1)
        gbias = jnp.concatenate([m_bi[i].reshape(-1), m_bf[i].reshape(-1), dtb, dtb,
                                 jnp.zeros((LANES - GC_DTA - 2 * S_HEADS,), F32)]).reshape(1, LANES)
        gmul = jnp.concatenate([jnp.ones((GC_DTA,), F32), a_neg[i].reshape(-1),
                                jnp.zeros((LANES - GC_DTA - 2 * S_HEADS,), F32)]).reshape(1, LANES)
        ncw = -(-CONV_CH // PROJ_TN) * PROJ_TN
        cw = jnp.zeros((3, ncw), F32).at[:, :CONV_CH].set(s_conv_w[i]).reshape(3, ncw // PROJ_TN, PROJ_TN)
        cw = cw.transpose(1, 0, 2)
        cb = jnp.zeros((ncw,), F32).at[:CONV_CH].set(s_conv_b[i]).reshape(ncw // PROJ_TN, 1, PROJ_TN)
        mod = mod_all[i]

        p, g, gt, bmt, kt, vt = _inproj(cfg, x, mod, norm1[i].reshape(1, d), w_main, w_small, gbias, gmul,
                                        cw, cb)

        c0 = jnp.concatenate([state_mlstm_c[:, i], state_mlstm_n[:, i][..., None],
                              jnp.zeros(state_mlstm_c[:, i].shape[:-1] + (M_DH - 1,), F32)], axis=-1)
        m0 = jnp.broadcast_to(state_mlstm_m[:, i][..., None, None], (n_smp, 2, M_HEADS, 1, LANES))
        ymf, ymb, (c_fin, m_fin) = _mlstm(cfg, p, kt, g, gt, tri, c0, m0)
        yhf, yhb, (h_fin,) = _hgrn(cfg, p, vt, lower[:, i].reshape(2, 1, H_W), tri, state_hgrn[:, i])
        sd = jnp.repeat(s_d[i].astype(F32), S_P).reshape(1, S_W)
        ysf, ysb, (s_fin,) = _ssd(cfg, p, bmt, g, gt, tri, ex, sd, state_ssm[:, i])

        x = _merge(cfg, x, mod, p, (ymf, ymb), (yhf, yhb), (ysf, ysb), m_norm[i].reshape(1, M_W),
                   h_norm[i].reshape(1, H_W), s_norm[i].reshape(1, S_W), w_bm[i].astype(BF16),
                   w_bh[i].astype(BF16), w_bs[i].astype(BF16), w_out[i].astype(BF16))
        act = _ffn_up(cfg, x, mod, norm2[i].reshape(1, d), w_gu[i].astype(BF16))
        if i + 1 < depth:
            x = _ffn_down(cfg, x, mod, act, w_down[i].astype(BF16))
        else:
            y_p, y_s = _ffn_down(cfg, x, mod, act, w_down[i].astype(BF16), norm_f.reshape(1, d))

        for cd, md, hd, sd_ in zip(c_fin, m_fin, h_fin, s_fin):
            new_c.append(cd[..., :M_DH])
            new_n.append(cd[..., M_DH])
            new_m.append(md[..., 0, 0])
            new_h.append(hd)
            new_s.append(sd_)

    def layers_dirs(parts):
        a = jnp.stack(parts, axis=1)
        return a.reshape((n_ctx, depth, 2) + a.shape[2:])

    return (y_p.reshape(n_ctx, l_ctx, d), y_s.reshape(n_smp, l_smp, d),
            layers_dirs(new_c), layers_dirs(new_n), layers_dirs(new_m), layers_dirs(new_h), layers_dirs(new_s))
```

```python
import functools

import numpy as np
import jax
import jax.numpy as jnp
from jax import lax
from jax.experimental import pallas as pl
from jax.experimental.pallas import tpu as pltpu

F32 = jnp.float32
BF16 = jnp.bfloat16

GRID_W = 64
EPS = 1e-6
POS_BASE = 10000.0
NEG_BIG = -1e30
M_HEADS, M_DH = 4, 128
H_HEADS, H_DK, H_DV = 4, 128, 128
S_HEADS, S_P, S_GROUPS, S_N = 16, 64, 4, 128
M_W = M_HEADS * M_DH
H_W = H_HEADS * H_DV
S_W = S_HEADS * S_P
CONV_CH = S_W + 2 * S_GROUPS * S_N

LANES = 128
MXU_N = 256
VMEM_LIMIT = 56 * 1024 * 1024
SCAN_CHUNK = 128
SCAN_BLOCK = 256
PROJ_TN = 1536

OFF_XBC, OFF_SZ, OFF_BG, OFF_M, OFF_H = 0, 2048, 3072, 6144, 8192
N_MAIN = 10752
GC_IG, GC_LF, GC_DT, GC_DTA = 0, 8, 16, 48


def _cparams(sem):
    return pltpu.CompilerParams(dimension_semantics=sem, vmem_limit_bytes=VMEM_LIMIT)


NEG_LOG2E = -1.4426950408889634


def _sigmoid(x):
    return 1.0 / (1.0 + jnp.exp2(x * NEG_LOG2E))


def _silu(x):
    return x * _sigmoid(x)


def _softplus(x):
    return jnp.maximum(x, 0.0) + jnp.log(1.0 + jnp.exp(-jnp.abs(x)))


def _dot(a, b):
    return jnp.dot(a.astype(BF16), b.astype(BF16), preferred_element_type=F32)


def _dot_nt(a, b):
    return lax.dot_general(a.astype(BF16), b.astype(BF16), (((1,), (1,)), ((), ())),
                           preferred_element_type=F32)


def _split3(x):
    x1 = x.astype(BF16)
    r1 = x - x1.astype(F32)
    x2 = r1.astype(BF16)
    x3 = (r1 - x2.astype(F32)).astype(BF16)
    return x1, x2, x3


def _sum01(sel, x):
    x1, x2, x3 = _split3(x)
    d = functools.partial(jnp.dot, preferred_element_type=F32)
    return d(sel, x1) + d(sel, x2) + d(sel, x3)


def _sum01_nt(x, sel):
    x1, x2, x3 = _split3(x)
    d = lambda a: lax.dot_general(a, sel, (((1,), (1,)), ((), ())), preferred_element_type=F32)
    return d(x1) + d(x2) + d(x3)


def _expand2(x, e):
    x1 = x.astype(BF16)
    x2 = (x - x1.astype(F32)).astype(BF16)
    return jnp.dot(x1, e, preferred_element_type=F32) + jnp.dot(x2, e, preferred_element_type=F32)


class _Cfg:
    def __init__(self, n_ctx, l_ctx, n_smp, l_smp, d_model, d_ff):
        self.n_ctx, self.l_ctx, self.n_smp, self.l_smp = n_ctx, l_ctx, n_smp, l_smp
        self.d, self.d_ff = d_model, d_ff
        self.t_ctx = n_ctx * l_ctx
        self.t_smp = n_smp * l_smp
        self.t = self.t_ctx + self.t_smp
        self.tb = SCAN_BLOCK
        self.c = SCAN_CHUNK
        assert l_ctx % self.tb == 0 and l_smp % self.tb == 0 and self.tb % self.c == 0
        assert l_ctx & (l_ctx - 1) == 0 and l_smp % GRID_W == 0
        self.nbc = l_ctx // self.tb
        self.nbs = l_smp // self.tb
        self.gc0 = self.t_ctx // self.tb
        self.nb = self.t // self.tb
        self.tm = 1024 if (self.t_ctx % 1024 == 0 and self.t_smp % 1024 == 0 and l_smp % 1024 == 0) else 256
        assert self.t_ctx % self.tm == 0 and l_smp % self.tm == 0 and self.tm % l_ctx == 0
        self.tm_merge = 256
        self.tm_ffn = 512 if self.tm == 1024 else 256

    def mod_row(self, tm):
        nct = self.t_ctx // tm
        per = self.l_smp // tm

        def f(m):
            return jnp.where(m < nct, self.n_smp, jnp.maximum(m - nct, 0) // per)
        return f

    def blk(self, d, j):
        return j + d * (self.nb - 1 - 2 * j)

    def scan_flags(self, d, j):
        g = self.blk(d, j)
        is_ctx = g < self.gc0
        gs = jnp.maximum(g - self.gc0, 0)
        pos = jnp.where(is_ctx, lax.rem(g, self.nbc), lax.rem(gs, self.nbs))
        nblk = jnp.where(is_ctx, self.nbc, self.nbs)
        spos = pos + d * (nblk - 1 - 2 * pos)
        return is_ctx, spos == 0, spos == nblk - 1

    def smp_seq(self, d, j):
        return jnp.clip((self.blk(d, j) - self.gc0) // self.nbs, 0, self.n_smp - 1)

    def ctx_seq(self, d, j):
        return jnp.minimum(self.blk(d, j) // self.nbc, self.n_ctx - 1)


def _tri_consts(c):
    t = np.arange(c)
    fwd = (t[None, :] <= t[:, None]).astype(np.float32)
    return jnp.asarray(np.stack([fwd, fwd.T]), dtype=BF16)


def _expand_consts():
    e = np.zeros((2, LANES, S_W), np.float32)
    for dd in range(2):
        for h in range(S_HEADS):
            e[dd, GC_DTA + dd * S_HEADS + h, h * S_P:(h + 1) * S_P] = 1.0
    return jnp.asarray(e, dtype=BF16)


def _embed_kernel(xp_ref, xs_ref, pos_ref, o_ref, *, nct):
    m = pl.program_id(0)

    @pl.when(m < nct)
    def _():
        o_ref[...] = xp_ref[...]

    @pl.when(m >= nct)
    def _():
        o_ref[...] = xs_ref[...] + pos_ref[...]


def _embed(cfg, xp2, xs2, pos):
    tm = cfg.tm
    nct = cfg.t_ctx // tm
    per = cfg.l_smp // tm
    return pl.pallas_call(
        functools.partial(_embed_kernel, nct=nct),
        grid=(cfg.t // tm,),
        in_specs=[pl.BlockSpec((tm, cfg.d), lambda m: (jnp.minimum(m, nct - 1), 0)),
                  pl.BlockSpec((tm, cfg.d), lambda m: (jnp.maximum(m - nct, 0), 0)),
                  pl.BlockSpec((tm, cfg.d), lambda m: (lax.rem(jnp.maximum(m - nct, 0), per), 0))],
        out_specs=pl.BlockSpec((tm, cfg.d), lambda m: (m, 0)),
        out_shape=jax.ShapeDtypeStruct((cfg.t, cfg.d), F32),
        compiler_params=_cparams(("arbitrary",)),
        name="embed",
    )(xp2, xs2, pos)


def _ada_kernel(c_ref, w_ref, b_ref, o_ref):
    o_ref[0] = _dot(_silu(c_ref[...]), w_ref[0]) + b_ref[0]


def _ada(cvec, w_ada, b_ada):
    depth, d, n = w_ada.shape
    rows = cvec.shape[0]
    tn = 1536
    return pl.pallas_call(
        _ada_kernel,
        grid=(depth, n // tn),
        in_specs=[pl.BlockSpec((rows, d), lambda l, j: (0, 0)),
                  pl.BlockSpec((1, d, tn), lambda l, j: (l, 0, j)),
                  pl.BlockSpec((1, 1, tn), lambda l, j: (l, 0, j))],
        out_specs=pl.BlockSpec((1, rows, tn), lambda l, j: (l, 0, j)),
        out_shape=jax.ShapeDtypeStruct((depth, rows, n), F32),
        compiler_params=_cparams(("arbitrary", "arbitrary")),
        name="ada",
    )(cvec, w_ada, b_ada.reshape(depth, 1, n))


def _inproj_kernel(x_ref, sc_ref, sh_ref, nw_ref, w_ref, ws_ref, gb_ref, ga_ref, cw_ref, cb_ref,
                   p_ref, g_ref, gt_ref, bmt_ref, kt_ref, vt_ref, xn_ref, *, tm, tn, nct, seg_ctx):
    m = pl.program_id(0)
    n = pl.program_id(1)
    gn = S_GROUPS * S_N
    nt = N_MAIN // tn

    def prologue():
        x = x_ref[...]
        y = x * lax.rsqrt(jnp.mean(x * x, axis=-1, keepdims=True) + EPS) * nw_ref[...]
        h = (y * (1.0 + sc_ref[0]) + sh_ref[0]).astype(BF16)
        xn_ref[...] = h
        raw = jnp.dot(h, ws_ref[...], preferred_element_type=F32) + gb_ref[...]
        lane = lax.broadcasted_iota(jnp.int32, raw.shape, 1)
        sp = _softplus(raw)
        g = jnp.where(lane < GC_LF, raw,
                      jnp.where(lane < GC_DT, -_softplus(-raw),
                                jnp.where(lane < GC_DTA, sp, sp * ga_ref[...])))
        g_ref[...] = g
        gt_ref[...] = g.T

    def matmul_cols(c0, c1):
        return jnp.dot(xn_ref[...], w_ref[:, c0:c1], preferred_element_type=F32)

    def conv_tile(tile, ncols):
        row = lax.broadcasted_iota(jnp.int32, (tm, LANES), 0)
        seg_mask = jnp.where(m < nct, seg_ctx - 1, GRID_W - 1)
        inseg = row & seg_mask
        has_prev = inseg != 0
        has_next = inseg != seg_mask
        cw = cw_ref[0]
        if ncols < tn:
            p_ref[:, ncols:] = matmul_cols(ncols, tn)
        accs = [matmul_cols(c0, c0 + MXU_N) for c0 in range(0, ncols, MXU_N)]
        for c0 in range(0, ncols, MXU_N):
            acc = accs[c0 // MXU_N]
            for cb in range(MXU_N // LANES):
                lo = c0 + cb * LANES
                sl = slice(lo, lo + LANES)
                u = acc[:, cb * LANES:(cb + 1) * LANES]
                up = jnp.where(has_prev, pltpu.roll(u, 1, 0), 0.0)
                un = jnp.where(has_next, pltpu.roll(u, tm - 1, 0), 0.0)
                v = _silu(cw[0:1, sl] * up + cw[1:2, sl] * u + cw[2:3, sl] * un + cb_ref[0][:, sl])
                p_ref[:, sl] = v
                col = tile * tn + lo - S_W
                if 0 <= col < gn:
                    bmt_ref[col:col + LANES, :] = v.T.astype(BF16)

    def plain_tile(transposed):
        acc = matmul_cols(0, tn)
        p_ref[...] = acc
        for dst_ref, off in transposed:
            for cb in range(M_W // LANES):
                blk = acc[:, off + cb * LANES:off + (cb + 1) * LANES]
                dst_ref[cb * LANES:(cb + 1) * LANES, :] = blk.T.astype(BF16)

    n_conv = -(-CONV_CH // tn)
    special = {}
    for dst_ref, col in ((kt_ref, OFF_M + M_W), (vt_ref, OFF_H + H_W)):
        tile, off = divmod(col, tn)
        assert off + M_W <= tn and tile >= n_conv
        special.setdefault(tile, []).append((dst_ref, off))

    for tile in range(nt):
        @pl.when(n == tile)
        def _(tile=tile):
            if tile == 0:
                prologue()
            if tile < n_conv:
                conv_tile(tile, min(tn, CONV_CH - tile * tn))
            else:
                plain_tile(special.get(tile, []))


def _inproj(cfg, x, mod, nw, w_main, w_small, gbias, gmul, cw, cb):
    tm, tn = cfg.tm, PROJ_TN
    nct = cfg.t_ctx // tm
    row = cfg.mod_row(tm)
    rows_pad = mod.shape[0] // 6
    nt = N_MAIN // tn
    kern = functools.partial(_inproj_kernel, tm=tm, tn=tn, nct=nct, seg_ctx=cfg.l_ctx)
    tspec = pl.BlockSpec((M_W, tm), lambda m, n: (0, m))
    tshape = jax.ShapeDtypeStruct((M_W, cfg.t), BF16)
    return pl.pallas_call(
        kern,
        grid=(cfg.t // tm, nt),
        in_specs=[pl.BlockSpec((tm, cfg.d), lambda m, n: (m, 0)),
                  pl.BlockSpec((1, 1, cfg.d), lambda m, n: (1 * rows_pad + row(m), 0, 0)),
                  pl.BlockSpec((1, 1, cfg.d), lambda m, n: (0 * rows_pad + row(m), 0, 0)),
                  pl.BlockSpec((1, cfg.d), lambda m, n: (0, 0)),
                  pl.BlockSpec((cfg.d, tn), lambda m, n: (0, n)),
                  pl.BlockSpec((cfg.d, LANES), lambda m, n: (0, 0)),
                  pl.BlockSpec((1, LANES), lambda m, n: (0, 0)),
                  pl.BlockSpec((1, LANES), lambda m, n: (0, 0)),
                  pl.BlockSpec((1, 3, tn), lambda m, n: (jnp.minimum(n, 1), 0, 0)),
                  pl.BlockSpec((1, 1, tn), lambda m, n: (jnp.minimum(n, 1), 0, 0))],
        out_specs=[pl.BlockSpec((tm, tn), lambda m, n: (m, n)),
                   pl.BlockSpec((tm, LANES), lambda m, n: (m, 0)),
                   pl.BlockSpec((LANES, tm), lambda m, n: (0, m)),
                   tspec, tspec, tspec],
        out_shape=[jax.ShapeDtypeStruct((cfg.t, N_MAIN), F32),
                   jax.ShapeDtypeStruct((cfg.t, LANES), F32),
                   jax.ShapeDtypeStruct((LANES, cfg.t), F32),
                   tshape, tshape, tshape],
        scratch_shapes=[pltpu.VMEM((tm, cfg.d), BF16)],
        compiler_params=_cparams(("arbitrary", "arbitrary")),
        name="inproj",
    )(x, mod, mod, nw, w_main, w_small, gbias, gmul, cw, cb)


def _bidir_kernel(*refs, cfg, n_dir, n_shared, n_state, n_scratch, run, load_state, store_state):
    it = iter(refs)
    take = lambda k: [next(it) for _ in range(k)]
    din = [take(n_dir), take(n_dir)]
    shared = take(n_shared)
    sin = [take(n_state), take(n_state)]
    yout = [take(1)[0], take(1)[0]]
    sout = [take(n_state), take(n_state)]
    scr = [take(n_scratch), take(n_scratch)]
    j = pl.program_id(0)
    flags = [cfg.scan_flags(d, j) for d in (0, 1)]

    for d in (0, 1):
        is_ctx, first, _ = flags[d]

        @pl.when(jnp.logical_and(first, is_ctx))
        def _(d=d):
            for s in scr[d]:
                s[...] = jnp.zeros_like(s)

        @pl.when(jnp.logical_and(first, jnp.logical_not(is_ctx)))
        def _(d=d):
            load_state(sin[d], scr[d])

    run(shared, [(bool(d), din[d], yout[d], scr[d]) for d in (0, 1)])

    for d in (0, 1):
        is_ctx, _, last = flags[d]

        @pl.when(jnp.logical_and(last, is_ctx))
        def _(d=d):
            store_state(scr[d], sout[d])


def _bidir_call(cfg, name, y_width, dir_inputs, shared_inputs, state_inputs, state_shapes, scratch_shapes,
                run, load_state, store_state):
    tb = cfg.tb
    in_specs, args = [], []
    for d in (0, 1):
        for arr, bshape, ifn in dir_inputs:
            in_specs.append(pl.BlockSpec(bshape, lambda j, d=d, ifn=ifn: ifn(d, cfg.blk(d, j))))
            args.append(arr)
    for arr in shared_inputs:
        in_specs.append(pl.BlockSpec(arr.shape, lambda j, nd=arr.ndim: (0,) * nd))
        args.append(arr)
    for d in (0, 1):
        for arr, shp in zip(state_inputs, state_shapes):
            in_specs.append(pl.BlockSpec((None, None) + shp,
                                         lambda j, d=d, k=len(shp): (cfg.smp_seq(d, j), d) + (0,) * k))
            args.append(arr)
    out_specs, out_shape = [], []
    for d in (0, 1):
        out_specs.append(pl.BlockSpec((tb, y_width), lambda j, d=d: (cfg.blk(d, j), 0)))
        out_shape.append(jax.ShapeDtypeStruct((cfg.t, y_width), BF16))
    for d in (0, 1):
        for shp in state_shapes:
            out_specs.append(pl.BlockSpec((None,) + shp, lambda j, d=d, k=len(shp): (cfg.ctx_seq(d, j),) + (0,) * k))
            out_shape.append(jax.ShapeDtypeStruct((cfg.n_ctx,) + shp, F32))
    kern = functools.partial(_bidir_kernel, cfg=cfg, n_dir=len(dir_inputs), n_shared=len(shared_inputs),
                             n_state=len(state_inputs), n_scratch=len(scratch_shapes), run=run,
                             load_state=load_state, store_state=store_state)
    outs = pl.pallas_call(
        kern,
        grid=(cfg.nb,),
        in_specs=in_specs,
        out_specs=out_specs,
        out_shape=out_shape,
        scratch_shapes=[pltpu.VMEM(s, F32) for s in scratch_shapes] * 2,
        compiler_params=_cparams(("arbitrary",)),
        name=name,
    )(*args)
    ns = len(state_shapes)
    finals = [(outs[2 + k], outs[2 + ns + k]) for k in range(ns)]
    return outs[0], outs[1], finals


def _chunk_rows(cfg, rev):
    nch = cfg.tb // cfg.c
    return [(nch - 1 - ci if rev else ci) * cfg.c for ci in range(nch)]


def _scan_masks(c, rev):
    tpos = lax.broadcasted_iota(jnp.int32, (c, c), 0)
    spos = lax.broadcasted_iota(jnp.int32, (c, c), 1)
    return tpos, spos, ((spos >= tpos) if rev else (spos <= tpos))


def _mlstm_run(cfg, shared, sides):
    (tri_ref,) = shared
    c = cfg.c
    unit = (lax.broadcasted_iota(jnp.int32, (c, LANES), 1) == 0).astype(F32)
    nch = cfg.tb // c
    heads = range(M_HEADS)

    row = lax.broadcasted_iota(jnp.int32, (c, LANES), 0)
    pre = {}
    for si, (rev, din, _, _) in enumerate(sides):
        g_ref, gt_ref = din[4], din[5]
        sel = tri_ref[int(rev)]
        for ci, r0 in enumerate(_chunk_rows(cfg, rev)):
            gc = g_ref[r0:r0 + c, :]
            gtc = gt_ref[:, r0:r0 + c]
            bcol = _sum01(sel, gc)
            brow = _sum01_nt(gtc[GC_LF:GC_LF + 8], sel)
            ba = pltpu.roll(bcol, LANES - (GC_LF - GC_IG), 1)
            cm = gc - ba
            k = 1
            while k < c:
                if rev:
                    cm = jnp.maximum(cm, jnp.where(row < c - k, pltpu.roll(cm, c - k, 0), NEG_BIG))
                else:
                    cm = jnp.maximum(cm, jnp.where(row >= k, pltpu.roll(cm, k, 0), NEG_BIG))
                k *= 2
            pre[si, ci] = dict(r0=r0, ba=ba, cm=cm,
                               urow=gtc[GC_IG:GC_IG + 8] - brow)
    probs = [(si, ci, h) for si in range(len(sides)) for ci in range(nch) for h in heads]
    st = {}
    for si, ci, h in probs:
        q_ref, k_ref = sides[si][1][0], sides[si][1][1]
        r0 = pre[si, ci]["r0"]
        hs = slice(h * M_DH, (h + 1) * M_DH)
        q = q_ref[r0:r0 + c, hs]
        k = (k_ref[r0:r0 + c, hs] * (M_DH ** -0.5)).astype(BF16)
        st[si, ci, h] = dict(q=q, qk=_dot_nt(q, k))
    masks = [_scan_masks(c, rev)[2] for rev, _, _, _ in sides]
    lane1 = lax.broadcasted_iota(jnp.int32, (1, LANES), 1)

    for ci in range(nch):
        step = [(si, h) for si in range(len(sides)) for h in heads]
        col_q = {}
        for si, (rev, _, _, (_, m_sc)) in enumerate(sides):
            p = pre[si, ci]
            end = 0 if rev else c - 1
            mrow = jnp.zeros((1, LANES), F32)
            for h in heads:
                mrow = jnp.where(lane1 == int(rev) * M_HEADS + h, m_sc[h], mrow)
            z = jnp.maximum(p["cm"], mrow)
            mt = p["ba"] + z
            m_end = mt[end:end + 1, :]
            b_end = p["ba"][end:end + 1, :]
            col_q[si] = dict(z=z, wi=jnp.exp(mrow - z), emt=jnp.exp(-mt), m_end=m_end,
                             bm=b_end - m_end, carry=jnp.exp(b_end + mrow - m_end))
        for si, h in step:
            e, cq = st[si, ci, h], col_q[si]
            col = int(sides[si][0]) * M_HEADS + h
            e["u_r"] = pre[si, ci]["urow"][col:col + 1, :]
            s = e["qk"] * jnp.exp(jnp.where(masks[si], e["u_r"] - cq["z"][:, col:col + 1], NEG_BIG))
            qi = e["q"] * cq["wi"][:, col:col + 1]
            e["lhs"] = jnp.concatenate([s, qi], axis=1).astype(BF16)
        for si, h in step:
            e = st[si, ci, h]
            v_ref, c_sc = sides[si][1][2], sides[si][3][0]
            r0 = pre[si, ci]["r0"]
            hs = slice(h * M_DH, (h + 1) * M_DH)
            e["vaug"] = jnp.concatenate([v_ref[r0:r0 + c, hs], unit], axis=1).astype(BF16)
            e["caug"] = c_sc[h]
            rhs = jnp.concatenate([e["vaug"], e["caug"].astype(BF16)], axis=0)
            e["numden"] = jnp.dot(e["lhs"], rhs, preferred_element_type=F32)
        for si, h in step:
            e = st[si, ci, h]
            rev, din, y_ref, _ = sides[si]
            r0 = pre[si, ci]["r0"]
            hs = slice(h * M_DH, (h + 1) * M_DH)
            cq = col_q[si]
            col = int(rev) * M_HEADS + h
            den = e["numden"][:, M_DH:M_DH + 1]
            hc = e["numden"][:, :M_DH] / jnp.maximum(jnp.abs(den), cq["emt"][:, col:col + 1])
            y_ref[r0:r0 + c, hs] = hc.astype(y_ref.dtype)
            wk = jnp.exp(e["u_r"] + cq["bm"][:, col:col + 1]) * (M_DH ** -0.5)
            e["ktw"] = (din[3][hs, r0:r0 + c].astype(F32) * wk).astype(BF16)
        for si, h in step:
            e, cq = st[si, ci, h], col_q[si]
            c_sc, m_sc = sides[si][3]
            col = int(sides[si][0]) * M_HEADS + h
            c_sc[h] = (cq["carry"][:, col:col + 1] * e["caug"]
                       + jnp.dot(e["ktw"], e["vaug"], preferred_element_type=F32))
            m_sc[h] = jnp.broadcast_to(cq["m_end"][:, col:col + 1], (1, LANES))


def _copy_state(src, dst):
    for s, d in zip(src, dst):
        d[...] = s[...]


def _mlstm(cfg, p, kt, g, gt, tri, c0, m0):
    tb = cfg.tb
    qi = OFF_M // M_W
    dir_inputs = [(p, (tb, M_W), lambda d, b: (b, qi)),
                  (p, (tb, M_W), lambda d, b: (b, qi + 1)),
                  (p, (tb, M_W), lambda d, b: (b, qi + 2)),
                  (kt, (M_W, tb), lambda d, b: (0, b)),
                  (g, (tb, LANES), lambda d, b: (b, 0)),
                  (gt, (LANES, tb), lambda d, b: (0, b))]
    shapes = [(M_HEADS, M_DH, 2 * M_DH), (M_HEADS, 1, LANES)]
    return _bidir_call(cfg, "mlstm", M_W, dir_inputs, [tri], [c0, m0], shapes, shapes,
                       functools.partial(_mlstm_run, cfg), _copy_state, _copy_state)


def _mid_rows(g, lev, rev):
    c, w = g.shape
    half = 1 << lev
    blk = 2 * half
    m = half if rev else half - 1
    if blk >= 8:
        g3 = g.reshape(c // blk, blk, w)
        return jnp.broadcast_to(g3[:, m:m + 1, :], g3.shape).reshape(c, w)
    g3 = g.reshape(c // 8, 8, w)
    sub = lax.broadcasted_iota(jnp.int32, g3.shape, 1)
    out = jnp.broadcast_to(g3[:, m:m + 1, :], g3.shape)
    for b0 in range(blk, 8, blk):
        out = jnp.where(sub >= b0, jnp.broadcast_to(g3[:, b0 + m:b0 + m + 1, :], g3.shape), out)
    return out.reshape(c, w)


def _hgrn_run(cfg, shared, sides):
    (tri_ref,) = shared
    c = cfg.c
    levels = int(np.log2(c))
    ones = jnp.ones((H_DK, LANES), BF16)
    nch = cfg.tb // c
    heads = range(H_HEADS)
    dn = (((1,), (1,)), ((), ()))

    pre, pair = {}, {}
    for si, (rev, din, _, _) in enumerate(sides):
        f_ref, lb_ref = din[3], din[4]
        sel = tri_ref[int(rev)]
        lb = lb_ref[0]
        tpos, spos, _ = _scan_masks(c, rev)
        before = (spos > tpos) if rev else (spos < tpos)
        lvl = jnp.where(before, tpos ^ spos, 0)
        pair[si] = [(lvl >> lev) == 1 for lev in range(levels)]
        for ci, r0 in enumerate(_chunk_rows(cfg, rev)):
            f = lb + (1.0 - lb) * _sigmoid(f_ref[r0:r0 + c, :])
            pre[si, ci] = dict(r0=r0, kk=1.0 - f, g=_sum01(sel, jnp.log(f)))
    probs = [(si, ci, h) for si in range(len(sides)) for ci in range(nch) for h in heads]
    st = {}
    for si, ci, h in probs:
        p = pre[si, ci]
        hs = slice(h * H_DK, (h + 1) * H_DK)
        q = sides[si][1][0][p["r0"]:p["r0"] + c, hs]
        kk = p["kk"][:, hs]
        st[si, ci, h] = dict(q=q, kk=kk, qb=q.astype(BF16), kb=kk.astype(BF16), g=p["g"][:, hs],
                             att=jnp.zeros((c, c), F32))
    for lev in range(levels):
        for si, ci, h in probs:
            e = st[si, ci, h]
            x = jnp.exp(-jnp.abs(e["g"] - _mid_rows(e["g"], lev, sides[si][0]))).astype(BF16)
            a = lax.dot_general(e["qb"] * x, e["kb"] * x, dn, preferred_element_type=F32)
            e["att"] = jnp.where(pair[si][lev], a, e["att"])
    for si, ci, h in probs:
        e = st[si, ci, h]
        e["diag"] = jnp.dot((e["q"] * e["kk"]).astype(BF16), ones, preferred_element_type=F32)

    for ci in range(nch):
        step = [(si, h) for si in range(len(sides)) for h in heads]
        for si, h in step:
            e = st[si, ci, h]
            rev, din, y_ref, (s_sc,) = sides[si]
            r0 = pre[si, ci]["r0"]
            hs = slice(h * H_DK, (h + 1) * H_DK)
            v = din[1][r0:r0 + c, hs]
            e["st"] = s_sc[h]
            o = _dot(e["att"], v) + e["diag"] * v + _dot_nt(e["q"] * jnp.exp(e["g"]), e["st"])
            y_ref[r0:r0 + c, hs] = o.astype(y_ref.dtype)
        for si, h in step:
            e = st[si, ci, h]
            rev, din, _, (s_sc,) = sides[si]
            r0 = pre[si, ci]["r0"]
            hs = slice(h * H_DK, (h + 1) * H_DK)
            end = 0 if rev else c - 1
            g_end = e["g"][end:end + 1, :]
            vt = din[2][hs, r0:r0 + c]
            s_sc[h] = jnp.exp(g_end) * e["st"] + _dot(vt, e["kk"] * jnp.exp(g_end - e["g"]))


def _hgrn_load(sin, scr):
    for h in range(H_HEADS):
        scr[0][h] = sin[0][h].T


def _hgrn_store(scr, sout):
    for h in range(H_HEADS):
        sout[0][h] = scr[0][h].T


def _hgrn(cfg, p, vt, lb, tri, s0):
    tb = cfg.tb
    qi = OFF_H // H_W
    dir_inputs = [(p, (tb, H_W), lambda d, b: (b, qi)),
                  (p, (tb, H_W), lambda d, b: (b, qi + 1)),
                  (vt, (H_W, tb), lambda d, b: (0, b)),
                  (p, (tb, H_W), lambda d, b: (b, qi + 3 + d)),
                  (lb, (1, 1, H_W), lambda d, b: (d, 0, 0))]
    return _bidir_call(cfg, "hgrn", H_W, dir_inputs, [tri], [s0], [(H_HEADS, H_DK, H_DV)],
                       [(H_HEADS, H_DV, H_DK)], functools.partial(_hgrn_run, cfg), _hgrn_load, _hgrn_store)


def _ssd_run(cfg, shared, sides):
    tri_ref, ex_ref, sd_ref = shared
    c = cfg.c
    n_rep = S_HEADS // S_GROUPS
    gw = n_rep * S_P
    nch = cfg.tb // c
    lane = lax.broadcasted_iota(jnp.int32, (c, LANES), 1)
    head_of_lane = lax.broadcasted_iota(jnp.int32, (c, gw), 1) // S_P
    in_head = [head_of_lane == r for r in range(n_rep)]

    rows = {(si, ci): r0 for si, (rev, _, _, _) in enumerate(sides)
            for ci, r0 in enumerate(_chunk_rows(cfg, rev))}
    probs = [(si, ci, grp) for si in range(len(sides)) for ci in range(nch) for grp in range(S_GROUPS)]
    st = {}
    for si, ci, grp in probs:
        cm_ref, bmt_ref = sides[si][1][1], sides[si][1][2]
        r0 = rows[si, ci]
        cmg = cm_ref[r0:r0 + c, grp * S_N:(grp + 1) * S_N].astype(BF16)
        bmt = bmt_ref[grp * S_N:(grp + 1) * S_N, r0:r0 + c]
        st[si, ci, grp] = dict(cmg=cmg, bmt=bmt, cb=jnp.dot(cmg, bmt, preferred_element_type=F32))
    pre, mask = {}, {}
    for si, (rev, din, _, _) in enumerate(sides):
        x_ref, g_ref, gt_ref = din[0], din[3], din[4]
        dd = int(rev)
        sel = tri_ref[dd]
        mask[si] = _scan_masks(c, rev)[2]
        la0 = GC_DTA + dd * S_HEADS
        is_la = (lane - la0).astype(jnp.uint32) < S_HEADS
        for ci in range(nch):
            r0 = rows[si, ci]
            gc = g_ref[r0:r0 + c, :]
            gtc = gt_ref[:, r0:r0 + c]
            pre[si, ci] = dict(r0=r0, la0=la0, gc=gc, xall=x_ref[r0:r0 + c, :],
                               lam=jnp.where(is_la, _sum01(sel, gc), 0.0),
                               larow=_sum01_nt(gtc[la0:la0 + S_HEADS], sel),
                               dtrow=gtc[GC_DT + dd * S_HEADS:GC_DT + (dd + 1) * S_HEADS])

    def expansions(si, ci):
        rev = sides[si][0]
        p = pre[si, ci]
        ex = ex_ref[int(rev)]
        end = 0 if rev else c - 1
        lam = p["lam"]
        ela = jnp.exp(lam)
        p["ea"] = jnp.dot(ela.astype(BF16), ex, preferred_element_type=F32)
        e8 = 0 if rev else c - 8
        p["ea_end"] = _expand2(ela[e8:e8 + 8, :], ex)[end - e8:end - e8 + 1, :]
        wk = jnp.exp(lam[end:end + 1, :] - lam) * pltpu.roll(p["gc"], GC_DTA - GC_DT, 1)
        ewk = jnp.dot(wk.astype(BF16), ex, preferred_element_type=F32)
        p["xw"] = (p["xall"] * ewk).astype(BF16)
    for si, ci, grp in probs:
        e, p = st[si, ci, grp], pre[si, ci]
        xg = p["xall"][:, grp * gw:(grp + 1) * gw]
        lhs, rhs = [], []
        for r in range(n_rep):
            h = grp * n_rep + r
            la_c = p["lam"][:, p["la0"] + h:p["la0"] + h + 1]
            seg = jnp.exp(jnp.where(mask[si], la_c - p["larow"][h:h + 1, :], NEG_BIG)) * p["dtrow"][h:h + 1, :]
            lhs.append((seg * e["cb"]).astype(BF16))
            rhs.append(jnp.where(in_head[r], xg, 0.0).astype(BF16))
        y = jnp.dot(jnp.concatenate(lhs, axis=1), jnp.concatenate(rhs, axis=0), preferred_element_type=F32)
        if not sides[si][0]:
            y = y + sd_ref[:, grp * gw:(grp + 1) * gw] * xg
        e["y"] = y
    for si, ci in rows:
        expansions(si, ci)

    for ci in range(nch):
        step = [(si, grp) for si in range(len(sides)) for grp in range(S_GROUPS)]
        for si, grp in step:
            e, p = st[si, ci, grp], pre[si, ci]
            _, _, y_ref, (h_sc,) = sides[si]
            gs = slice(grp * gw, (grp + 1) * gw)
            e["ht"] = h_sc[grp]
            y = e["y"] + p["ea"][:, gs] * jnp.dot(e["cmg"], e["ht"].astype(BF16), preferred_element_type=F32)
            y_ref[p["r0"]:p["r0"] + c, gs] = y.astype(y_ref.dtype)
        for si, grp in step:
            e, p = st[si, ci, grp], pre[si, ci]
            (h_sc,) = sides[si][3]
            gs = slice(grp * gw, (grp + 1) * gw)
            h_sc[grp] = p["ea_end"][:, gs] * e["ht"] + jnp.dot(e["bmt"], p["xw"][:, gs],
                                                               preferred_element_type=F32)


def _ssd_load(sin, scr):
    n_rep = S_HEADS // S_GROUPS
    for grp in range(S_GROUPS):
        scr[0][grp] = sin[0][grp * n_rep:(grp + 1) * n_rep].reshape(n_rep * S_P, S_N).T


def _ssd_store(scr, sout):
    n_rep = S_HEADS // S_GROUPS
    for grp in range(S_GROUPS):
        sout[0][grp * n_rep:(grp + 1) * n_rep] = scr[0][grp].T.reshape(n_rep, S_P, S_N)


def _ssd(cfg, p, bmt, g, gt, tri, ex, sd, h0):
    tb = cfg.tb
    gn = S_GROUPS * S_N
    dir_inputs = [(p, (tb, S_W), lambda d, b: (b, OFF_XBC // S_W)),
                  (p, (tb, gn), lambda d, b: (b, (OFF_XBC + S_W) // gn + 1)),
                  (bmt, (gn, tb), lambda d, b: (0, b)),
                  (g, (tb, LANES), lambda d, b: (b, 0)),
                  (gt, (LANES, tb), lambda d, b: (0, b))]
    return _bidir_call(cfg, "ssd", S_W, dir_inputs, [tri, ex, sd], [h0], [(S_HEADS, S_P, S_N)],
                       [(S_GROUPS, S_N, (S_HEADS // S_GROUPS) * S_P)], functools.partial(_ssd_run, cfg),
                       _ssd_load, _ssd_store)


def _head_norm(y, heads, width):
    outs = []
    for h in range(heads):
        s = y[:, h * width:(h + 1) * width]
        outs.append(s * lax.rsqrt(jnp.mean(s * s, axis=-1, keepdims=True) + EPS))
    return jnp.concatenate(outs, axis=1)


def _merge_kernel(x_ref, gate_ref, mo_ref, hg_ref, sz_ref, bg_ref, ymf_ref, ymb_ref, yhf_ref, yhb_ref,
                  ysf_ref, ysb_ref, mn_ref, hn_ref, sn_ref, wbm_ref, wbh_ref, wbs_ref, wo_ref, o_ref,
                  *, d_model):
    add = lambda a, b: a[...].astype(F32) + b[...].astype(F32)
    y_m = _head_norm(add(ymf_ref, ymb_ref), M_HEADS, M_DH) * mn_ref[...] * _sigmoid(mo_ref[...])
    y_h = _head_norm(add(yhf_ref, yhb_ref), H_HEADS, H_DV) * hn_ref[...] * _silu(hg_ref[...])
    ys = add(ysf_ref, ysb_ref) * _silu(sz_ref[...])
    y_s = ys * lax.rsqrt(jnp.mean(ys * ys, axis=-1, keepdims=True) + EPS) * sn_ref[...]
    bg = bg_ref[...]
    dm = d_model
    merged = (_sigmoid(bg[:, :dm]) * jnp.dot(y_m.astype(BF16), wbm_ref[...], preferred_element_type=F32)
              + _sigmoid(bg[:, dm:2 * dm]) * jnp.dot(y_h.astype(BF16), wbh_ref[...], preferred_element_type=F32)
              + _sigmoid(bg[:, 2 * dm:]) * jnp.dot(y_s.astype(BF16), wbs_ref[...], preferred_element_type=F32))
    out = jnp.dot(merged.astype(BF16), wo_ref[...], preferred_element_type=F32)
    o_ref[...] = x_ref[...] + gate_ref[0] * out


def _merge(cfg, x, mod, p, ym, yh, ys, mn, hn, sn, wbm, wbh, wbs, wo):
    tm = cfg.tm_merge
    row = cfg.mod_row(tm)
    rows_pad = mod.shape[0] // 6
    d = cfg.d
    const = lambda shape: pl.BlockSpec(shape, lambda m: (0,) * len(shape))
    tok = lambda w: pl.BlockSpec((tm, w), lambda m: (m, 0))
    return pl.pallas_call(
        functools.partial(_merge_kernel, d_model=d),
        grid=(cfg.t // tm,),
        in_specs=[tok(d),
                  pl.BlockSpec((1, 1, d), lambda m: (2 * rows_pad + row(m), 0, 0)),
                  pl.BlockSpec((tm, M_W), lambda m: (m, (OFF_M + 3 * M_W) // M_W)),
                  pl.BlockSpec((tm, H_W), lambda m: (m, (OFF_H + 2 * H_W) // H_W)),
                  pl.BlockSpec((tm, S_W), lambda m: (m, OFF_SZ // S_W)),
                  pl.BlockSpec((tm, 3 * d), lambda m: (m, OFF_BG // (3 * d))),
                  tok(M_W), tok(M_W), tok(H_W), tok(H_W), tok(S_W), tok(S_W),
                  const((1, M_W)), const((1, H_W)), const((1, S_W)),
                  const((M_W, d)), const((H_W, d)), const((S_W, d)), const((d, d))],
        out_specs=tok(d),
        out_shape=jax.ShapeDtypeStruct((cfg.t, d), F32),
        compiler_params=_cparams(("arbitrary",)),
        name="merge",
    )(x, mod, p, p, p, p, ym[0], ym[1], yh[0], yh[1], ys[0], ys[1], mn, hn, sn, wbm, wbh, wbs, wo)


def _ffn_up_kernel(x_ref, sc_ref, sh_ref, nw_ref, wa_ref, wb_ref, o_ref, *, d_ff, tc):
    x = x_ref[...]
    y = x * lax.rsqrt(jnp.mean(x * x, axis=-1, keepdims=True) + EPS) * nw_ref[...]
    h = (y * (1.0 + sc_ref[0]) + sh_ref[0]).astype(BF16)
    for j in range(d_ff // tc):
        sl = slice(j * tc, (j + 1) * tc)
        a = jnp.dot(h, wa_ref[:, sl], preferred_element_type=F32)
        b = jnp.dot(h, wb_ref[:, sl], preferred_element_type=F32)
        o_ref[:, sl] = (_silu(a) * b).astype(BF16)


def _ffn_up(cfg, x, mod, nw, w_gu):
    tm = cfg.tm_ffn
    row = cfg.mod_row(tm)
    rows_pad = mod.shape[0] // 6
    d, d_ff = cfg.d, cfg.d_ff
    return pl.pallas_call(
        functools.partial(_ffn_up_kernel, d_ff=d_ff, tc=MXU_N),
        grid=(cfg.t // tm,),
        in_specs=[pl.BlockSpec((tm, d), lambda m: (m, 0)),
                  pl.BlockSpec((1, 1, d), lambda m: (4 * rows_pad + row(m), 0, 0)),
                  pl.BlockSpec((1, 1, d), lambda m: (3 * rows_pad + row(m), 0, 0)),
                  pl.BlockSpec((1, d), lambda m: (0, 0)),
                  pl.BlockSpec((d, d_ff), lambda m: (0, 0)),
                  pl.BlockSpec((d, d_ff), lambda m: (0, 1))],
        out_specs=pl.BlockSpec((tm, d_ff), lambda m: (m, 0)),
        out_shape=jax.ShapeDtypeStruct((cfg.t, d_ff), BF16),
        compiler_params=_cparams(("arbitrary",)),
        name="ffn_up",
    )(x, mod, mod, nw, w_gu, w_gu)


def _ffn_down_kernel(x_ref, gate_ref, a_ref, w_ref, o_ref):
    o_ref[...] = x_ref[...] + gate_ref[0] * jnp.dot(a_ref[...], w_ref[...], preferred_element_type=F32)


def _ffn_down_final_kernel(x_ref, gate_ref, a_ref, w_ref, nf_ref, op_ref, os_ref, *, nct):
    m = pl.program_id(0)
    x = x_ref[...] + gate_ref[0] * jnp.dot(a_ref[...], w_ref[...], preferred_element_type=F32)
    y = x * lax.rsqrt(jnp.mean(x * x, axis=-1, keepdims=True) + EPS) * nf_ref[...]

    @pl.when(m < nct)
    def _():
        op_ref[...] = y

    @pl.when(m >= nct)
    def _():
        os_ref[...] = y


def _ffn_down(cfg, x, mod, act, w_down, norm_f=None):
    tm = cfg.tm_ffn
    row = cfg.mod_row(tm)
    rows_pad = mod.shape[0] // 6
    d, d_ff = cfg.d, cfg.d_ff
    in_specs = [pl.BlockSpec((tm, d), lambda m: (m, 0)),
                pl.BlockSpec((1, 1, d), lambda m: (5 * rows_pad + row(m), 0, 0)),
                pl.BlockSpec((tm, d_ff), lambda m: (m, 0)),
                pl.BlockSpec((d_ff, d), lambda m: (0, 0))]
    if norm_f is None:
        return pl.pallas_call(
            _ffn_down_kernel,
            grid=(cfg.t // tm,),
            in_specs=in_specs,
            out_specs=pl.BlockSpec((tm, d), lambda m: (m, 0)),
            out_shape=jax.ShapeDtypeStruct((cfg.t, d), F32),
            compiler_params=_cparams(("arbitrary",)),
            name="ffn_down",
        )(x, mod, act, w_down)
    nct = cfg.t_ctx // tm
    return pl.pallas_call(
        functools.partial(_ffn_down_final_kernel, nct=nct),
        grid=(cfg.t // tm,),
        in_specs=in_specs + [pl.BlockSpec((1, d), lambda m: (0, 0))],
        out_specs=[pl.BlockSpec((tm, d), lambda m: (jnp.minimum(m, nct - 1), 0)),
                   pl.BlockSpec((tm, d), lambda m: (jnp.maximum(m - nct, 0), 0))],
        out_shape=[jax.ShapeDtypeStruct((cfg.t_ctx, d), F32),
                   jax.ShapeDtypeStruct((cfg.t_smp, d), F32)],
        compiler_params=_cparams(("arbitrary",)),
        name="ffn_down_final",
    )(x, mod, act, w_down, norm_f)


def _grid_pos_embed(rows, d_model):
    quarter = d_model // 4
    freq = POS_BASE ** (-jnp.arange(quarter, dtype=F32) / quarter)
    r = jnp.arange(rows, dtype=F32)[:, None] * freq
    cl = jnp.arange(GRID_W, dtype=F32)[:, None] * freq
    row_e = jnp.concatenate([jnp.sin(r), jnp.cos(r)], axis=-1)
    col_e = jnp.concatenate([jnp.sin(cl), jnp.cos(cl)], axis=-1)
    emb = jnp.concatenate([jnp.broadcast_to(row_e[:, None], (rows, GRID_W, d_model // 2)),
                           jnp.broadcast_to(col_e[None], (rows, GRID_W, d_model // 2))], axis=-1)
    return emb.reshape(rows * GRID_W, d_model)


def kernel(x_prompt, x_sample, state_mlstm_c, state_mlstm_n, state_mlstm_m, state_hgrn, state_ssm, c, c_ctx, w_ada, b_ada, norm1, norm2, w_in, m_bi, m_bf, m_norm, h_lb, h_norm, s_conv_w, s_conv_b, s_dt_bias, s_a_log, s_d, s_norm, w_bm, w_bh, w_bs, w_out, w_gu, w_down, norm_f):
    n_ctx, l_ctx, d = x_prompt.shape
    n_smp, l_smp, _ = x_sample.shape
    depth = w_in.shape[0]
    d_ff = w_down.shape[1]
    cfg = _Cfg(n_ctx, l_ctx, n_smp, l_smp, d, d_ff)

    tri = _tri_consts(cfg.c)
    ex = _expand_consts()

    rows_pad = -(-(n_smp + 1) // 8) * 8
    cvec = jnp.zeros((rows_pad, d), F32).at[:n_smp].set(c).at[n_smp].set(c_ctx)
    mod_all = _ada(cvec, w_ada, b_ada)
    mod_all = mod_all.reshape(depth, rows_pad, 6, d).transpose(0, 2, 1, 3).reshape(depth, 6 * rows_pad, 1, d)

    p_lb = jax.nn.softmax(h_lb.astype(F32), axis=1)
    lower = jnp.cumsum(p_lb, axis=1) - p_lb[:, :1]

    pos = _grid_pos_embed(l_smp // GRID_W, d)
    x = _embed(cfg, x_prompt.reshape(cfg.t_ctx, d), x_sample.reshape(cfg.t_smp, d), pos)

    a_neg = -jnp.exp(s_a_log.astype(F32))
    new_c, new_n, new_m, new_h, new_s = [], [], [], [], []
    y_p = y_s = None
    for i in range(depth):
        w = w_in[i]
        o_mi, o_hq, o_sz, o_xbc, o_dt, o_bg = 2048, 2064, 4624, 5648, 7696, 7728
        w_main = jnp.concatenate([w[:, o_xbc:o_dt], w[:, o_sz:o_xbc], w[:, o_bg:], w[:, :o_mi],
                                  w[:, o_hq:o_sz]], axis=1).astype(BF16)
        w_small = jnp.concatenate([w[:, o_mi:o_hq], w[:, o_dt:o_bg], w[:, o_dt:o_bg],
                                   jnp.zeros((d, LANES - GC_DTA - 2 * S_HEADS), F32)], axis=1).astype(BF16)
        dtb = s_dt_bias[i].reshape(-1)
        gbias = jnp.concatenate([m_bi[i].reshape(-1), m_bf[i].reshape(-1), dtb, dtb,
                                 jnp.zeros((LANES - GC_DTA - 2 * S_HEADS,), F32)]).reshape(1, LANES)
        gmul = jnp.concatenate([jnp.ones((GC_DTA,), F32), a_neg[i].reshape(-1),
                                jnp.zeros((LANES - GC_DTA - 2 * S_HEADS,), F32)]).reshape(1, LANES)
        ncw = -(-CONV_CH // PROJ_TN) * PROJ_TN
        cw = jnp.zeros((3, ncw), F32).at[:, :CONV_CH].set(s_conv_w[i]).reshape(3, ncw // PROJ_TN, PROJ_TN)
        cw = cw.transpose(1, 0, 2)
        cb = jnp.zeros((ncw,), F32).at[:CONV_CH].set(s_conv_b[i]).reshape(ncw // PROJ_TN, 1, PROJ_TN)
        mod = mod_all[i]

        p, g, gt, bmt, kt, vt = _inproj(cfg, x, mod, norm1[i].reshape(1, d), w_main, w_small, gbias, gmul,
                                        cw, cb)

        c0 = jnp.concatenate([state_mlstm_c[:, i], state_mlstm_n[:, i][..., None],
                              jnp.zeros(state_mlstm_c[:, i].shape[:-1] + (M_DH - 1,), F32)], axis=-1)
        m0 = jnp.broadcast_to(state_mlstm_m[:, i][..., None, None], (n_smp, 2, M_HEADS, 1, LANES))
        ymf, ymb, (c_fin, m_fin) = _mlstm(cfg, p, kt, g, gt, tri, c0, m0)
        yhf, yhb, (h_fin,) = _hgrn(cfg, p, vt, lower[:, i].reshape(2, 1, H_W), tri, state_hgrn[:, i])
        sd = jnp.repeat(s_d[i].astype(F32), S_P).reshape(1, S_W)
        ysf, ysb, (s_fin,) = _ssd(cfg, p, bmt, g, gt, tri, ex, sd, state_ssm[:, i])

        x = _merge(cfg, x, mod, p, (ymf, ymb), (yhf, yhb), (ysf, ysb), m_norm[i].reshape(1, M_W),
                   h_norm[i].reshape(1, H_W), s_norm[i].reshape(1, S_W), w_bm[i].astype(BF16),
                   w_bh[i].astype(BF16), w_bs[i].astype(BF16), w_out[i].astype(BF16))
        act = _ffn_up(cfg, x, mod, norm2[i].reshape(1, d), w_gu[i].astype(BF16))
        if i + 1 < depth:
            x = _ffn_down(cfg, x, mod, act, w_down[i].astype(BF16))
        else:
            y_p, y_s = _ffn_down(cfg, x, mod, act, w_down[i].astype(BF16), norm_f.reshape(1, d))

        for cd, md, hd, sd_ in zip(c_fin, m_fin, h_fin, s_fin):
            new_c.append(cd[..., :M_DH])
            new_n.append(cd[..., M_DH])
            new_m.append(md[..., 0, 0])
            new_h.append(hd)
            new_s.append(sd_)

    def layers_dirs(parts):
        a = jnp.stack(parts, axis=1)
        return a.reshape((n_ctx, depth, 2) + a.shape[2:])

    return (y_p.reshape(n_ctx, l_ctx, d), y_s.reshape(n_smp, l_smp, d),
            layers_dirs(new_c), layers_dirs(new_n), layers_dirs(new_m), layers_dirs(new_h), layers_dirs(new_s))
```

```python
import functools

import numpy as np
import jax
import jax.numpy as jnp
from jax import lax
from jax.experimental import pallas as pl
from jax.experimental.pallas import tpu as pltpu

F32 = jnp.float32
BF16 = jnp.bfloat16

GRID_W = 64
EPS = 1e-6
POS_BASE = 10000.0
NEG_BIG = -1e30
M_HEADS, M_DH = 4, 128
H_HEADS, H_DK, H_DV = 4, 128, 128
S_HEADS, S_P, S_GROUPS, S_N = 16, 64, 4, 128
M_W = M_HEADS * M_DH
H_W = H_HEADS * H_DV
S_W = S_HEADS * S_P
CONV_CH = S_W + 2 * S_GROUPS * S_N

LANES = 128
MXU_N = 256
VMEM_LIMIT = 56 * 1024 * 1024
SCAN_CHUNK = 128
SCAN_BLOCK = 256
PROJ_TN = 1536

OFF_XBC, OFF_SZ, OFF_BG, OFF_M, OFF_H = 0, 2048, 3072, 6144, 8192
N_MAIN = 10752
GC_IG, GC_LF, GC_DT, GC_DTA = 0, 8, 16, 48


def _cparams(sem):
    return pltpu.CompilerParams(dimension_semantics=sem, vmem_limit_bytes=VMEM_LIMIT)


NEG_LOG2E = -1.4426950408889634


def _sigmoid(x):
    return 1.0 / (1.0 + jnp.exp2(x * NEG_LOG2E))


def _silu(x):
    return x * _sigmoid(x)


def _softplus(x):
    return jnp.maximum(x, 0.0) + jnp.log(1.0 + jnp.exp(-jnp.abs(x)))


def _dot(a, b):
    return jnp.dot(a.astype(BF16), b.astype(BF16), preferred_element_type=F32)


def _dot_nt(a, b):
    return lax.dot_general(a.astype(BF16), b.astype(BF16), (((1,), (1,)), ((), ())),
                           preferred_element_type=F32)


def _split3(x):
    x1 = x.astype(BF16)
    r1 = x - x1.astype(F32)
    x2 = r1.astype(BF16)
    x3 = (r1 - x2.astype(F32)).astype(BF16)
    return x1, x2, x3


def _sum01(sel, x):
    x1, x2, x3 = _split3(x)
    d = functools.partial(jnp.dot, preferred_element_type=F32)
    return d(sel, x1) + d(sel, x2) + d(sel, x3)


def _sum01_nt(x, sel):
    x1, x2, x3 = _split3(x)
    d = lambda a: lax.dot_general(a, sel, (((1,), (1,)), ((), ())), preferred_element_type=F32)
    return d(x1) + d(x2) + d(x3)


def _expand2(x, e):
    x1 = x.astype(BF16)
    x2 = (x - x1.astype(F32)).astype(BF16)
    return jnp.dot(x1, e, preferred_element_type=F32) + jnp.dot(x2, e, preferred_element_type=F32)


class _Cfg:
    def __init__(self, n_ctx, l_ctx, n_smp, l_smp, d_model, d_ff):
        self.n_ctx, self.l_ctx, self.n_smp, self.l_smp = n_ctx, l_ctx, n_smp, l_smp
        self.d, self.d_ff = d_model, d_ff
        self.t_ctx = n_ctx * l_ctx
        self.t_smp = n_smp * l_smp
        self.t = self.t_ctx + self.t_smp
        self.tb = SCAN_BLOCK
        self.c = SCAN_CHUNK
        assert l_ctx % self.tb == 0 and l_smp % self.tb == 0 and self.tb % self.c == 0
        assert l_ctx & (l_ctx - 1) == 0 and l_smp % GRID_W == 0
        self.nbc = l_ctx // self.tb
        self.nbs = l_smp // self.tb
        self.gc0 = self.t_ctx // self.tb
        self.nb = self.t // self.tb
        self.tm = 1024 if (self.t_ctx % 1024 == 0 and self.t_smp % 1024 == 0 and l_smp % 1024 == 0) else 256
        assert self.t_ctx % self.tm == 0 and l_smp % self.tm == 0 and self.tm % l_ctx == 0
        self.tm_merge = 256
        self.tm_ffn = 512 if self.tm == 1024 else 256

    def mod_row(self, tm):
        nct = self.t_ctx // tm
        per = self.l_smp // tm

        def f(m):
            return jnp.where(m < nct, self.n_smp, jnp.maximum(m - nct, 0) // per)
        return f

    def blk(self, d, j):
        return j + d * (self.nb - 1 - 2 * j)

    def scan_flags(self, d, j):
        g = self.blk(d, j)
        is_ctx = g < self.gc0
        gs = jnp.maximum(g - self.gc0, 0)
        pos = jnp.where(is_ctx, lax.rem(g, self.nbc), lax.rem(gs, self.nbs))
        nblk = jnp.where(is_ctx, self.nbc, self.nbs)
        spos = pos + d * (nblk - 1 - 2 * pos)
        return is_ctx, spos == 0, spos == nblk - 1

    def smp_seq(self, d, j):
        return jnp.clip((self.blk(d, j) - self.gc0) // self.nbs, 0, self.n_smp - 1)

    def ctx_seq(self, d, j):
        return jnp.minimum(self.blk(d, j) // self.nbc, self.n_ctx - 1)


def _tri_consts(c):
    t = np.arange(c)
    fwd = (t[None, :] <= t[:, None]).astype(np.float32)
    return jnp.asarray(np.stack([fwd, fwd.T]), dtype=BF16)


def _expand_consts():
    e = np.zeros((2, LANES, S_W), np.float32)
    for dd in range(2):
        for h in range(S_HEADS):
            e[dd, GC_DTA + dd * S_HEADS + h, h * S_P:(h + 1) * S_P] = 1.0
    return jnp.asarray(e, dtype=BF16)


def _embed_kernel(xp_ref, xs_ref, pos_ref, o_ref, *, nct):
    m = pl.program_id(0)

    @pl.when(m < nct)
    def _():
        o_ref[...] = xp_ref[...]

    @pl.when(m >= nct)
    def _():
        o_ref[...] = xs_ref[...] + pos_ref[...]


def _embed(cfg, xp2, xs2, pos):
    tm = cfg.tm
    nct = cfg.t_ctx // tm
    per = cfg.l_smp // tm
    return pl.pallas_call(
        functools.partial(_embed_kernel, nct=nct),
        grid=(cfg.t // tm,),
        in_specs=[pl.BlockSpec((tm, cfg.d), lambda m: (jnp.minimum(m, nct - 1), 0)),
                  pl.BlockSpec((tm, cfg.d), lambda m: (jnp.maximum(m - nct, 0), 0)),
                  pl.BlockSpec((tm, cfg.d), lambda m: (lax.rem(jnp.maximum(m - nct, 0), per), 0))],
        out_specs=pl.BlockSpec((tm, cfg.d), lambda m: (m, 0)),
        out_shape=jax.ShapeDtypeStruct((cfg.t, cfg.d), F32),
        compiler_params=_cparams(("arbitrary",)),
        name="embed",
    )(xp2, xs2, pos)


def _ada_kernel(c_ref, w_ref, b_ref, o_ref):
    o_ref[0] = _dot(_silu(c_ref[...]), w_ref[0]) + b_ref[0]


def _ada(cvec, w_ada, b_ada):
    depth, d, n = w_ada.shape
    rows = cvec.shape[0]
    tn = 1536
    return pl.pallas_call(
        _ada_kernel,
        grid=(depth, n // tn),
        in_specs=[pl.BlockSpec((rows, d), lambda l, j: (0, 0)),
                  pl.BlockSpec((1, d, tn), lambda l, j: (l, 0, j)),
                  pl.BlockSpec((1, 1, tn), lambda l, j: (l, 0, j))],
        out_specs=pl.BlockSpec((1, rows, tn), lambda l, j: (l, 0, j)),
        out_shape=jax.ShapeDtypeStruct((depth, rows, n), F32),
        compiler_params=_cparams(("arbitrary", "arbitrary")),
        name="ada",
    )(cvec, w_ada, b_ada.reshape(depth, 1, n))


def _inproj_kernel(x_ref, sc_ref, sh_ref, nw_ref, w_ref, ws_ref, gb_ref, ga_ref, cw_ref, cb_ref,
                   p_ref, pf_ref, g_ref, gt_ref, bmt_ref, kt_ref, vt_ref, xn_ref, *, tm, tn, nct, seg_ctx):
    m = pl.program_id(0)
    n = pl.program_id(1)
    gn = S_GROUPS * S_N
    nt = N_MAIN // tn

    def prologue():
        x = x_ref[...]
        y = x * lax.rsqrt(jnp.mean(x * x, axis=-1, keepdims=True) + EPS) * nw_ref[...]
        h = (y * (1.0 + sc_ref[0]) + sh_ref[0]).astype(BF16)
        xn_ref[...] = h
        raw = jnp.dot(h, ws_ref[...], preferred_element_type=F32) + gb_ref[...]
        lane = lax.broadcasted_iota(jnp.int32, raw.shape, 1)
        sp = _softplus(raw)
        g = jnp.where(lane < GC_LF, raw,
                      jnp.where(lane < GC_DT, -_softplus(-raw),
                                jnp.where(lane < GC_DTA, sp, sp * ga_ref[...])))
        g_ref[...] = g
        gt_ref[...] = g.T

    def matmul_cols(c0, c1):
        return jnp.dot(xn_ref[...], w_ref[:, c0:c1], preferred_element_type=F32)

    def conv_tile(tile, ncols):
        row = lax.broadcasted_iota(jnp.int32, (tm, LANES), 0)
        seg_mask = jnp.where(m < nct, seg_ctx - 1, GRID_W - 1)
        inseg = row & seg_mask
        has_prev = inseg != 0
        has_next = inseg != seg_mask
        cw = cw_ref[0]
        if ncols < tn:
            p_ref[:, ncols:] = matmul_cols(ncols, tn).astype(p_ref.dtype)
        accs = [matmul_cols(c0, c0 + MXU_N) for c0 in range(0, ncols, MXU_N)]
        for c0 in range(0, ncols, MXU_N):
            acc = accs[c0 // MXU_N]
            for cb in range(MXU_N // LANES):
                lo = c0 + cb * LANES
                sl = slice(lo, lo + LANES)
                u = acc[:, cb * LANES:(cb + 1) * LANES]
                up = jnp.where(has_prev, pltpu.roll(u, 1, 0), 0.0)
                un = jnp.where(has_next, pltpu.roll(u, tm - 1, 0), 0.0)
                v = _silu(cw[0:1, sl] * up + cw[1:2, sl] * u + cw[2:3, sl] * un + cb_ref[0][:, sl])
                p_ref[:, sl] = v.astype(p_ref.dtype)
                col = tile * tn + lo - S_W
                if 0 <= col < gn:
                    bmt_ref[col:col + LANES, :] = v.T.astype(BF16)

    f32_groups = {}
    for k, col in enumerate((OFF_H, OFF_H + 3 * H_W)):
        tile, off = divmod(col, tn)
        assert off + 2 * H_W <= tn
        f32_groups[tile] = (k, off)

    def plain_tile(tile, transposed):
        acc = matmul_cols(0, tn)
        p_ref[...] = acc.astype(p_ref.dtype)
        if tile in f32_groups:
            k, off = f32_groups[tile]
            pf_ref[:, k * 2 * H_W:(k + 1) * 2 * H_W] = acc[:, off:off + 2 * H_W]
        for dst_ref, off in transposed:
            for cb in range(M_W // LANES):
                blk = acc[:, off + cb * LANES:off + (cb + 1) * LANES]
                dst_ref[cb * LANES:(cb + 1) * LANES, :] = blk.T.astype(BF16)

    n_conv = -(-CONV_CH // tn)
    special = {}
    for dst_ref, col in ((kt_ref, OFF_M + M_W), (vt_ref, OFF_H + H_W)):
        tile, off = divmod(col, tn)
        assert off + M_W <= tn and tile >= n_conv
        special.setdefault(tile, []).append((dst_ref, off))

    for tile in range(nt):
        @pl.when(n == tile)
        def _(tile=tile):
            if tile == 0:
                prologue()
            if tile < n_conv:
                conv_tile(tile, min(tn, CONV_CH - tile * tn))
            else:
                plain_tile(tile, special.get(tile, []))


def _inproj(cfg, x, mod, nw, w_main, w_small, gbias, gmul, cw, cb):
    tm, tn = cfg.tm, PROJ_TN
    nct = cfg.t_ctx // tm
    row = cfg.mod_row(tm)
    rows_pad = mod.shape[0] // 6
    nt = N_MAIN // tn
    kern = functools.partial(_inproj_kernel, tm=tm, tn=tn, nct=nct, seg_ctx=cfg.l_ctx)
    tspec = pl.BlockSpec((M_W, tm), lambda m, n: (0, m))
    tshape = jax.ShapeDtypeStruct((M_W, cfg.t), BF16)
    return pl.pallas_call(
        kern,
        grid=(cfg.t // tm, nt),
        in_specs=[pl.BlockSpec((tm, cfg.d), lambda m, n: (m, 0)),
                  pl.BlockSpec((1, 1, cfg.d), lambda m, n: (1 * rows_pad + row(m), 0, 0)),
                  pl.BlockSpec((1, 1, cfg.d), lambda m, n: (0 * rows_pad + row(m), 0, 0)),
                  pl.BlockSpec((1, cfg.d), lambda m, n: (0, 0)),
                  pl.BlockSpec((cfg.d, tn), lambda m, n: (0, n)),
                  pl.BlockSpec((cfg.d, LANES), lambda m, n: (0, 0)),
                  pl.BlockSpec((1, LANES), lambda m, n: (0, 0)),
                  pl.BlockSpec((1, LANES), lambda m, n: (0, 0)),
                  pl.BlockSpec((1, 3, tn), lambda m, n: (jnp.minimum(n, 1), 0, 0)),
                  pl.BlockSpec((1, 1, tn), lambda m, n: (jnp.minimum(n, 1), 0, 0))],
        out_specs=[pl.BlockSpec((tm, tn), lambda m, n: (m, n)),
                   pl.BlockSpec((tm, 4 * H_W), lambda m, n: (m, 0)),
                   pl.BlockSpec((tm, LANES), lambda m, n: (m, 0)),
                   pl.BlockSpec((LANES, tm), lambda m, n: (0, m)),
                   tspec, tspec, tspec],
        out_shape=[jax.ShapeDtypeStruct((cfg.t, N_MAIN), BF16),
                   jax.ShapeDtypeStruct((cfg.t, 4 * H_W), F32),
                   jax.ShapeDtypeStruct((cfg.t, LANES), F32),
                   jax.ShapeDtypeStruct((LANES, cfg.t), F32),
                   tshape, tshape, tshape],
        scratch_shapes=[pltpu.VMEM((tm, cfg.d), BF16)],
        compiler_params=_cparams(("arbitrary", "arbitrary")),
        name="inproj",
    )(x, mod, mod, nw, w_main, w_small, gbias, gmul, cw, cb)


def _bidir_kernel(*refs, cfg, n_dir, n_shared, n_state, n_scratch, run, load_state, store_state):
    it = iter(refs)
    take = lambda k: [next(it) for _ in range(k)]
    din = [take(n_dir), take(n_dir)]
    shared = take(n_shared)
    sin = [take(n_state), take(n_state)]
    yout = [take(1)[0], take(1)[0]]
    sout = [take(n_state), take(n_state)]
    scr = [take(n_scratch), take(n_scratch)]
    j = pl.program_id(0)
    flags = [cfg.scan_flags(d, j) for d in (0, 1)]

    for d in (0, 1):
        is_ctx, first, _ = flags[d]

        @pl.when(jnp.logical_and(first, is_ctx))
        def _(d=d):
            for s in scr[d]:
                s[...] = jnp.zeros_like(s)

        @pl.when(jnp.logical_and(first, jnp.logical_not(is_ctx)))
        def _(d=d):
            load_state(sin[d], scr[d])

    run(shared, [(bool(d), din[d], yout[d], scr[d]) for d in (0, 1)])

    for d in (0, 1):
        is_ctx, _, last = flags[d]

        @pl.when(jnp.logical_and(last, is_ctx))
        def _(d=d):
            store_state(scr[d], sout[d])


def _bidir_call(cfg, name, y_width, dir_inputs, shared_inputs, state_inputs, state_shapes, scratch_shapes,
                run, load_state, store_state):
    tb = cfg.tb
    in_specs, args = [], []
    for d in (0, 1):
        for arr, bshape, ifn in dir_inputs:
            in_specs.append(pl.BlockSpec(bshape, lambda j, d=d, ifn=ifn: ifn(d, cfg.blk(d, j))))
            args.append(arr)
    for arr in shared_inputs:
        in_specs.append(pl.BlockSpec(arr.shape, lambda j, nd=arr.ndim: (0,) * nd))
        args.append(arr)
    for d in (0, 1):
        for arr, shp in zip(state_inputs, state_shapes):
            in_specs.append(pl.BlockSpec((None, None) + shp,
                                         lambda j, d=d, k=len(shp): (cfg.smp_seq(d, j), d) + (0,) * k))
            args.append(arr)
    out_specs, out_shape = [], []
    for d in (0, 1):
        out_specs.append(pl.BlockSpec((tb, y_width), lambda j, d=d: (cfg.blk(d, j), 0)))
        out_shape.append(jax.ShapeDtypeStruct((cfg.t, y_width), BF16))
    for d in (0, 1):
        for shp in state_shapes:
            out_specs.append(pl.BlockSpec((None,) + shp, lambda j, d=d, k=len(shp): (cfg.ctx_seq(d, j),) + (0,) * k))
            out_shape.append(jax.ShapeDtypeStruct((cfg.n_ctx,) + shp, F32))
    kern = functools.partial(_bidir_kernel, cfg=cfg, n_dir=len(dir_inputs), n_shared=len(shared_inputs),
                             n_state=len(state_inputs), n_scratch=len(scratch_shapes), run=run,
                             load_state=load_state, store_state=store_state)
    outs = pl.pallas_call(
        kern,
        grid=(cfg.nb,),
        in_specs=in_specs,
        out_specs=out_specs,
        out_shape=out_shape,
        scratch_shapes=[pltpu.VMEM(s, F32) for s in scratch_shapes] * 2,
        compiler_params=_cparams(("arbitrary",)),
        name=name,
    )(*args)
    ns = len(state_shapes)
    finals = [(outs[2 + k], outs[2 + ns + k]) for k in range(ns)]
    return outs[0], outs[1], finals


def _chunk_rows(cfg, rev):
    nch = cfg.tb // cfg.c
    return [(nch - 1 - ci if rev else ci) * cfg.c for ci in range(nch)]


def _scan_masks(c, rev):
    tpos = lax.broadcasted_iota(jnp.int32, (c, c), 0)
    spos = lax.broadcasted_iota(jnp.int32, (c, c), 1)
    return tpos, spos, ((spos >= tpos) if rev else (spos <= tpos))


def _mlstm_run(cfg, shared, sides):
    (tri_ref,) = shared
    c = cfg.c
    unit = (lax.broadcasted_iota(jnp.int32, (c, LANES), 1) == 0).astype(F32)
    nch = cfg.tb // c
    heads = range(M_HEADS)

    row = lax.broadcasted_iota(jnp.int32, (c, LANES), 0)
    pre = {}
    for si, (rev, din, _, _) in enumerate(sides):
        g_ref, gt_ref = din[4], din[5]
        sel = tri_ref[int(rev)]
        for ci, r0 in enumerate(_chunk_rows(cfg, rev)):
            gc = g_ref[r0:r0 + c, :]
            gtc = gt_ref[:, r0:r0 + c]
            bcol = _sum01(sel, gc)
            brow = _sum01_nt(gtc[GC_LF:GC_LF + 8], sel)
            ba = pltpu.roll(bcol, LANES - (GC_LF - GC_IG), 1)
            cm = gc - ba
            k = 1
            while k < c:
                if rev:
                    cm = jnp.maximum(cm, jnp.where(row < c - k, pltpu.roll(cm, c - k, 0), NEG_BIG))
                else:
                    cm = jnp.maximum(cm, jnp.where(row >= k, pltpu.roll(cm, k, 0), NEG_BIG))
                k *= 2
            pre[si, ci] = dict(r0=r0, ba=ba, cm=cm,
                               urow=gtc[GC_IG:GC_IG + 8] - brow)
    probs = [(si, ci, h) for si in range(len(sides)) for ci in range(nch) for h in heads]
    st = {}
    for si, ci, h in probs:
        q_ref, k_ref = sides[si][1][0], sides[si][1][1]
        r0 = pre[si, ci]["r0"]
        hs = slice(h * M_DH, (h + 1) * M_DH)
        q = q_ref[r0:r0 + c, hs]
        st[si, ci, h] = dict(q=q, qk=_dot_nt(q, k_ref[r0:r0 + c, hs]))
    masks = [_scan_masks(c, rev)[2] for rev, _, _, _ in sides]
    lane1 = lax.broadcasted_iota(jnp.int32, (1, LANES), 1)
    log_kscale = float(np.log(M_DH ** -0.5))

    for ci in range(nch):
        step = [(si, h) for si in range(len(sides)) for h in heads]
        col_q = {}
        for si, (rev, _, _, (_, m_sc)) in enumerate(sides):
            p = pre[si, ci]
            end = 0 if rev else c - 1
            mrow = jnp.zeros((1, LANES), F32)
            for h in heads:
                mrow = jnp.where(lane1 == int(rev) * M_HEADS + h, m_sc[h], mrow)
            z = jnp.maximum(p["cm"], mrow)
            mt = p["ba"] + z
            m_end = mt[end:end + 1, :]
            b_end = p["ba"][end:end + 1, :]
            col_q[si] = dict(z=z, wi=jnp.exp(mrow - z), emt=jnp.exp(-mt), m_end=m_end,
                             bm=b_end - m_end, carry=jnp.exp(b_end + mrow - m_end))
        for si, h in step:
            e, cq = st[si, ci, h], col_q[si]
            col = int(sides[si][0]) * M_HEADS + h
            e["u_r"] = pre[si, ci]["urow"][col:col + 1, :]
            s = e["qk"] * jnp.exp(jnp.where(masks[si], (e["u_r"] + log_kscale) - cq["z"][:, col:col + 1], NEG_BIG))
            qi = e["q"].astype(F32) * cq["wi"][:, col:col + 1]
            e["lhs"] = jnp.concatenate([s, qi], axis=1).astype(BF16)
        for si, h in step:
            e = st[si, ci, h]
            v_ref, c_sc = sides[si][1][2], sides[si][3][0]
            r0 = pre[si, ci]["r0"]
            hs = slice(h * M_DH, (h + 1) * M_DH)
            e["vaug"] = jnp.concatenate([v_ref[r0:r0 + c, hs].astype(BF16), unit.astype(BF16)], axis=1)
            e["caug"] = c_sc[h]
            rhs = jnp.concatenate([e["vaug"], e["caug"].astype(BF16)], axis=0)
            e["numden"] = jnp.dot(e["lhs"], rhs, preferred_element_type=F32)
        for si, h in step:
            e = st[si, ci, h]
            rev, din, y_ref, _ = sides[si]
            r0 = pre[si, ci]["r0"]
            hs = slice(h * M_DH, (h + 1) * M_DH)
            cq = col_q[si]
            col = int(rev) * M_HEADS + h
            den = e["numden"][:, M_DH:M_DH + 1]
            hc = e["numden"][:, :M_DH] / jnp.maximum(jnp.abs(den), cq["emt"][:, col:col + 1])
            y_ref[r0:r0 + c, hs] = hc.astype(y_ref.dtype)
            wk = jnp.exp(e["u_r"] + cq["bm"][:, col:col + 1]) * (M_DH ** -0.5)
            e["ktw"] = (din[3][hs, r0:r0 + c].astype(F32) * wk).astype(BF16)
        for si, h in step:
            e, cq = st[si, ci, h], col_q[si]
            c_sc, m_sc = sides[si][3]
            col = int(sides[si][0]) * M_HEADS + h
            c_sc[h] = (cq["carry"][:, col:col + 1] * e["caug"]
                       + jnp.dot(e["ktw"], e["vaug"], preferred_element_type=F32))
            m_sc[h] = jnp.broadcast_to(cq["m_end"][:, col:col + 1], (1, LANES))


def _copy_state(src, dst):
    for s, d in zip(src, dst):
        d[...] = s[...]


def _mlstm(cfg, p, kt, g, gt, tri, c0, m0):
    tb = cfg.tb
    qi = OFF_M // M_W
    dir_inputs = [(p, (tb, M_W), lambda d, b: (b, qi)),
                  (p, (tb, M_W), lambda d, b: (b, qi + 1)),
                  (p, (tb, M_W), lambda d, b: (b, qi + 2)),
                  (kt, (M_W, tb), lambda d, b: (0, b)),
                  (g, (tb, LANES), lambda d, b: (b, 0)),
                  (gt, (LANES, tb), lambda d, b: (0, b))]
    shapes = [(M_HEADS, M_DH, 2 * M_DH), (M_HEADS, 1, LANES)]
    return _bidir_call(cfg, "mlstm", M_W, dir_inputs, [tri], [c0, m0], shapes, shapes,
                       functools.partial(_mlstm_run, cfg), _copy_state, _copy_state)


def _mid_rows(g, lev, rev):
    c, w = g.shape
    half = 1 << lev
    blk = 2 * half
    m = half if rev else half - 1
    if blk >= 8:
        g3 = g.reshape(c // blk, blk, w)
        return jnp.broadcast_to(g3[:, m:m + 1, :], g3.shape).reshape(c, w)
    g3 = g.reshape(c // 8, 8, w)
    sub = lax.broadcasted_iota(jnp.int32, g3.shape, 1)
    out = jnp.broadcast_to(g3[:, m:m + 1, :], g3.shape)
    for b0 in range(blk, 8, blk):
        out = jnp.where(sub >= b0, jnp.broadcast_to(g3[:, b0 + m:b0 + m + 1, :], g3.shape), out)
    return out.reshape(c, w)


def _hgrn_run(cfg, shared, sides):
    (tri_ref,) = shared
    c = cfg.c
    levels = int(np.log2(c))
    ones = jnp.ones((H_DK, LANES), BF16)
    nch = cfg.tb // c
    heads = range(H_HEADS)
    dn = (((1,), (1,)), ((), ()))

    pre, pair = {}, {}
    for si, (rev, din, _, _) in enumerate(sides):
        f_ref, lb_ref = din[3], din[4]
        sel = tri_ref[int(rev)]
        lb = lb_ref[0]
        tpos, spos, _ = _scan_masks(c, rev)
        before = (spos > tpos) if rev else (spos < tpos)
        lvl = jnp.where(before, tpos ^ spos, 0)
        pair[si] = [(lvl >> lev) == 1 for lev in range(levels)]
        for ci, r0 in enumerate(_chunk_rows(cfg, rev)):
            f = lb + (1.0 - lb) * _sigmoid(f_ref[r0:r0 + c, :])
            pre[si, ci] = dict(r0=r0, kk=1.0 - f, g=_sum01(sel, jnp.log(f)))
    probs = [(si, ci, h) for si in range(len(sides)) for ci in range(nch) for h in heads]
    st = {}
    for si, ci, h in probs:
        p = pre[si, ci]
        hs = slice(h * H_DK, (h + 1) * H_DK)
        q = sides[si][1][0][p["r0"]:p["r0"] + c, hs]
        kk = p["kk"][:, hs]
        st[si, ci, h] = dict(q=q, kk=kk, qb=q.astype(BF16), kb=kk.astype(BF16), g=p["g"][:, hs],
                             att=jnp.zeros((c, c), F32))
    for lev in range(levels):
        for si, ci, h in probs:
            e = st[si, ci, h]
            x = jnp.exp(-jnp.abs(e["g"] - _mid_rows(e["g"], lev, sides[si][0]))).astype(BF16)
            a = lax.dot_general(e["qb"] * x, e["kb"] * x, dn, preferred_element_type=F32)
            e["att"] = jnp.where(pair[si][lev], a, e["att"])
    for si, ci, h in probs:
        e = st[si, ci, h]
        e["diag"] = jnp.dot((e["q"] * e["kk"]).astype(BF16), ones, preferred_element_type=F32)

    for ci in range(nch):
        step = [(si, h) for si in range(len(sides)) for h in heads]
        for si, h in step:
            e = st[si, ci, h]
            rev, din, y_ref, (s_sc,) = sides[si]
            r0 = pre[si, ci]["r0"]
            hs = slice(h * H_DK, (h + 1) * H_DK)
            v = din[1][r0:r0 + c, hs]
            e["st"] = s_sc[h]
            o = _dot(e["att"], v) + e["diag"] * v + _dot_nt(e["q"] * jnp.exp(e["g"]), e["st"])
            y_ref[r0:r0 + c, hs] = o.astype(y_ref.dtype)
        for si, h in step:
            e = st[si, ci, h]
            rev, din, _, (s_sc,) = sides[si]
            r0 = pre[si, ci]["r0"]
            hs = slice(h * H_DK, (h + 1) * H_DK)
            end = 0 if rev else c - 1
            g_end = e["g"][end:end + 1, :]
            vt = din[2][hs, r0:r0 + c]
            s_sc[h] = jnp.exp(g_end) * e["st"] + _dot(vt, e["kk"] * jnp.exp(g_end - e["g"]))


def _hgrn_load(sin, scr):
    for h in range(H_HEADS):
        scr[0][h] = sin[0][h].T


def _hgrn_store(scr, sout):
    for h in range(H_HEADS):
        sout[0][h] = scr[0][h].T


def _hgrn(cfg, pf, vt, lb, tri, s0):
    tb = cfg.tb
    dir_inputs = [(pf, (tb, H_W), lambda d, b: (b, 0)),
                  (pf, (tb, H_W), lambda d, b: (b, 1)),
                  (vt, (H_W, tb), lambda d, b: (0, b)),
                  (pf, (tb, H_W), lambda d, b: (b, 2 + d)),
                  (lb, (1, 1, H_W), lambda d, b: (d, 0, 0))]
    return _bidir_call(cfg, "hgrn", H_W, dir_inputs, [tri], [s0], [(H_HEADS, H_DK, H_DV)],
                       [(H_HEADS, H_DV, H_DK)], functools.partial(_hgrn_run, cfg), _hgrn_load, _hgrn_store)


def _ssd_run(cfg, shared, sides):
    tri_ref, ex_ref, sd_ref = shared
    c = cfg.c
    n_rep = S_HEADS // S_GROUPS
    gw = n_rep * S_P
    nch = cfg.tb // c
    lane = lax.broadcasted_iota(jnp.int32, (c, LANES), 1)
    head_of_lane = lax.broadcasted_iota(jnp.int32, (c, gw), 1) // S_P
    in_head = [head_of_lane == r for r in range(n_rep)]

    rows = {(si, ci): r0 for si, (rev, _, _, _) in enumerate(sides)
            for ci, r0 in enumerate(_chunk_rows(cfg, rev))}
    probs = [(si, ci, grp) for si in range(len(sides)) for ci in range(nch) for grp in range(S_GROUPS)]
    st = {}
    for si, ci, grp in probs:
        cm_ref, bmt_ref = sides[si][1][1], sides[si][1][2]
        r0 = rows[si, ci]
        cmg = cm_ref[r0:r0 + c, grp * S_N:(grp + 1) * S_N].astype(BF16)
        bmt = bmt_ref[grp * S_N:(grp + 1) * S_N, r0:r0 + c]
        st[si, ci, grp] = dict(cmg=cmg, bmt=bmt, cb=jnp.dot(cmg, bmt, preferred_element_type=F32))
    pre, mask = {}, {}
    for si, (rev, din, _, _) in enumerate(sides):
        x_ref, g_ref, gt_ref = din[0], din[3], din[4]
        dd = int(rev)
        sel = tri_ref[dd]
        mask[si] = _scan_masks(c, rev)[2]
        la0 = GC_DTA + dd * S_HEADS
        is_la = (lane - la0).astype(jnp.uint32) < S_HEADS
        for ci in range(nch):
            r0 = rows[si, ci]
            gc = g_ref[r0:r0 + c, :]
            gtc = gt_ref[:, r0:r0 + c]
            pre[si, ci] = dict(r0=r0, la0=la0, gc=gc, xall=x_ref[r0:r0 + c, :].astype(F32),
                               lam=jnp.where(is_la, _sum01(sel, gc), 0.0),
                               larow=_sum01_nt(gtc[la0:la0 + S_HEADS], sel),
                               dtrow=gtc[GC_DT + dd * S_HEADS:GC_DT + (dd + 1) * S_HEADS])

    def expansions(si, ci):
        rev = sides[si][0]
        p = pre[si, ci]
        ex = ex_ref[int(rev)]
        end = 0 if rev else c - 1
        lam = p["lam"]
        ela = jnp.exp(lam)
        p["ea"] = jnp.dot(ela.astype(BF16), ex, preferred_element_type=F32)
        e8 = 0 if rev else c - 8
        p["ea_end"] = _expand2(ela[e8:e8 + 8, :], ex)[end - e8:end - e8 + 1, :]
        wk = jnp.exp(lam[end:end + 1, :] - lam) * pltpu.roll(p["gc"], GC_DTA - GC_DT, 1)
        ewk = jnp.dot(wk.astype(BF16), ex, preferred_element_type=F32)
        p["xw"] = (p["xall"] * ewk).astype(BF16)
    for si, ci, grp in probs:
        e, p = st[si, ci, grp], pre[si, ci]
        xg = p["xall"][:, grp * gw:(grp + 1) * gw]
        lhs, rhs = [], []
        for r in range(n_rep):
            h = grp * n_rep + r
            la_c = p["lam"][:, p["la0"] + h:p["la0"] + h + 1]
            seg = jnp.exp(jnp.where(mask[si], la_c - p["larow"][h:h + 1, :], NEG_BIG)) * p["dtrow"][h:h + 1, :]
            lhs.append((seg * e["cb"]).astype(BF16))
            rhs.append(jnp.where(in_head[r], xg, 0.0).astype(BF16))
        y = jnp.dot(jnp.concatenate(lhs, axis=1), jnp.concatenate(rhs, axis=0), preferred_element_type=F32)
        if not sides[si][0]:
            y = y + sd_ref[:, grp * gw:(grp + 1) * gw] * xg
        e["y"] = y
    for si, ci in rows:
        expansions(si, ci)

    for ci in range(nch):
        step = [(si, grp) for si in range(len(sides)) for grp in range(S_GROUPS)]
        for si, grp in step:
            e, p = st[si, ci, grp], pre[si, ci]
            _, _, y_ref, (h_sc,) = sides[si]
            gs = slice(grp * gw, (grp + 1) * gw)
            e["ht"] = h_sc[grp]
            y = e["y"] + p["ea"][:, gs] * jnp.dot(e["cmg"], e["ht"].astype(BF16), preferred_element_type=F32)
            y_ref[p["r0"]:p["r0"] + c, gs] = y.astype(y_ref.dtype)
        for si, grp in step:
            e, p = st[si, ci, grp], pre[si, ci]
            (h_sc,) = sides[si][3]
            gs = slice(grp * gw, (grp + 1) * gw)
            h_sc[grp] = p["ea_end"][:, gs] * e["ht"] + jnp.dot(e["bmt"], p["xw"][:, gs],
                                                               preferred_element_type=F32)


def _ssd_load(sin, scr):
    n_rep = S_HEADS // S_GROUPS
    for grp in range(S_GROUPS):
        scr[0][grp] = sin[0][grp * n_rep:(grp + 1) * n_rep].reshape(n_rep * S_P, S_N).T


def _ssd_store(scr, sout):
    n_rep = S_HEADS // S_GROUPS
    for grp in range(S_GROUPS):
        sout[0][grp * n_rep:(grp + 1) * n_rep] = scr[0][grp].T.reshape(n_rep, S_P, S_N)


def _ssd(cfg, p, bmt, g, gt, tri, ex, sd, h0):
    tb = cfg.tb
    gn = S_GROUPS * S_N
    dir_inputs = [(p, (tb, S_W), lambda d, b: (b, OFF_XBC // S_W)),
                  (p, (tb, gn), lambda d, b: (b, (OFF_XBC + S_W) // gn + 1)),
                  (bmt, (gn, tb), lambda d, b: (0, b)),
                  (g, (tb, LANES), lambda d, b: (b, 0)),
                  (gt, (LANES, tb), lambda d, b: (0, b))]
    return _bidir_call(cfg, "ssd", S_W, dir_inputs, [tri, ex, sd], [h0], [(S_HEADS, S_P, S_N)],
                       [(S_GROUPS, S_N, (S_HEADS // S_GROUPS) * S_P)], functools.partial(_ssd_run, cfg),
                       _ssd_load, _ssd_store)


def _head_norm(y, heads, width):
    outs = []
    for h in range(heads):
        s = y[:, h * width:(h + 1) * width]
        outs.append(s * lax.rsqrt(jnp.mean(s * s, axis=-1, keepdims=True) + EPS))
    return jnp.concatenate(outs, axis=1)


def _merge_kernel(x_ref, gate_ref, mo_ref, hg_ref, sz_ref, bg_ref, ymf_ref, ymb_ref, yhf_ref, yhb_ref,
                  ysf_ref, ysb_ref, mn_ref, hn_ref, sn_ref, wbm_ref, wbh_ref, wbs_ref, wo_ref, o_ref,
                  *, d_model):
    add = lambda a, b: a[...].astype(F32) + b[...].astype(F32)
    f32 = lambda r: r[...].astype(F32)
    y_m = _head_norm(add(ymf_ref, ymb_ref), M_HEADS, M_DH) * mn_ref[...] * _sigmoid(f32(mo_ref))
    y_h = _head_norm(add(yhf_ref, yhb_ref), H_HEADS, H_DV) * hn_ref[...] * _silu(f32(hg_ref))
    ys = add(ysf_ref, ysb_ref) * _silu(f32(sz_ref))
    y_s = ys * lax.rsqrt(jnp.mean(ys * ys, axis=-1, keepdims=True) + EPS) * sn_ref[...]
    bg = f32(bg_ref)
    dm = d_model
    merged = (_sigmoid(bg[:, :dm]) * jnp.dot(y_m.astype(BF16), wbm_ref[...], preferred_element_type=F32)
              + _sigmoid(bg[:, dm:2 * dm]) * jnp.dot(y_h.astype(BF16), wbh_ref[...], preferred_element_type=F32)
              + _sigmoid(bg[:, 2 * dm:]) * jnp.dot(y_s.astype(BF16), wbs_ref[...], preferred_element_type=F32))
    out = jnp.dot(merged.astype(BF16), wo_ref[...], preferred_element_type=F32)
    o_ref[...] = x_ref[...] + gate_ref[0] * out


def _merge(cfg, x, mod, p, ym, yh, ys, mn, hn, sn, wbm, wbh, wbs, wo):
    tm = cfg.tm_merge
    row = cfg.mod_row(tm)
    rows_pad = mod.shape[0] // 6
    d = cfg.d
    const = lambda shape: pl.BlockSpec(shape, lambda m: (0,) * len(shape))
    tok = lambda w: pl.BlockSpec((tm, w), lambda m: (m, 0))
    return pl.pallas_call(
        functools.partial(_merge_kernel, d_model=d),
        grid=(cfg.t // tm,),
        in_specs=[tok(d),
                  pl.BlockSpec((1, 1, d), lambda m: (2 * rows_pad + row(m), 0, 0)),
                  pl.BlockSpec((tm, M_W), lambda m: (m, (OFF_M + 3 * M_W) // M_W)),
                  pl.BlockSpec((tm, H_W), lambda m: (m, (OFF_H + 2 * H_W) // H_W)),
                  pl.BlockSpec((tm, S_W), lambda m: (m, OFF_SZ // S_W)),
                  pl.BlockSpec((tm, 3 * d), lambda m: (m, OFF_BG // (3 * d))),
                  tok(M_W), tok(M_W), tok(H_W), tok(H_W), tok(S_W), tok(S_W),
                  const((1, M_W)), const((1, H_W)), const((1, S_W)),
                  const((M_W, d)), const((H_W, d)), const((S_W, d)), const((d, d))],
        out_specs=tok(d),
        out_shape=jax.ShapeDtypeStruct((cfg.t, d), F32),
        compiler_params=_cparams(("arbitrary",)),
        name="merge",
    )(x, mod, p, p, p, p, ym[0], ym[1], yh[0], yh[1], ys[0], ys[1], mn, hn, sn, wbm, wbh, wbs, wo)


def _ffn_up_kernel(x_ref, sc_ref, sh_ref, nw_ref, wa_ref, wb_ref, o_ref, *, d_ff, tc):
    x = x_ref[...]
    y = x * lax.rsqrt(jnp.mean(x * x, axis=-1, keepdims=True) + EPS) * nw_ref[...]
    h = (y * (1.0 + sc_ref[0]) + sh_ref[0]).astype(BF16)
    for j in range(d_ff // tc):
        sl = slice(j * tc, (j + 1) * tc)
        a = jnp.dot(h, wa_ref[:, sl], preferred_element_type=F32)
        b = jnp.dot(h, wb_ref[:, sl], preferred_element_type=F32)
        o_ref[:, sl] = (_silu(a) * b).astype(BF16)


def _ffn_up(cfg, x, mod, nw, w_gu):
    tm = cfg.tm_ffn
    row = cfg.mod_row(tm)
    rows_pad = mod.shape[0] // 6
    d, d_ff = cfg.d, cfg.d_ff
    return pl.pallas_call(
        functools.partial(_ffn_up_kernel, d_ff=d_ff, tc=MXU_N),
        grid=(cfg.t // tm,),
        in_specs=[pl.BlockSpec((tm, d), lambda m: (m, 0)),
                  pl.BlockSpec((1, 1, d), lambda m: (4 * rows_pad + row(m), 0, 0)),
                  pl.BlockSpec((1, 1, d), lambda m: (3 * rows_pad + row(m), 0, 0)),
                  pl.BlockSpec((1, d), lambda m: (0, 0)),
                  pl.BlockSpec((d, d_ff), lambda m: (0, 0)),
                  pl.BlockSpec((d, d_ff), lambda m: (0, 1))],
        out_specs=pl.BlockSpec((tm, d_ff), lambda m: (m, 0)),
        out_shape=jax.ShapeDtypeStruct((cfg.t, d_ff), BF16),
        compiler_params=_cparams(("arbitrary",)),
        name="ffn_up",
    )(x, mod, mod, nw, w_gu, w_gu)


def _ffn_down_kernel(x_ref, gate_ref, a_ref, w_ref, o_ref):
    o_ref[...] = x_ref[...] + gate_ref[0] * jnp.dot(a_ref[...], w_ref[...], preferred_element_type=F32)


def _ffn_down_final_kernel(x_ref, gate_ref, a_ref, w_ref, nf_ref, op_ref, os_ref, *, nct):
    m = pl.program_id(0)
    x = x_ref[...] + gate_ref[0] * jnp.dot(a_ref[...], w_ref[...], preferred_element_type=F32)
    y = x * lax.rsqrt(jnp.mean(x * x, axis=-1, keepdims=True) + EPS) * nf_ref[...]

    @pl.when(m < nct)
    def _():
        op_ref[...] = y

    @pl.when(m >= nct)
    def _():
        os_ref[...] = y


def _ffn_down(cfg, x, mod, act, w_down, norm_f=None):
    tm = cfg.tm_ffn
    row = cfg.mod_row(tm)
    rows_pad = mod.shape[0] // 6
    d, d_ff = cfg.d, cfg.d_ff
    in_specs = [pl.BlockSpec((tm, d), lambda m: (m, 0)),
                pl.BlockSpec((1, 1, d), lambda m: (5 * rows_pad + row(m), 0, 0)),
                pl.BlockSpec((tm, d_ff), lambda m: (m, 0)),
                pl.BlockSpec((d_ff, d), lambda m: (0, 0))]
    if norm_f is None:
        return pl.pallas_call(
            _ffn_down_kernel,
            grid=(cfg.t // tm,),
            in_specs=in_specs,
            out_specs=pl.BlockSpec((tm, d), lambda m: (m, 0)),
            out_shape=jax.ShapeDtypeStruct((cfg.t, d), F32),
            compiler_params=_cparams(("arbitrary",)),
            name="ffn_down",
        )(x, mod, act, w_down)
    nct = cfg.t_ctx // tm
    return pl.pallas_call(
        functools.partial(_ffn_down_final_kernel, nct=nct),
        grid=(cfg.t // tm,),
        in_specs=in_specs + [pl.BlockSpec((1, d), lambda m: (0, 0))],
        out_specs=[pl.BlockSpec((tm, d), lambda m: (jnp.minimum(m, nct - 1), 0)),
                   pl.BlockSpec((tm, d), lambda m: (jnp.maximum(m - nct, 0), 0))],
        out_shape=[jax.ShapeDtypeStruct((cfg.t_ctx, d), F32),
                   jax.ShapeDtypeStruct((cfg.t_smp, d), F32)],
        compiler_params=_cparams(("arbitrary",)),
        name="ffn_down_final",
    )(x, mod, act, w_down, norm_f)


def _grid_pos_embed(rows, d_model):
    quarter = d_model // 4
    freq = POS_BASE ** (-jnp.arange(quarter, dtype=F32) / quarter)
    r = jnp.arange(rows, dtype=F32)[:, None] * freq
    cl = jnp.arange(GRID_W, dtype=F32)[:, None] * freq
    row_e = jnp.concatenate([jnp.sin(r), jnp.cos(r)], axis=-1)
    col_e = jnp.concatenate([jnp.sin(cl), jnp.cos(cl)], axis=-1)
    emb = jnp.concatenate([jnp.broadcast_to(row_e[:, None], (rows, GRID_W, d_model // 2)),
                           jnp.broadcast_to(col_e[None], (rows, GRID_W, d_model // 2))], axis=-1)
    return emb.reshape(rows * GRID_W, d_model)


def kernel(x_prompt, x_sample, state_mlstm_c, state_mlstm_n, state_mlstm_m, state_hgrn, state_ssm, c, c_ctx, w_ada, b_ada, norm1, norm2, w_in, m_bi, m_bf, m_norm, h_lb, h_norm, s_conv_w, s_conv_b, s_dt_bias, s_a_log, s_d, s_norm, w_bm, w_bh, w_bs, w_out, w_gu, w_down, norm_f):
    n_ctx, l_ctx, d = x_prompt.shape
    n_smp, l_smp, _ = x_sample.shape
    depth = w_in.shape[0]
    d_ff = w_down.shape[1]
    cfg = _Cfg(n_ctx, l_ctx, n_smp, l_smp, d, d_ff)

    tri = _tri_consts(cfg.c)
    ex = _expand_consts()

    rows_pad = -(-(n_smp + 1) // 8) * 8
    cvec = jnp.zeros((rows_pad, d), F32).at[:n_smp].set(c).at[n_smp].set(c_ctx)
    mod_all = _ada(cvec, w_ada, b_ada)
    mod_all = mod_all.reshape(depth, rows_pad, 6, d).transpose(0, 2, 1, 3).reshape(depth, 6 * rows_pad, 1, d)

    p_lb = jax.nn.softmax(h_lb.astype(F32), axis=1)
    lower = jnp.cumsum(p_lb, axis=1) - p_lb[:, :1]

    pos = _grid_pos_embed(l_smp // GRID_W, d)
    x = _embed(cfg, x_prompt.reshape(cfg.t_ctx, d), x_sample.reshape(cfg.t_smp, d), pos)

    a_neg = -jnp.exp(s_a_log.astype(F32))
    new_c, new_n, new_m, new_h, new_s = [], [], [], [], []
    y_p = y_s = None
    for i in range(depth):
        w = w_in[i]
        o_mi, o_hq, o_sz, o_xbc, o_dt, o_bg = 2048, 2064, 4624, 5648, 7696, 7728
        w_main = jnp.concatenate([w[:, o_xbc:o_dt], w[:, o_sz:o_xbc], w[:, o_bg:], w[:, :o_mi],
                                  w[:, o_hq:o_sz]], axis=1).astype(BF16)
        w_small = jnp.concatenate([w[:, o_mi:o_hq], w[:, o_dt:o_bg], w[:, o_dt:o_bg],
                                   jnp.zeros((d, LANES - GC_DTA - 2 * S_HEADS), F32)], axis=1).astype(BF16)
        dtb = s_dt_bias[i].reshape(-1)
        gbias = jnp.concatenate([m_bi[i].reshape(-1), m_bf[i].reshape(-1), dtb, dtb,
                                 jnp.zeros((LANES - GC_DTA - 2 * S_HEADS,), F32)]).reshape(1, LANES)
        gmul = jnp.concatenate([jnp.ones((GC_DTA,), F32), a_neg[i].reshape(-1),
                                jnp.zeros((LANES - GC_DTA - 2 * S_HEADS,), F32)]).reshape(1, LANES)
        ncw = -(-CONV_CH // PROJ_TN) * PROJ_TN
        cw = jnp.zeros((3, ncw), F32).at[:, :CONV_CH].set(s_conv_w[i]).reshape(3, ncw // PROJ_TN, PROJ_TN)
        cw = cw.transpose(1, 0, 2)
        cb = jnp.zeros((ncw,), F32).at[:CONV_CH].set(s_conv_b[i]).reshape(ncw // PROJ_TN, 1, PROJ_TN)
        mod = mod_all[i]

        p, pf, g, gt, bmt, kt, vt = _inproj(cfg, x, mod, norm1[i].reshape(1, d), w_main, w_small, gbias, gmul,
                                            cw, cb)

        c0 = jnp.concatenate([state_mlstm_c[:, i], state_mlstm_n[:, i][..., None],
                              jnp.zeros(state_mlstm_c[:, i].shape[:-1] + (M_DH - 1,), F32)], axis=-1)
        m0 = jnp.broadcast_to(state_mlstm_m[:, i][..., None, None], (n_smp, 2, M_HEADS, 1, LANES))
        ymf, ymb, (c_fin, m_fin) = _mlstm(cfg, p, kt, g, gt, tri, c0, m0)
        yhf, yhb, (h_fin,) = _hgrn(cfg, pf, vt, lower[:, i].reshape(2, 1, H_W), tri, state_hgrn[:, i])
        sd = jnp.repeat(s_d[i].astype(F32), S_P).reshape(1, S_W)
        ysf, ysb, (s_fin,) = _ssd(cfg, p, bmt, g, gt, tri, ex, sd, state_ssm[:, i])

        x = _merge(cfg, x, mod, p, (ymf, ymb), (yhf, yhb), (ysf, ysb), m_norm[i].reshape(1, M_W),
                   h_norm[i].reshape(1, H_W), s_norm[i].reshape(1, S_W), w_bm[i].astype(BF16),
                   w_bh[i].astype(BF16), w_bs[i].astype(BF16), w_out[i].astype(BF16))
        act = _ffn_up(cfg, x, mod, norm2[i].reshape(1, d), w_gu[i].astype(BF16))
        if i + 1 < depth:
            x = _ffn_down(cfg, x, mod, act, w_down[i].astype(BF16))
        else:
            y_p, y_s = _ffn_down(cfg, x, mod, act, w_down[i].astype(BF16), norm_f.reshape(1, d))

        for cd, md, hd, sd_ in zip(c_fin, m_fin, h_fin, s_fin):
            new_c.append(cd[..., :M_DH])
            new_n.append(cd[..., M_DH])
            new_m.append(md[..., 0, 0])
            new_h.append(hd)
            new_s.append(sd_)

    def layers_dirs(parts):
        a = jnp.stack(parts, axis=1)
        return a.reshape((n_ctx, depth, 2) + a.shape[2:])

    return (y_p.reshape(n_ctx, l_ctx, d), y_s.reshape(n_smp, l_smp, d),
            layers_dirs(new_c), layers_dirs(new_n), layers_dirs(new_m), layers_dirs(new_h), layers_dirs(new_s))
```

```python
import functools

import numpy as np
import jax
import jax.numpy as jnp
from jax import lax
from jax.experimental import pallas as pl
from jax.experimental.pallas import tpu as pltpu

F32 = jnp.float32
BF16 = jnp.bfloat16

GRID_W = 64
EPS = 1e-6
POS_BASE = 10000.0
NEG_BIG = -1e30
M_HEADS, M_DH = 4, 128
H_HEADS, H_DK, H_DV = 4, 128, 128
S_HEADS, S_P, S_GROUPS, S_N = 16, 64, 4, 128
M_W = M_HEADS * M_DH
H_W = H_HEADS * H_DV
S_W = S_HEADS * S_P
CONV_CH = S_W + 2 * S_GROUPS * S_N

LANES = 128
MXU_N = 256
VMEM_LIMIT = 56 * 1024 * 1024
SCAN_CHUNK = 128
SCAN_BLOCK = 256
PROJ_TN = 1536

OFF_XBC, OFF_SZ, OFF_BG, OFF_M, OFF_H = 0, 2048, 3072, 6144, 8192
N_MAIN = 10752
GC_IG, GC_LF, GC_DT, GC_DTA = 0, 8, 16, 48


def _cparams(sem):
    return pltpu.CompilerParams(dimension_semantics=sem, vmem_limit_bytes=VMEM_LIMIT)


NEG_LOG2E = -1.4426950408889634


def _sigmoid(x):
    return 1.0 / (1.0 + jnp.exp2(x * NEG_LOG2E))


def _silu(x):
    return x * _sigmoid(x)


def _softplus(x):
    return jnp.maximum(x, 0.0) + jnp.log(1.0 + jnp.exp(-jnp.abs(x)))


def _dot(a, b):
    return jnp.dot(a.astype(BF16), b.astype(BF16), preferred_element_type=F32)


def _dot_nt(a, b):
    return lax.dot_general(a.astype(BF16), b.astype(BF16), (((1,), (1,)), ((), ())),
                           preferred_element_type=F32)


def _split3(x):
    x1 = x.astype(BF16)
    r1 = x - x1.astype(F32)
    x2 = r1.astype(BF16)
    x3 = (r1 - x2.astype(F32)).astype(BF16)
    return x1, x2, x3


def _sum01(sel, x):
    x1, x2, x3 = _split3(x)
    d = functools.partial(jnp.dot, preferred_element_type=F32)
    return d(sel, x1) + d(sel, x2) + d(sel, x3)


def _sum01_nt(x, sel):
    x1, x2, x3 = _split3(x)
    d = lambda a: lax.dot_general(a, sel, (((1,), (1,)), ((), ())), preferred_element_type=F32)
    return d(x1) + d(x2) + d(x3)


def _expand2(x, e):
    x1 = x.astype(BF16)
    x2 = (x - x1.astype(F32)).astype(BF16)
    return jnp.dot(x1, e, preferred_element_type=F32) + jnp.dot(x2, e, preferred_element_type=F32)


class _Cfg:
    def __init__(self, n_ctx, l_ctx, n_smp, l_smp, d_model, d_ff):
        self.n_ctx, self.l_ctx, self.n_smp, self.l_smp = n_ctx, l_ctx, n_smp, l_smp
        self.d, self.d_ff = d_model, d_ff
        self.t_ctx = n_ctx * l_ctx
        self.t_smp = n_smp * l_smp
        self.t = self.t_ctx + self.t_smp
        self.tb = SCAN_BLOCK
        self.c = SCAN_CHUNK
        assert l_ctx % self.tb == 0 and l_smp % self.tb == 0 and self.tb % self.c == 0
        assert l_ctx & (l_ctx - 1) == 0 and l_smp % GRID_W == 0
        self.nbc = l_ctx // self.tb
        self.nbs = l_smp // self.tb
        self.gc0 = self.t_ctx // self.tb
        self.nb = self.t // self.tb
        self.tm = 1024 if (self.t_ctx % 1024 == 0 and self.t_smp % 1024 == 0 and l_smp % 1024 == 0) else 256
        assert self.t_ctx % self.tm == 0 and l_smp % self.tm == 0 and self.tm % l_ctx == 0
        self.tm_merge = 512 if self.tm == 1024 else 256
        self.tm_ffn = 1024 if self.tm == 1024 else 256

    def mod_row(self, tm):
        nct = self.t_ctx // tm
        per = self.l_smp // tm

        def f(m):
            return jnp.where(m < nct, self.n_smp, jnp.maximum(m - nct, 0) // per)
        return f

    def blk(self, d, j):
        return j + d * (self.nb - 1 - 2 * j)

    def scan_flags(self, d, j):
        g = self.blk(d, j)
        is_ctx = g < self.gc0
        gs = jnp.maximum(g - self.gc0, 0)
        pos = jnp.where(is_ctx, lax.rem(g, self.nbc), lax.rem(gs, self.nbs))
        nblk = jnp.where(is_ctx, self.nbc, self.nbs)
        spos = pos + d * (nblk - 1 - 2 * pos)
        return is_ctx, spos == 0, spos == nblk - 1

    def smp_seq(self, d, j):
        return jnp.clip((self.blk(d, j) - self.gc0) // self.nbs, 0, self.n_smp - 1)

    def ctx_seq(self, d, j):
        return jnp.minimum(self.blk(d, j) // self.nbc, self.n_ctx - 1)


def _tri_consts(c):
    t = np.arange(c)
    fwd = (t[None, :] <= t[:, None]).astype(np.float32)
    return jnp.asarray(np.stack([fwd, fwd.T]), dtype=BF16)


def _expand_consts():
    e = np.zeros((2, LANES, S_W), np.float32)
    for dd in range(2):
        for h in range(S_HEADS):
            e[dd, GC_DTA + dd * S_HEADS + h, h * S_P:(h + 1) * S_P] = 1.0
    return jnp.asarray(e, dtype=BF16)


def _embed_kernel(xp_ref, xs_ref, pos_ref, o_ref, *, nct):
    m = pl.program_id(0)

    @pl.when(m < nct)
    def _():
        o_ref[...] = xp_ref[...]

    @pl.when(m >= nct)
    def _():
        o_ref[...] = xs_ref[...] + pos_ref[...]


def _embed(cfg, xp2, xs2, pos):
    tm = cfg.tm
    nct = cfg.t_ctx // tm
    per = cfg.l_smp // tm
    return pl.pallas_call(
        functools.partial(_embed_kernel, nct=nct),
        grid=(cfg.t // tm,),
        in_specs=[pl.BlockSpec((tm, cfg.d), lambda m: (jnp.minimum(m, nct - 1), 0)),
                  pl.BlockSpec((tm, cfg.d), lambda m: (jnp.maximum(m - nct, 0), 0)),
                  pl.BlockSpec((tm, cfg.d), lambda m: (lax.rem(jnp.maximum(m - nct, 0), per), 0))],
        out_specs=pl.BlockSpec((tm, cfg.d), lambda m: (m, 0)),
        out_shape=jax.ShapeDtypeStruct((cfg.t, cfg.d), F32),
        compiler_params=_cparams(("arbitrary",)),
        name="embed",
    )(xp2, xs2, pos)


def _ada_kernel(c_ref, w_ref, b_ref, o_ref):
    o_ref[0] = _dot(_silu(c_ref[...]), w_ref[0]) + b_ref[0]


def _ada(cvec, w_ada, b_ada):
    depth, d, n = w_ada.shape
    rows = cvec.shape[0]
    tn = 1536
    return pl.pallas_call(
        _ada_kernel,
        grid=(depth, n // tn),
        in_specs=[pl.BlockSpec((rows, d), lambda l, j: (0, 0)),
                  pl.BlockSpec((1, d, tn), lambda l, j: (l, 0, j)),
                  pl.BlockSpec((1, 1, tn), lambda l, j: (l, 0, j))],
        out_specs=pl.BlockSpec((1, rows, tn), lambda l, j: (l, 0, j)),
        out_shape=jax.ShapeDtypeStruct((depth, rows, n), F32),
        compiler_params=_cparams(("arbitrary", "arbitrary")),
        name="ada",
    )(cvec, w_ada, b_ada.reshape(depth, 1, n))


def _inproj_kernel(x_ref, sc_ref, sh_ref, nw_ref, w_ref, ws_ref, gb_ref, ga_ref, cw_ref, cb_ref,
                   p_ref, pf_ref, g_ref, gt_ref, bmt_ref, kt_ref, vt_ref, xn_ref, *, tm, tn, nct, seg_ctx):
    m = pl.program_id(0)
    n = pl.program_id(1)
    gn = S_GROUPS * S_N
    nt = N_MAIN // tn

    def prologue():
        x = x_ref[...]
        y = x * lax.rsqrt(jnp.mean(x * x, axis=-1, keepdims=True) + EPS) * nw_ref[...]
        h = (y * (1.0 + sc_ref[0]) + sh_ref[0]).astype(BF16)
        xn_ref[...] = h
        raw = jnp.dot(h, ws_ref[...], preferred_element_type=F32) + gb_ref[...]
        lane = lax.broadcasted_iota(jnp.int32, raw.shape, 1)
        sp = _softplus(raw)
        g = jnp.where(lane < GC_LF, raw,
                      jnp.where(lane < GC_DT, -_softplus(-raw),
                                jnp.where(lane < GC_DTA, sp, sp * ga_ref[...])))
        g_ref[...] = g
        gt_ref[...] = g.T

    def matmul_cols(c0, c1):
        return jnp.dot(xn_ref[...], w_ref[:, c0:c1], preferred_element_type=F32)

    def conv_tile(tile, ncols):
        row = lax.broadcasted_iota(jnp.int32, (tm, LANES), 0)
        seg_mask = jnp.where(m < nct, seg_ctx - 1, GRID_W - 1)
        inseg = row & seg_mask
        has_prev = inseg != 0
        has_next = inseg != seg_mask
        cw = cw_ref[0]
        if ncols < tn:
            p_ref[:, ncols:] = matmul_cols(ncols, tn).astype(p_ref.dtype)
        accs = [matmul_cols(c0, c0 + MXU_N) for c0 in range(0, ncols, MXU_N)]
        for c0 in range(0, ncols, MXU_N):
            acc = accs[c0 // MXU_N]
            for cb in range(MXU_N // LANES):
                lo = c0 + cb * LANES
                sl = slice(lo, lo + LANES)
                u = acc[:, cb * LANES:(cb + 1) * LANES]
                up = jnp.where(has_prev, pltpu.roll(u, 1, 0), 0.0)
                un = jnp.where(has_next, pltpu.roll(u, tm - 1, 0), 0.0)
                v = _silu(cw[0:1, sl] * up + cw[1:2, sl] * u + cw[2:3, sl] * un + cb_ref[0][:, sl])
                p_ref[:, sl] = v.astype(p_ref.dtype)
                col = tile * tn + lo - S_W
                if 0 <= col < gn:
                    bmt_ref[col:col + LANES, :] = v.T.astype(BF16)

    f32_groups = {}
    for k, col in enumerate((OFF_H, OFF_H + 3 * H_W)):
        tile, off = divmod(col, tn)
        assert off + 2 * H_W <= tn
        f32_groups[tile] = (k, off)

    def plain_tile(tile, transposed):
        acc = matmul_cols(0, tn)
        p_ref[...] = acc.astype(p_ref.dtype)
        if tile in f32_groups:
            k, off = f32_groups[tile]
            pf_ref[:, k * 2 * H_W:(k + 1) * 2 * H_W] = acc[:, off:off + 2 * H_W]
        for dst_ref, off in transposed:
            for cb in range(M_W // LANES):
                blk = acc[:, off + cb * LANES:off + (cb + 1) * LANES]
                dst_ref[cb * LANES:(cb + 1) * LANES, :] = blk.T.astype(BF16)

    n_conv = -(-CONV_CH // tn)
    special = {}
    for dst_ref, col in ((kt_ref, OFF_M + M_W), (vt_ref, OFF_H + H_W)):
        tile, off = divmod(col, tn)
        assert off + M_W <= tn and tile >= n_conv
        special.setdefault(tile, []).append((dst_ref, off))

    for tile in range(nt):
        @pl.when(n == tile)
        def _(tile=tile):
            if tile == 0:
                prologue()
            if tile < n_conv:
                conv_tile(tile, min(tn, CONV_CH - tile * tn))
            else:
                plain_tile(tile, special.get(tile, []))


def _inproj(cfg, x, mod, nw, w_main, w_small, gbias, gmul, cw, cb):
    tm, tn = cfg.tm, PROJ_TN
    nct = cfg.t_ctx // tm
    row = cfg.mod_row(tm)
    rows_pad = mod.shape[0] // 6
    nt = N_MAIN // tn
    kern = functools.partial(_inproj_kernel, tm=tm, tn=tn, nct=nct, seg_ctx=cfg.l_ctx)
    tspec = pl.BlockSpec((M_W, tm), lambda m, n: (0, m))
    tshape = jax.ShapeDtypeStruct((M_W, cfg.t), BF16)
    return pl.pallas_call(
        kern,
        grid=(cfg.t // tm, nt),
        in_specs=[pl.BlockSpec((tm, cfg.d), lambda m, n: (m, 0)),
                  pl.BlockSpec((1, 1, cfg.d), lambda m, n: (1 * rows_pad + row(m), 0, 0)),
                  pl.BlockSpec((1, 1, cfg.d), lambda m, n: (0 * rows_pad + row(m), 0, 0)),
                  pl.BlockSpec((1, cfg.d), lambda m, n: (0, 0)),
                  pl.BlockSpec((cfg.d, tn), lambda m, n: (0, n)),
                  pl.BlockSpec((cfg.d, LANES), lambda m, n: (0, 0)),
                  pl.BlockSpec((1, LANES), lambda m, n: (0, 0)),
                  pl.BlockSpec((1, LANES), lambda m, n: (0, 0)),
                  pl.BlockSpec((1, 3, tn), lambda m, n: (jnp.minimum(n, 1), 0, 0)),
                  pl.BlockSpec((1, 1, tn), lambda m, n: (jnp.minimum(n, 1), 0, 0))],
        out_specs=[pl.BlockSpec((tm, tn), lambda m, n: (m, n)),
                   pl.BlockSpec((tm, 4 * H_W), lambda m, n: (m, 0)),
                   pl.BlockSpec((tm, LANES), lambda m, n: (m, 0)),
                   pl.BlockSpec((LANES, tm), lambda m, n: (0, m)),
                   tspec, tspec, tspec],
        out_shape=[jax.ShapeDtypeStruct((cfg.t, N_MAIN), BF16),
                   jax.ShapeDtypeStruct((cfg.t, 4 * H_W), F32),
                   jax.ShapeDtypeStruct((cfg.t, LANES), F32),
                   jax.ShapeDtypeStruct((LANES, cfg.t), F32),
                   tshape, tshape, tshape],
        scratch_shapes=[pltpu.VMEM((tm, cfg.d), BF16)],
        compiler_params=_cparams(("arbitrary", "arbitrary")),
        name="inproj",
    )(x, mod, mod, nw, w_main, w_small, gbias, gmul, cw, cb)


def _bidir_kernel(*refs, cfg, n_dir, n_shared, n_state, n_scratch, run, load_state, store_state):
    it = iter(refs)
    take = lambda k: [next(it) for _ in range(k)]
    din = [take(n_dir), take(n_dir)]
    shared = take(n_shared)
    sin = [take(n_state), take(n_state)]
    yout = [take(1)[0], take(1)[0]]
    sout = [take(n_state), take(n_state)]
    scr = [take(n_scratch), take(n_scratch)]
    j = pl.program_id(0)
    flags = [cfg.scan_flags(d, j) for d in (0, 1)]

    for d in (0, 1):
        is_ctx, first, _ = flags[d]

        @pl.when(jnp.logical_and(first, is_ctx))
        def _(d=d):
            for s in scr[d]:
                s[...] = jnp.zeros_like(s)

        @pl.when(jnp.logical_and(first, jnp.logical_not(is_ctx)))
        def _(d=d):
            load_state(sin[d], scr[d])

    run(shared, [(bool(d), din[d], yout[d], scr[d]) for d in (0, 1)])

    for d in (0, 1):
        is_ctx, _, last = flags[d]

        @pl.when(jnp.logical_and(last, is_ctx))
        def _(d=d):
            store_state(scr[d], sout[d])


def _bidir_call(cfg, name, y_width, dir_inputs, shared_inputs, state_inputs, state_shapes, scratch_shapes,
                run, load_state, store_state):
    tb = cfg.tb
    in_specs, args = [], []
    for d in (0, 1):
        for arr, bshape, ifn in dir_inputs:
            in_specs.append(pl.BlockSpec(bshape, lambda j, d=d, ifn=ifn: ifn(d, cfg.blk(d, j))))
            args.append(arr)
    for arr in shared_inputs:
        in_specs.append(pl.BlockSpec(arr.shape, lambda j, nd=arr.ndim: (0,) * nd))
        args.append(arr)
    for d in (0, 1):
        for arr, shp in zip(state_inputs, state_shapes):
            in_specs.append(pl.BlockSpec((None, None) + shp,
                                         lambda j, d=d, k=len(shp): (cfg.smp_seq(d, j), d) + (0,) * k))
            args.append(arr)
    out_specs, out_shape = [], []
    for d in (0, 1):
        out_specs.append(pl.BlockSpec((tb, y_width), lambda j, d=d: (cfg.blk(d, j), 0)))
        out_shape.append(jax.ShapeDtypeStruct((cfg.t, y_width), BF16))
    for d in (0, 1):
        for shp in state_shapes:
            out_specs.append(pl.BlockSpec((None,) + shp, lambda j, d=d, k=len(shp): (cfg.ctx_seq(d, j),) + (0,) * k))
            out_shape.append(jax.ShapeDtypeStruct((cfg.n_ctx,) + shp, F32))
    kern = functools.partial(_bidir_kernel, cfg=cfg, n_dir=len(dir_inputs), n_shared=len(shared_inputs),
                             n_state=len(state_inputs), n_scratch=len(scratch_shapes), run=run,
                             load_state=load_state, store_state=store_state)
    outs = pl.pallas_call(
        kern,
        grid=(cfg.nb,),
        in_specs=in_specs,
        out_specs=out_specs,
        out_shape=out_shape,
        scratch_shapes=[pltpu.VMEM(s, F32) for s in scratch_shapes] * 2,
        compiler_params=_cparams(("arbitrary",)),
        name=name,
    )(*args)
    ns = len(state_shapes)
    finals = [(outs[2 + k], outs[2 + ns + k]) for k in range(ns)]
    return outs[0], outs[1], finals


def _chunk_rows(cfg, rev):
    nch = cfg.tb // cfg.c
    return [(nch - 1 - ci if rev else ci) * cfg.c for ci in range(nch)]


def _scan_masks(c, rev):
    tpos = lax.broadcasted_iota(jnp.int32, (c, c), 0)
    spos = lax.broadcasted_iota(jnp.int32, (c, c), 1)
    return tpos, spos, ((spos >= tpos) if rev else (spos <= tpos))


def _mlstm_run(cfg, shared, sides):
    (tri_ref,) = shared
    c = cfg.c
    unit = (lax.broadcasted_iota(jnp.int32, (c, LANES), 1) == 0).astype(F32)
    nch = cfg.tb // c
    heads = range(M_HEADS)

    row = lax.broadcasted_iota(jnp.int32, (c, LANES), 0)
    pre = {}
    for si, (rev, din, _, _) in enumerate(sides):
        g_ref, gt_ref = din[4], din[5]
        sel = tri_ref[int(rev)]
        for ci, r0 in enumerate(_chunk_rows(cfg, rev)):
            gc = g_ref[r0:r0 + c, :]
            gtc = gt_ref[:, r0:r0 + c]
            bcol = _sum01(sel, gc)
            brow = _sum01_nt(gtc[GC_LF:GC_LF + 8], sel)
            ba = pltpu.roll(bcol, LANES - (GC_LF - GC_IG), 1)
            cm = gc - ba
            k = 1
            while k < c:
                if rev:
                    cm = jnp.maximum(cm, jnp.where(row < c - k, pltpu.roll(cm, c - k, 0), NEG_BIG))
                else:
                    cm = jnp.maximum(cm, jnp.where(row >= k, pltpu.roll(cm, k, 0), NEG_BIG))
                k *= 2
            pre[si, ci] = dict(r0=r0, ba=ba, cm=cm,
                               urow=gtc[GC_IG:GC_IG + 8] - brow)
    probs = [(si, ci, h) for si in range(len(sides)) for ci in range(nch) for h in heads]
    st = {}
    for si, ci, h in probs:
        q_ref, k_ref = sides[si][1][0], sides[si][1][1]
        r0 = pre[si, ci]["r0"]
        hs = slice(h * M_DH, (h + 1) * M_DH)
        q = q_ref[r0:r0 + c, hs]
        st[si, ci, h] = dict(q=q, qk=_dot_nt(q, k_ref[r0:r0 + c, hs]))
    masks = [_scan_masks(c, rev)[2] for rev, _, _, _ in sides]
    lane1 = lax.broadcasted_iota(jnp.int32, (1, LANES), 1)
    log_kscale = float(np.log(M_DH ** -0.5))

    for ci in range(nch):
        step = [(si, h) for si in range(len(sides)) for h in heads]
        col_q = {}
        for si, (rev, _, _, (_, m_sc)) in enumerate(sides):
            p = pre[si, ci]
            end = 0 if rev else c - 1
            mrow = jnp.zeros((1, LANES), F32)
            for h in heads:
                mrow = jnp.where(lane1 == int(rev) * M_HEADS + h, m_sc[h], mrow)
            z = jnp.maximum(p["cm"], mrow)
            mt = p["ba"] + z
            m_end = mt[end:end + 1, :]
            b_end = p["ba"][end:end + 1, :]
            col_q[si] = dict(z=z, wi=jnp.exp(mrow - z), emt=jnp.exp(-mt), m_end=m_end,
                             bm=b_end - m_end, carry=jnp.exp(b_end + mrow - m_end))
        for si, h in step:
            e, cq = st[si, ci, h], col_q[si]
            col = int(sides[si][0]) * M_HEADS + h
            e["u_r"] = pre[si, ci]["urow"][col:col + 1, :]
            s = e["qk"] * jnp.exp(jnp.where(masks[si], (e["u_r"] + log_kscale) - cq["z"][:, col:col + 1], NEG_BIG))
            qi = e["q"].astype(F32) * cq["wi"][:, col:col + 1]
            e["lhs"] = jnp.concatenate([s, qi], axis=1).astype(BF16)
        for si, h in step:
            e = st[si, ci, h]
            v_ref, c_sc = sides[si][1][2], sides[si][3][0]
            r0 = pre[si, ci]["r0"]
            hs = slice(h * M_DH, (h + 1) * M_DH)
            e["vaug"] = jnp.concatenate([v_ref[r0:r0 + c, hs].astype(BF16), unit.astype(BF16)], axis=1)
            e["caug"] = c_sc[h]
            rhs = jnp.concatenate([e["vaug"], e["caug"].astype(BF16)], axis=0)
            e["numden"] = jnp.dot(e["lhs"], rhs, preferred_element_type=F32)
        for si, h in step:
            e = st[si, ci, h]
            rev, din, y_ref, _ = sides[si]
            r0 = pre[si, ci]["r0"]
            hs = slice(h * M_DH, (h + 1) * M_DH)
            cq = col_q[si]
            col = int(rev) * M_HEADS + h
            den = e["numden"][:, M_DH:M_DH + 1]
            hc = e["numden"][:, :M_DH] / jnp.maximum(jnp.abs(den), cq["emt"][:, col:col + 1])
            y_ref[r0:r0 + c, hs] = hc.astype(y_ref.dtype)
            wk = jnp.exp(e["u_r"] + cq["bm"][:, col:col + 1]) * (M_DH ** -0.5)
            e["ktw"] = (din[3][hs, r0:r0 + c].astype(F32) * wk).astype(BF16)
        for si, h in step:
            e, cq = st[si, ci, h], col_q[si]
            c_sc, m_sc = sides[si][3]
            col = int(sides[si][0]) * M_HEADS + h
            c_sc[h] = (cq["carry"][:, col:col + 1] * e["caug"]
                       + jnp.dot(e["ktw"], e["vaug"], preferred_element_type=F32))
            m_sc[h] = jnp.broadcast_to(cq["m_end"][:, col:col + 1], (1, LANES))


def _copy_state(src, dst):
    for s, d in zip(src, dst):
        d[...] = s[...]


def _mlstm(cfg, p, kt, g, gt, tri, c0, m0):
    tb = cfg.tb
    qi = OFF_M // M_W
    dir_inputs = [(p, (tb, M_W), lambda d, b: (b, qi)),
                  (p, (tb, M_W), lambda d, b: (b, qi + 1)),
                  (p, (tb, M_W), lambda d, b: (b, qi + 2)),
                  (kt, (M_W, tb), lambda d, b: (0, b)),
                  (g, (tb, LANES), lambda d, b: (b, 0)),
                  (gt, (LANES, tb), lambda d, b: (0, b))]
    shapes = [(M_HEADS, M_DH, 2 * M_DH), (M_HEADS, 1, LANES)]
    return _bidir_call(cfg, "mlstm", M_W, dir_inputs, [tri], [c0, m0], shapes, shapes,
                       functools.partial(_mlstm_run, cfg), _copy_state, _copy_state)


def _mid_rows(g, lev, rev):
    c, w = g.shape
    half = 1 << lev
    blk = 2 * half
    m = half if rev else half - 1
    if blk >= 8:
        g3 = g.reshape(c // blk, blk, w)
        return jnp.broadcast_to(g3[:, m:m + 1, :], g3.shape).reshape(c, w)
    g3 = g.reshape(c // 8, 8, w)
    sub = lax.broadcasted_iota(jnp.int32, g3.shape, 1)
    out = jnp.broadcast_to(g3[:, m:m + 1, :], g3.shape)
    for b0 in range(blk, 8, blk):
        out = jnp.where(sub >= b0, jnp.broadcast_to(g3[:, b0 + m:b0 + m + 1, :], g3.shape), out)
    return out.reshape(c, w)


def _hgrn_run(cfg, shared, sides):
    (tri_ref,) = shared
    c = cfg.c
    levels = int(np.log2(c))
    ones = jnp.ones((H_DK, LANES), BF16)
    nch = cfg.tb // c
    heads = range(H_HEADS)
    dn = (((1,), (1,)), ((), ()))

    pre, pair = {}, {}
    for si, (rev, din, _, _) in enumerate(sides):
        f_ref, lb_ref = din[3], din[4]
        sel = tri_ref[int(rev)]
        lb = lb_ref[0]
        tpos, spos, _ = _scan_masks(c, rev)
        before = (spos > tpos) if rev else (spos < tpos)
        lvl = jnp.where(before, tpos ^ spos, 0)
        pair[si] = [(lvl >> lev) == 1 for lev in range(levels)]
        for ci, r0 in enumerate(_chunk_rows(cfg, rev)):
            f = lb + (1.0 - lb) * _sigmoid(f_ref[r0:r0 + c, :])
            pre[si, ci] = dict(r0=r0, kk=1.0 - f, g=_sum01(sel, jnp.log(f)))
    probs = [(si, ci, h) for si in range(len(sides)) for ci in range(nch) for h in heads]
    st = {}
    for si, ci, h in probs:
        p = pre[si, ci]
        hs = slice(h * H_DK, (h + 1) * H_DK)
        q = sides[si][1][0][p["r0"]:p["r0"] + c, hs]
        kk = p["kk"][:, hs]
        st[si, ci, h] = dict(q=q, kk=kk, qb=q.astype(BF16), kb=kk.astype(BF16), g=p["g"][:, hs],
                             att=jnp.zeros((c, c), F32))
    for lev in range(levels):
        for si, ci, h in probs:
            e = st[si, ci, h]
            x = jnp.exp(-jnp.abs(e["g"] - _mid_rows(e["g"], lev, sides[si][0]))).astype(BF16)
            a = lax.dot_general(e["qb"] * x, e["kb"] * x, dn, preferred_element_type=F32)
            e["att"] = jnp.where(pair[si][lev], a, e["att"])
    for si, ci, h in probs:
        e = st[si, ci, h]
        e["diag"] = jnp.dot((e["q"] * e["kk"]).astype(BF16), ones, preferred_element_type=F32)

    for ci in range(nch):
        step = [(si, h) for si in range(len(sides)) for h in heads]
        for si, h in step:
            e = st[si, ci, h]
            rev, din, y_ref, (s_sc,) = sides[si]
            r0 = pre[si, ci]["r0"]
            hs = slice(h * H_DK, (h + 1) * H_DK)
            v = din[1][r0:r0 + c, hs]
            e["st"] = s_sc[h]
            o = _dot(e["att"], v) + e["diag"] * v + _dot_nt(e["q"] * jnp.exp(e["g"]), e["st"])
            y_ref[r0:r0 + c, hs] = o.astype(y_ref.dtype)
        for si, h in step:
            e = st[si, ci, h]
            rev, din, _, (s_sc,) = sides[si]
            r0 = pre[si, ci]["r0"]
            hs = slice(h * H_DK, (h + 1) * H_DK)
            end = 0 if rev else c - 1
            g_end = e["g"][end:end + 1, :]
            vt = din[2][hs, r0:r0 + c]
            s_sc[h] = jnp.exp(g_end) * e["st"] + _dot(vt, e["kk"] * jnp.exp(g_end - e["g"]))


def _hgrn_load(sin, scr):
    for h in range(H_HEADS):
        scr[0][h] = sin[0][h].T


def _hgrn_store(scr, sout):
    for h in range(H_HEADS):
        sout[0][h] = scr[0][h].T


def _hgrn(cfg, pf, vt, lb, tri, s0):
    tb = cfg.tb
    dir_inputs = [(pf, (tb, H_W), lambda d, b: (b, 0)),
                  (pf, (tb, H_W), lambda d, b: (b, 1)),
                  (vt, (H_W, tb), lambda d, b: (0, b)),
                  (pf, (tb, H_W), lambda d, b: (b, 2 + d)),
                  (lb, (1, 1, H_W), lambda d, b: (d, 0, 0))]
    return _bidir_call(cfg, "hgrn", H_W, dir_inputs, [tri], [s0], [(H_HEADS, H_DK, H_DV)],
                       [(H_HEADS, H_DV, H_DK)], functools.partial(_hgrn_run, cfg), _hgrn_load, _hgrn_store)


def _ssd_run(cfg, shared, sides):
    tri_ref, ex_ref, sd_ref = shared
    c = cfg.c
    n_rep = S_HEADS // S_GROUPS
    gw = n_rep * S_P
    nch = cfg.tb // c
    lane = lax.broadcasted_iota(jnp.int32, (c, LANES), 1)
    head_of_lane = lax.broadcasted_iota(jnp.int32, (c, gw), 1) // S_P
    in_head = [head_of_lane == r for r in range(n_rep)]

    rows = {(si, ci): r0 for si, (rev, _, _, _) in enumerate(sides)
            for ci, r0 in enumerate(_chunk_rows(cfg, rev))}
    probs = [(si, ci, grp) for si in range(len(sides)) for ci in range(nch) for grp in range(S_GROUPS)]
    st = {}
    for si, ci, grp in probs:
        cm_ref, bmt_ref = sides[si][1][1], sides[si][1][2]
        r0 = rows[si, ci]
        cmg = cm_ref[r0:r0 + c, grp * S_N:(grp + 1) * S_N].astype(BF16)
        bmt = bmt_ref[grp * S_N:(grp + 1) * S_N, r0:r0 + c]
        st[si, ci, grp] = dict(cmg=cmg, bmt=bmt, cb=jnp.dot(cmg, bmt, preferred_element_type=F32))
    pre, mask = {}, {}
    for si, (rev, din, _, _) in enumerate(sides):
        x_ref, g_ref, gt_ref = din[0], din[3], din[4]
        dd = int(rev)
        sel = tri_ref[dd]
        mask[si] = _scan_masks(c, rev)[2]
        la0 = GC_DTA + dd * S_HEADS
        is_la = (lane - la0).astype(jnp.uint32) < S_HEADS
        for ci in range(nch):
            r0 = rows[si, ci]
            gc = g_ref[r0:r0 + c, :]
            gtc = gt_ref[:, r0:r0 + c]
            pre[si, ci] = dict(r0=r0, la0=la0, gc=gc, xall=x_ref[r0:r0 + c, :].astype(F32),
                               lam=jnp.where(is_la, _sum01(sel, gc), 0.0),
                               larow=_sum01_nt(gtc[la0:la0 + S_HEADS], sel),
                               dtrow=gtc[GC_DT + dd * S_HEADS:GC_DT + (dd + 1) * S_HEADS])

    def expansions(si, ci):
        rev = sides[si][0]
        p = pre[si, ci]
        ex = ex_ref[int(rev)]
        end = 0 if rev else c - 1
        lam = p["lam"]
        ela = jnp.exp(lam)
        p["ea"] = jnp.dot(ela.astype(BF16), ex, preferred_element_type=F32)
        e8 = 0 if rev else c - 8
        p["ea_end"] = _expand2(ela[e8:e8 + 8, :], ex)[end - e8:end - e8 + 1, :]
        wk = jnp.exp(lam[end:end + 1, :] - lam) * pltpu.roll(p["gc"], GC_DTA - GC_DT, 1)
        ewk = jnp.dot(wk.astype(BF16), ex, preferred_element_type=F32)
        p["xw"] = (p["xall"] * ewk).astype(BF16)
    for si, ci, grp in probs:
        e, p = st[si, ci, grp], pre[si, ci]
        xg = p["xall"][:, grp * gw:(grp + 1) * gw]
        lhs, rhs = [], []
        for r in range(n_rep):
            h = grp * n_rep + r
            la_c = p["lam"][:, p["la0"] + h:p["la0"] + h + 1]
            seg = jnp.exp(jnp.where(mask[si], la_c - p["larow"][h:h + 1, :], NEG_BIG)) * p["dtrow"][h:h + 1, :]
            lhs.append((seg * e["cb"]).astype(BF16))
            rhs.append(jnp.where(in_head[r], xg, 0.0).astype(BF16))
        y = jnp.dot(jnp.concatenate(lhs, axis=1), jnp.concatenate(rhs, axis=0), preferred_element_type=F32)
        if not sides[si][0]:
            y = y + sd_ref[:, grp * gw:(grp + 1) * gw] * xg
        e["y"] = y
    for si, ci in rows:
        expansions(si, ci)

    for ci in range(nch):
        step = [(si, grp) for si in range(len(sides)) for grp in range(S_GROUPS)]
        for si, grp in step:
            e, p = st[si, ci, grp], pre[si, ci]
            _, _, y_ref, (h_sc,) = sides[si]
            gs = slice(grp * gw, (grp + 1) * gw)
            e["ht"] = h_sc[grp]
            y = e["y"] + p["ea"][:, gs] * jnp.dot(e["cmg"], e["ht"].astype(BF16), preferred_element_type=F32)
            y_ref[p["r0"]:p["r0"] + c, gs] = y.astype(y_ref.dtype)
        for si, grp in step:
            e, p = st[si, ci, grp], pre[si, ci]
            (h_sc,) = sides[si][3]
            gs = slice(grp * gw, (grp + 1) * gw)
            h_sc[grp] = p["ea_end"][:, gs] * e["ht"] + jnp.dot(e["bmt"], p["xw"][:, gs],
                                                               preferred_element_type=F32)


def _ssd_load(sin, scr):
    n_rep = S_HEADS // S_GROUPS
    for grp in range(S_GROUPS):
        scr[0][grp] = sin[0][grp * n_rep:(grp + 1) * n_rep].reshape(n_rep * S_P, S_N).T


def _ssd_store(scr, sout):
    n_rep = S_HEADS // S_GROUPS
    for grp in range(S_GROUPS):
        sout[0][grp * n_rep:(grp + 1) * n_rep] = scr[0][grp].T.reshape(n_rep, S_P, S_N)


def _ssd(cfg, p, bmt, g, gt, tri, ex, sd, h0):
    tb = cfg.tb
    gn = S_GROUPS * S_N
    dir_inputs = [(p, (tb, S_W), lambda d, b: (b, OFF_XBC // S_W)),
                  (p, (tb, gn), lambda d, b: (b, (OFF_XBC + S_W) // gn + 1)),
                  (bmt, (gn, tb), lambda d, b: (0, b)),
                  (g, (tb, LANES), lambda d, b: (b, 0)),
                  (gt, (LANES, tb), lambda d, b: (0, b))]
    return _bidir_call(cfg, "ssd", S_W, dir_inputs, [tri, ex, sd], [h0], [(S_HEADS, S_P, S_N)],
                       [(S_GROUPS, S_N, (S_HEADS // S_GROUPS) * S_P)], functools.partial(_ssd_run, cfg),
                       _ssd_load, _ssd_store)


def _head_norm(y, heads, width):
    outs = []
    for h in range(heads):
        s = y[:, h * width:(h + 1) * width]
        outs.append(s * lax.rsqrt(jnp.mean(s * s, axis=-1, keepdims=True) + EPS))
    return jnp.concatenate(outs, axis=1)


def _merge_kernel(x_ref, gate_ref, mo_ref, hg_ref, sz_ref, bg_ref, ymf_ref, ymb_ref, yhf_ref, yhb_ref,
                  ysf_ref, ysb_ref, mn_ref, hn_ref, sn_ref, wbm_ref, wbh_ref, wbs_ref, wo_ref, o_ref,
                  *, d_model):
    add = lambda a, b: a[...].astype(F32) + b[...].astype(F32)
    f32 = lambda r: r[...].astype(F32)
    y_m = _head_norm(add(ymf_ref, ymb_ref), M_HEADS, M_DH) * mn_ref[...] * _sigmoid(f32(mo_ref))
    y_h = _head_norm(add(yhf_ref, yhb_ref), H_HEADS, H_DV) * hn_ref[...] * _silu(f32(hg_ref))
    ys = add(ysf_ref, ysb_ref) * _silu(f32(sz_ref))
    y_s = ys * lax.rsqrt(jnp.mean(ys * ys, axis=-1, keepdims=True) + EPS) * sn_ref[...]
    bg = f32(bg_ref)
    dm = d_model
    merged = (_sigmoid(bg[:, :dm]) * jnp.dot(y_m.astype(BF16), wbm_ref[...], preferred_element_type=F32)
              + _sigmoid(bg[:, dm:2 * dm]) * jnp.dot(y_h.astype(BF16), wbh_ref[...], preferred_element_type=F32)
              + _sigmoid(bg[:, 2 * dm:]) * jnp.dot(y_s.astype(BF16), wbs_ref[...], preferred_element_type=F32))
    out = jnp.dot(merged.astype(BF16), wo_ref[...], preferred_element_type=F32)
    o_ref[...] = x_ref[...] + gate_ref[0] * out


def _merge(cfg, x, mod, p, ym, yh, ys, mn, hn, sn, wbm, wbh, wbs, wo):
    tm = cfg.tm_merge
    row = cfg.mod_row(tm)
    rows_pad = mod.shape[0] // 6
    d = cfg.d
    const = lambda shape: pl.BlockSpec(shape, lambda m: (0,) * len(shape))
    tok = lambda w: pl.BlockSpec((tm, w), lambda m: (m, 0))
    return pl.pallas_call(
        functools.partial(_merge_kernel, d_model=d),
        grid=(cfg.t // tm,),
        in_specs=[tok(d),
                  pl.BlockSpec((1, 1, d), lambda m: (2 * rows_pad + row(m), 0, 0)),
                  pl.BlockSpec((tm, M_W), lambda m: (m, (OFF_M + 3 * M_W) // M_W)),
                  pl.BlockSpec((tm, H_W), lambda m: (m, (OFF_H + 2 * H_W) // H_W)),
                  pl.BlockSpec((tm, S_W), lambda m: (m, OFF_SZ // S_W)),
                  pl.BlockSpec((tm, 3 * d), lambda m: (m, OFF_BG // (3 * d))),
                  tok(M_W), tok(M_W), tok(H_W), tok(H_W), tok(S_W), tok(S_W),
                  const((1, M_W)), const((1, H_W)), const((1, S_W)),
                  const((M_W, d)), const((H_W, d)), const((S_W, d)), const((d, d))],
        out_specs=tok(d),
        out_shape=jax.ShapeDtypeStruct((cfg.t, d), F32),
        compiler_params=_cparams(("arbitrary",)),
        name="merge",
    )(x, mod, p, p, p, p, ym[0], ym[1], yh[0], yh[1], ys[0], ys[1], mn, hn, sn, wbm, wbh, wbs, wo)


def _ffn_up_kernel(x_ref, sc_ref, sh_ref, nw_ref, wa_ref, wb_ref, o_ref, *, d_ff, tc):
    x = x_ref[...]
    y = x * lax.rsqrt(jnp.mean(x * x, axis=-1, keepdims=True) + EPS) * nw_ref[...]
    h = (y * (1.0 + sc_ref[0]) + sh_ref[0]).astype(BF16)
    for j in range(d_ff // tc):
        sl = slice(j * tc, (j + 1) * tc)
        a = jnp.dot(h, wa_ref[:, sl], preferred_element_type=F32)
        b = jnp.dot(h, wb_ref[:, sl], preferred_element_type=F32)
        o_ref[:, sl] = (_silu(a) * b).astype(BF16)


def _ffn_up(cfg, x, mod, nw, w_gu):
    tm = cfg.tm_ffn
    row = cfg.mod_row(tm)
    rows_pad = mod.shape[0] // 6
    d, d_ff = cfg.d, cfg.d_ff
    return pl.pallas_call(
        functools.partial(_ffn_up_kernel, d_ff=d_ff, tc=MXU_N),
        grid=(cfg.t // tm,),
        in_specs=[pl.BlockSpec((tm, d), lambda m: (m, 0)),
                  pl.BlockSpec((1, 1, d), lambda m: (4 * rows_pad + row(m), 0, 0)),
                  pl.BlockSpec((1, 1, d), lambda m: (3 * rows_pad + row(m), 0, 0)),
                  pl.BlockSpec((1, d), lambda m: (0, 0)),
                  pl.BlockSpec((d, d_ff), lambda m: (0, 0)),
                  pl.BlockSpec((d, d_ff), lambda m: (0, 1))],
        out_specs=pl.BlockSpec((tm, d_ff), lambda m: (m, 0)),
        out_shape=jax.ShapeDtypeStruct((cfg.t, d_ff), BF16),
        compiler_params=_cparams(("arbitrary",)),
        name="ffn_up",
    )(x, mod, mod, nw, w_gu, w_gu)


def _ffn_down_kernel(x_ref, gate_ref, a_ref, w_ref, o_ref):
    o_ref[...] = x_ref[...] + gate_ref[0] * jnp.dot(a_ref[...], w_ref[...], preferred_element_type=F32)


def _ffn_down_final_kernel(x_ref, gate_ref, a_ref, w_ref, nf_ref, op_ref, os_ref, *, nct):
    m = pl.program_id(0)
    x = x_ref[...] + gate_ref[0] * jnp.dot(a_ref[...], w_ref[...], preferred_element_type=F32)
    y = x * lax.rsqrt(jnp.mean(x * x, axis=-1, keepdims=True) + EPS) * nf_ref[...]

    @pl.when(m < nct)
    def _():
        op_ref[...] = y

    @pl.when(m >= nct)
    def _():
        os_ref[...] = y


def _ffn_down(cfg, x, mod, act, w_down, norm_f=None):
    tm = cfg.tm_ffn
    row = cfg.mod_row(tm)
    rows_pad = mod.shape[0] // 6
    d, d_ff = cfg.d, cfg.d_ff
    in_specs = [pl.BlockSpec((tm, d), lambda m: (m, 0)),
                pl.BlockSpec((1, 1, d), lambda m: (5 * rows_pad + row(m), 0, 0)),
                pl.BlockSpec((tm, d_ff), lambda m: (m, 0)),
                pl.BlockSpec((d_ff, d), lambda m: (0, 0))]
    if norm_f is None:
        return pl.pallas_call(
            _ffn_down_kernel,
            grid=(cfg.t // tm,),
            in_specs=in_specs,
            out_specs=pl.BlockSpec((tm, d), lambda m: (m, 0)),
            out_shape=jax.ShapeDtypeStruct((cfg.t, d), F32),
            compiler_params=_cparams(("arbitrary",)),
            name="ffn_down",
        )(x, mod, act, w_down)
    nct = cfg.t_ctx // tm
    return pl.pallas_call(
        functools.partial(_ffn_down_final_kernel, nct=nct),
        grid=(cfg.t // tm,),
        in_specs=in_specs + [pl.BlockSpec((1, d), lambda m: (0, 0))],
        out_specs=[pl.BlockSpec((tm, d), lambda m: (jnp.minimum(m, nct - 1), 0)),
                   pl.BlockSpec((tm, d), lambda m: (jnp.maximum(m - nct, 0), 0))],
        out_shape=[jax.ShapeDtypeStruct((cfg.t_ctx, d), F32),
                   jax.ShapeDtypeStruct((cfg.t_smp, d), F32)],
        compiler_params=_cparams(("arbitrary",)),
        name="ffn_down_final",
    )(x, mod, act, w_down, norm_f)


def _grid_pos_embed(rows, d_model):
    quarter = d_model // 4
    freq = POS_BASE ** (-jnp.arange(quarter, dtype=F32) / quarter)
    r = jnp.arange(rows, dtype=F32)[:, None] * freq
    cl = jnp.arange(GRID_W, dtype=F32)[:, None] * freq
    row_e = jnp.concatenate([jnp.sin(r), jnp.cos(r)], axis=-1)
    col_e = jnp.concatenate([jnp.sin(cl), jnp.cos(cl)], axis=-1)
    emb = jnp.concatenate([jnp.broadcast_to(row_e[:, None], (rows, GRID_W, d_model // 2)),
                           jnp.broadcast_to(col_e[None], (rows, GRID_W, d_model // 2))], axis=-1)
    return emb.reshape(rows * GRID_W, d_model)


def kernel(x_prompt, x_sample, state_mlstm_c, state_mlstm_n, state_mlstm_m, state_hgrn, state_ssm, c, c_ctx, w_ada, b_ada, norm1, norm2, w_in, m_bi, m_bf, m_norm, h_lb, h_norm, s_conv_w, s_conv_b, s_dt_bias, s_a_log, s_d, s_norm, w_bm, w_bh, w_bs, w_out, w_gu, w_down, norm_f):
    n_ctx, l_ctx, d = x_prompt.shape
    n_smp, l_smp, _ = x_sample.shape
    depth = w_in.shape[0]
    d_ff = w_down.shape[1]
    cfg = _Cfg(n_ctx, l_ctx, n_smp, l_smp, d, d_ff)

    tri = _tri_consts(cfg.c)
    ex = _expand_consts()

    rows_pad = -(-(n_smp + 1) // 8) * 8
    cvec = jnp.zeros((rows_pad, d), F32).at[:n_smp].set(c).at[n_smp].set(c_ctx)
    mod_all = _ada(cvec, w_ada, b_ada)
    mod_all = mod_all.reshape(depth, rows_pad, 6, d).transpose(0, 2, 1, 3).reshape(depth, 6 * rows_pad, 1, d)

    p_lb = jax.nn.softmax(h_lb.astype(F32), axis=1)
    lower = jnp.cumsum(p_lb, axis=1) - p_lb[:, :1]

    pos = _grid_pos_embed(l_smp // GRID_W, d)
    x = _embed(cfg, x_prompt.reshape(cfg.t_ctx, d), x_sample.reshape(cfg.t_smp, d), pos)

    a_neg = -jnp.exp(s_a_log.astype(F32))
    new_c, new_n, new_m, new_h, new_s = [], [], [], [], []
    y_p = y_s = None
    for i in range(depth):
        w = w_in[i]
        o_mi, o_hq, o_sz, o_xbc, o_dt, o_bg = 2048, 2064, 4624, 5648, 7696, 7728
        w_main = jnp.concatenate([w[:, o_xbc:o_dt], w[:, o_sz:o_xbc], w[:, o_bg:], w[:, :o_mi],
                                  w[:, o_hq:o_sz]], axis=1).astype(BF16)
        w_small = jnp.concatenate([w[:, o_mi:o_hq], w[:, o_dt:o_bg], w[:, o_dt:o_bg],
                                   jnp.zeros((d, LANES - GC_DTA - 2 * S_HEADS), F32)], axis=1).astype(BF16)
        dtb = s_dt_bias[i].reshape(-1)
        gbias = jnp.concatenate([m_bi[i].reshape(-1), m_bf[i].reshape(-1), dtb, dtb,
                                 jnp.zeros((LANES - GC_DTA - 2 * S_HEADS,), F32)]).reshape(1, LANES)
        gmul = jnp.concatenate([jnp.ones((GC_DTA,), F32), a_neg[i].reshape(-1),
                                jnp.zeros((LANES - GC_DTA - 2 * S_HEADS,), F32)]).reshape(1, LANES)
        ncw = -(-CONV_CH // PROJ_TN) * PROJ_TN
        cw = jnp.zeros((3, ncw), F32).at[:, :CONV_CH].set(s_conv_w[i]).reshape(3, ncw // PROJ_TN, PROJ_TN)
        cw = cw.transpose(1, 0, 2)
        cb = jnp.zeros((ncw,), F32).at[:CONV_CH].set(s_conv_b[i]).reshape(ncw // PROJ_TN, 1, PROJ_TN)
        mod = mod_all[i]

        p, pf, g, gt, bmt, kt, vt = _inproj(cfg, x, mod, norm1[i].reshape(1, d), w_main, w_small, gbias, gmul,
                                            cw, cb)

        c0 = jnp.concatenate([state_mlstm_c[:, i], state_mlstm_n[:, i][..., None],
                              jnp.zeros(state_mlstm_c[:, i].shape[:-1] + (M_DH - 1,), F32)], axis=-1)
        m0 = jnp.broadcast_to(state_mlstm_m[:, i][..., None, None], (n_smp, 2, M_HEADS, 1, LANES))
        ymf, ymb, (c_fin, m_fin) = _mlstm(cfg, p, kt, g, gt, tri, c0, m0)
        yhf, yhb, (h_fin,) = _hgrn(cfg, pf, vt, lower[:, i].reshape(2, 1, H_W), tri, state_hgrn[:, i])
        sd = jnp.repeat(s_d[i].astype(F32), S_P).reshape(1, S_W)
        ysf, ysb, (s_fin,) = _ssd(cfg, p, bmt, g, gt, tri, ex, sd, state_ssm[:, i])

        x = _merge(cfg, x, mod, p, (ymf, ymb), (yhf, yhb), (ysf, ysb), m_norm[i].reshape(1, M_W),
                   h_norm[i].reshape(1, H_W), s_norm[i].reshape(1, S_W), w_bm[i].astype(BF16),
                   w_bh[i].astype(BF16), w_bs[i].astype(BF16), w_out[i].astype(BF16))
        act = _ffn_up(cfg, x, mod, norm2[i].reshape(1, d), w_gu[i].astype(BF16))
        if i + 1 < depth:
            x = _ffn_down(cfg, x, mod, act, w_down[i].astype(BF16))
        else:
            y_p, y_s = _ffn_down(cfg, x, mod, act, w_down[i].astype(BF16), norm_f.reshape(1, d))

        for cd, md, hd, sd_ in zip(c_fin, m_fin, h_fin, s_fin):
            new_c.append(cd[..., :M_DH])
            new_n.append(cd[..., M_DH])
            new_m.append(md[..., 0, 0])
            new_h.append(hd)
            new_s.append(sd_)

    def layers_dirs(parts):
        a = jnp.stack(parts, axis=1)
        return a.reshape((n_ctx, depth, 2) + a.shape[2:])

    return (y_p.reshape(n_ctx, l_ctx, d), y_s.reshape(n_smp, l_smp, d),
            layers_dirs(new_c), layers_dirs(new_n), layers_dirs(new_m), layers_dirs(new_h), layers_dirs(new_s))
```

```python
import functools

import numpy as np
import jax
import jax.numpy as jnp
from jax import lax
from jax.experimental import pallas as pl
from jax.experimental.pallas import tpu as pltpu

F32 = jnp.float32
BF16 = jnp.bfloat16

GRID_W = 64
EPS = 1e-6
POS_BASE = 10000.0
NEG_BIG = -1e30
M_HEADS, M_DH = 4, 128
H_HEADS, H_DK, H_DV = 4, 128, 128
S_HEADS, S_P, S_GROUPS, S_N = 16, 64, 4, 128
M_W = M_HEADS * M_DH
H_W = H_HEADS * H_DV
S_W = S_HEADS * S_P
CONV_CH = S_W + 2 * S_GROUPS * S_N

LANES = 128
MXU_N = 256
VMEM_LIMIT = 56 * 1024 * 1024
SCAN_CHUNK = 128
SCAN_BLOCK = 256
PROJ_TN = 1536

OFF_XBC, OFF_SZ, OFF_BG, OFF_M, OFF_H = 0, 2048, 3072, 6144, 8192
N_MAIN = 10752
GC_IG, GC_LF, GC_DT, GC_DTA = 0, 8, 16, 48


def _cparams(sem):
    return pltpu.CompilerParams(dimension_semantics=sem, vmem_limit_bytes=VMEM_LIMIT)


NEG_LOG2E = -1.4426950408889634


def _sigmoid(x):
    return 1.0 / (1.0 + jnp.exp2(x * NEG_LOG2E))


def _silu(x):
    return x * _sigmoid(x)


def _softplus(x):
    return jnp.maximum(x, 0.0) + jnp.log(1.0 + jnp.exp(-jnp.abs(x)))


def _dot(a, b):
    return jnp.dot(a.astype(BF16), b.astype(BF16), preferred_element_type=F32)


def _dot_nt(a, b):
    return lax.dot_general(a.astype(BF16), b.astype(BF16), (((1,), (1,)), ((), ())),
                           preferred_element_type=F32)


def _split3(x):
    x1 = x.astype(BF16)
    r1 = x - x1.astype(F32)
    x2 = r1.astype(BF16)
    x3 = (r1 - x2.astype(F32)).astype(BF16)
    return x1, x2, x3


def _sum01(sel, x):
    x1, x2, x3 = _split3(x)
    d = functools.partial(jnp.dot, preferred_element_type=F32)
    return d(sel, x1) + d(sel, x2) + d(sel, x3)


def _sum01_nt(x, sel):
    x1, x2, x3 = _split3(x)
    d = lambda a: lax.dot_general(a, sel, (((1,), (1,)), ((), ())), preferred_element_type=F32)
    return d(x1) + d(x2) + d(x3)


def _expand2(x, e):
    x1 = x.astype(BF16)
    x2 = (x - x1.astype(F32)).astype(BF16)
    return jnp.dot(x1, e, preferred_element_type=F32) + jnp.dot(x2, e, preferred_element_type=F32)


class _Cfg:
    def __init__(self, n_ctx, l_ctx, n_smp, l_smp, d_model, d_ff):
        self.n_ctx, self.l_ctx, self.n_smp, self.l_smp = n_ctx, l_ctx, n_smp, l_smp
        self.d, self.d_ff = d_model, d_ff
        self.t_ctx = n_ctx * l_ctx
        self.t_smp = n_smp * l_smp
        self.t = self.t_ctx + self.t_smp
        self.tb = SCAN_BLOCK
        self.c = SCAN_CHUNK
        assert l_ctx % self.tb == 0 and l_smp % self.tb == 0 and self.tb % self.c == 0
        assert l_ctx & (l_ctx - 1) == 0 and l_smp % GRID_W == 0
        self.nbc = l_ctx // self.tb
        self.nbs = l_smp // self.tb
        self.gc0 = self.t_ctx // self.tb
        self.nb = self.t // self.tb
        self.tm = 1024 if (self.t_ctx % 1024 == 0 and self.t_smp % 1024 == 0 and l_smp % 1024 == 0) else 256
        assert self.t_ctx % self.tm == 0 and l_smp % self.tm == 0 and self.tm % l_ctx == 0
        self.tm_merge = 512 if self.tm == 1024 else 256
        self.tm_ffn = 1024 if self.tm == 1024 else 256

    def mod_row(self, tm):
        nct = self.t_ctx // tm
        per = self.l_smp // tm

        def f(m):
            return jnp.where(m < nct, self.n_smp, jnp.maximum(m - nct, 0) // per)
        return f

    def blk(self, d, j):
        return j + d * (self.nb - 1 - 2 * j)

    def scan_flags(self, d, j):
        g = self.blk(d, j)
        is_ctx = g < self.gc0
        gs = jnp.maximum(g - self.gc0, 0)
        pos = jnp.where(is_ctx, lax.rem(g, self.nbc), lax.rem(gs, self.nbs))
        nblk = jnp.where(is_ctx, self.nbc, self.nbs)
        spos = pos + d * (nblk - 1 - 2 * pos)
        return is_ctx, spos == 0, spos == nblk - 1

    def smp_seq(self, d, j):
        return jnp.clip((self.blk(d, j) - self.gc0) // self.nbs, 0, self.n_smp - 1)

    def ctx_seq(self, d, j):
        return jnp.minimum(self.blk(d, j) // self.nbc, self.n_ctx - 1)


def _tri_consts(c):
    t = np.arange(c)
    fwd = (t[None, :] <= t[:, None]).astype(np.float32)
    return jnp.asarray(np.stack([fwd, fwd.T]), dtype=BF16)


def _expand_consts():
    e = np.zeros((2, LANES, S_W), np.float32)
    for dd in range(2):
        for h in range(S_HEADS):
            e[dd, GC_DTA + dd * S_HEADS + h, h * S_P:(h + 1) * S_P] = 1.0
    return jnp.asarray(e, dtype=BF16)


def _embed_kernel(xp_ref, xs_ref, pos_ref, o_ref, *, nct):
    m = pl.program_id(0)

    @pl.when(m < nct)
    def _():
        o_ref[...] = xp_ref[...]

    @pl.when(m >= nct)
    def _():
        o_ref[...] = xs_ref[...] + pos_ref[...]


def _embed(cfg, xp2, xs2, pos):
    tm = cfg.tm
    nct = cfg.t_ctx // tm
    per = cfg.l_smp // tm
    return pl.pallas_call(
        functools.partial(_embed_kernel, nct=nct),
        grid=(cfg.t // tm,),
        in_specs=[pl.BlockSpec((tm, cfg.d), lambda m: (jnp.minimum(m, nct - 1), 0)),
                  pl.BlockSpec((tm, cfg.d), lambda m: (jnp.maximum(m - nct, 0), 0)),
                  pl.BlockSpec((tm, cfg.d), lambda m: (lax.rem(jnp.maximum(m - nct, 0), per), 0))],
        out_specs=pl.BlockSpec((tm, cfg.d), lambda m: (m, 0)),
        out_shape=jax.ShapeDtypeStruct((cfg.t, cfg.d), F32),
        compiler_params=_cparams(("arbitrary",)),
        name="embed",
    )(xp2, xs2, pos)


def _ada_kernel(c_ref, w_ref, b_ref, o_ref):
    o_ref[0] = _dot(_silu(c_ref[...]), w_ref[0]) + b_ref[0]


def _ada(cvec, w_ada, b_ada):
    depth, d, n = w_ada.shape
    rows = cvec.shape[0]
    tn = 1536
    return pl.pallas_call(
        _ada_kernel,
        grid=(depth, n // tn),
        in_specs=[pl.BlockSpec((rows, d), lambda l, j: (0, 0)),
                  pl.BlockSpec((1, d, tn), lambda l, j: (l, 0, j)),
                  pl.BlockSpec((1, 1, tn), lambda l, j: (l, 0, j))],
        out_specs=pl.BlockSpec((1, rows, tn), lambda l, j: (l, 0, j)),
        out_shape=jax.ShapeDtypeStruct((depth, rows, n), F32),
        compiler_params=_cparams(("arbitrary", "arbitrary")),
        name="ada",
    )(cvec, w_ada, b_ada.reshape(depth, 1, n))


def _inproj_kernel(x_ref, sc_ref, sh_ref, nw_ref, w_ref, ws_ref, gb_ref, ga_ref, cw_ref, cb_ref,
                   p_ref, pf_ref, g_ref, gt_ref, bmt_ref, kt_ref, vt_ref, xn_ref, *, tm, tn, nct, seg_ctx):
    m = pl.program_id(0)
    n = pl.program_id(1)
    gn = S_GROUPS * S_N
    nt = N_MAIN // tn

    def prologue():
        x = x_ref[...]
        y = x * lax.rsqrt(jnp.mean(x * x, axis=-1, keepdims=True) + EPS) * nw_ref[...]
        h = (y * (1.0 + sc_ref[0]) + sh_ref[0]).astype(BF16)
        xn_ref[...] = h
        raw = jnp.dot(h, ws_ref[...], preferred_element_type=F32) + gb_ref[...]
        lane = lax.broadcasted_iota(jnp.int32, raw.shape, 1)
        sp = _softplus(raw)
        g = jnp.where(lane < GC_LF, raw,
                      jnp.where(lane < GC_DT, -_softplus(-raw),
                                jnp.where(lane < GC_DTA, sp, sp * ga_ref[...])))
        g_ref[...] = g
        gt_ref[...] = g.T

    def matmul_cols(c0, c1):
        return jnp.dot(xn_ref[...], w_ref[:, c0:c1], preferred_element_type=F32)

    def conv_tile(tile, ncols):
        row = lax.broadcasted_iota(jnp.int32, (tm, LANES), 0)
        seg_mask = jnp.where(m < nct, seg_ctx - 1, GRID_W - 1)
        inseg = row & seg_mask
        has_prev = inseg != 0
        has_next = inseg != seg_mask
        cw = cw_ref[0]
        if ncols < tn:
            p_ref[:, ncols:] = matmul_cols(ncols, tn).astype(p_ref.dtype)
        accs = [matmul_cols(c0, c0 + MXU_N) for c0 in range(0, ncols, MXU_N)]
        for c0 in range(0, ncols, MXU_N):
            acc = accs[c0 // MXU_N]
            for cb in range(MXU_N // LANES):
                lo = c0 + cb * LANES
                sl = slice(lo, lo + LANES)
                u = acc[:, cb * LANES:(cb + 1) * LANES]
                up = jnp.where(has_prev, pltpu.roll(u, 1, 0), 0.0)
                un = jnp.where(has_next, pltpu.roll(u, tm - 1, 0), 0.0)
                v = _silu(cw[0:1, sl] * up + cw[1:2, sl] * u + cw[2:3, sl] * un + cb_ref[0][:, sl])
                p_ref[:, sl] = v.astype(p_ref.dtype)
                col = tile * tn + lo - S_W
                if 0 <= col < gn:
                    bmt_ref[col:col + LANES, :] = v.T.astype(BF16)

    f32_groups = {}
    for k, col in enumerate((OFF_H, OFF_H + 3 * H_W)):
        tile, off = divmod(col, tn)
        assert off + 2 * H_W <= tn
        f32_groups[tile] = (k, off)

    def plain_tile(tile, transposed):
        acc = matmul_cols(0, tn)
        p_ref[...] = acc.astype(p_ref.dtype)
        if tile in f32_groups:
            k, off = f32_groups[tile]
            pf_ref[:, k * 2 * H_W:(k + 1) * 2 * H_W] = acc[:, off:off + 2 * H_W]
        for dst_ref, off in transposed:
            for cb in range(M_W // LANES):
                blk = acc[:, off + cb * LANES:off + (cb + 1) * LANES]
                dst_ref[cb * LANES:(cb + 1) * LANES, :] = blk.T.astype(BF16)

    n_conv = -(-CONV_CH // tn)
    special = {}
    for dst_ref, col in ((kt_ref, OFF_M + M_W), (vt_ref, OFF_H + H_W)):
        tile, off = divmod(col, tn)
        assert off + M_W <= tn and tile >= n_conv
        special.setdefault(tile, []).append((dst_ref, off))

    for tile in range(nt):
        @pl.when(n == tile)
        def _(tile=tile):
            if tile == 0:
                prologue()
            if tile < n_conv:
                conv_tile(tile, min(tn, CONV_CH - tile * tn))
            else:
                plain_tile(tile, special.get(tile, []))


def _inproj(cfg, x, mod, nw, w_main, w_small, gbias, gmul, cw, cb):
    tm, tn = cfg.tm, PROJ_TN
    nct = cfg.t_ctx // tm
    row = cfg.mod_row(tm)
    rows_pad = mod.shape[0] // 6
    nt = N_MAIN // tn
    kern = functools.partial(_inproj_kernel, tm=tm, tn=tn, nct=nct, seg_ctx=cfg.l_ctx)
    tspec = pl.BlockSpec((M_W, tm), lambda m, n: (0, m))
    tshape = jax.ShapeDtypeStruct((M_W, cfg.t), BF16)
    return pl.pallas_call(
        kern,
        grid=(cfg.t // tm, nt),
        in_specs=[pl.BlockSpec((tm, cfg.d), lambda m, n: (m, 0)),
                  pl.BlockSpec((1, 1, cfg.d), lambda m, n: (1 * rows_pad + row(m), 0, 0)),
                  pl.BlockSpec((1, 1, cfg.d), lambda m, n: (0 * rows_pad + row(m), 0, 0)),
                  pl.BlockSpec((1, cfg.d), lambda m, n: (0, 0)),
                  pl.BlockSpec((cfg.d, tn), lambda m, n: (0, n)),
                  pl.BlockSpec((cfg.d, LANES), lambda m, n: (0, 0)),
                  pl.BlockSpec((1, LANES), lambda m, n: (0, 0)),
                  pl.BlockSpec((1, LANES), lambda m, n: (0, 0)),
                  pl.BlockSpec((1, 3, tn), lambda m, n: (jnp.minimum(n, 1), 0, 0)),
                  pl.BlockSpec((1, 1, tn), lambda m, n: (jnp.minimum(n, 1), 0, 0))],
        out_specs=[pl.BlockSpec((tm, tn), lambda m, n: (m, n)),
                   pl.BlockSpec((tm, 4 * H_W), lambda m, n: (m, 0)),
                   pl.BlockSpec((tm, LANES), lambda m, n: (m, 0)),
                   pl.BlockSpec((LANES, tm), lambda m, n: (0, m)),
                   tspec, tspec, tspec],
        out_shape=[jax.ShapeDtypeStruct((cfg.t, N_MAIN), BF16),
                   jax.ShapeDtypeStruct((cfg.t, 4 * H_W), F32),
                   jax.ShapeDtypeStruct((cfg.t, LANES), F32),
                   jax.ShapeDtypeStruct((LANES, cfg.t), F32),
                   tshape, tshape, tshape],
        scratch_shapes=[pltpu.VMEM((tm, cfg.d), BF16)],
        compiler_params=_cparams(("arbitrary", "arbitrary")),
        name="inproj",
    )(x, mod, mod, nw, w_main, w_small, gbias, gmul, cw, cb)


def _bidir_kernel(*refs, cfg, n_dir, n_shared, n_state, n_scratch, run, load_state, store_state):
    it = iter(refs)
    take = lambda k: [next(it) for _ in range(k)]
    din = [take(n_dir), take(n_dir)]
    shared = take(n_shared)
    sin = [take(n_state), take(n_state)]
    yout = [take(1)[0], take(1)[0]]
    sout = [take(n_state), take(n_state)]
    scr = [take(n_scratch), take(n_scratch)]
    j = pl.program_id(0)
    flags = [cfg.scan_flags(d, j) for d in (0, 1)]

    for d in (0, 1):
        is_ctx, first, _ = flags[d]

        @pl.when(jnp.logical_and(first, is_ctx))
        def _(d=d):
            for s in scr[d]:
                s[...] = jnp.zeros_like(s)

        @pl.when(jnp.logical_and(first, jnp.logical_not(is_ctx)))
        def _(d=d):
            load_state(sin[d], scr[d])

    run(shared, [(bool(d), din[d], yout[d], scr[d]) for d in (0, 1)])

    for d in (0, 1):
        is_ctx, _, last = flags[d]

        @pl.when(jnp.logical_and(last, is_ctx))
        def _(d=d):
            store_state(scr[d], sout[d])


def _bidir_call(cfg, name, y_width, dir_inputs, shared_inputs, state_inputs, state_shapes, scratch_shapes,
                run, load_state, store_state):
    tb = cfg.tb
    in_specs, args = [], []
    for d in (0, 1):
        for arr, bshape, ifn in dir_inputs:
            in_specs.append(pl.BlockSpec(bshape, lambda j, d=d, ifn=ifn: ifn(d, cfg.blk(d, j))))
            args.append(arr)
    for arr in shared_inputs:
        in_specs.append(pl.BlockSpec(arr.shape, lambda j, nd=arr.ndim: (0,) * nd))
        args.append(arr)
    for d in (0, 1):
        for arr, shp in zip(state_inputs, state_shapes):
            in_specs.append(pl.BlockSpec((None, None) + shp,
                                         lambda j, d=d, k=len(shp): (cfg.smp_seq(d, j), d) + (0,) * k))
            args.append(arr)
    out_specs, out_shape = [], []
    for d in (0, 1):
        out_specs.append(pl.BlockSpec((tb, y_width), lambda j, d=d: (cfg.blk(d, j), 0)))
        out_shape.append(jax.ShapeDtypeStruct((cfg.t, y_width), BF16))
    for d in (0, 1):
        for shp in state_shapes:
            out_specs.append(pl.BlockSpec((None,) + shp, lambda j, d=d, k=len(shp): (cfg.ctx_seq(d, j),) + (0,) * k))
            out_shape.append(jax.ShapeDtypeStruct((cfg.n_ctx,) + shp, F32))
    kern = functools.partial(_bidir_kernel, cfg=cfg, n_dir=len(dir_inputs), n_shared=len(shared_inputs),
                             n_state=len(state_inputs), n_scratch=len(scratch_shapes), run=run,
                             load_state=load_state, store_state=store_state)
    outs = pl.pallas_call(
        kern,
        grid=(cfg.nb,),
        in_specs=in_specs,
        out_specs=out_specs,
        out_shape=out_shape,
        scratch_shapes=[pltpu.VMEM(s, F32) for s in scratch_shapes] * 2,
        compiler_params=_cparams(("arbitrary",)),
        name=name,
    )(*args)
    ns = len(state_shapes)
    finals = [(outs[2 + k], outs[2 + ns + k]) for k in range(ns)]
    return outs[0], outs[1], finals


def _chunk_rows(cfg, rev):
    nch = cfg.tb // cfg.c
    return [(nch - 1 - ci if rev else ci) * cfg.c for ci in range(nch)]


def _scan_masks(c, rev):
    tpos = lax.broadcasted_iota(jnp.int32, (c, c), 0)
    spos = lax.broadcasted_iota(jnp.int32, (c, c), 1)
    return tpos, spos, ((spos >= tpos) if rev else (spos <= tpos))


def _mlstm_run(cfg, shared, sides):
    (tri_ref,) = shared
    c = cfg.c
    unit = (lax.broadcasted_iota(jnp.int32, (c, LANES), 1) == 0).astype(F32)
    nch = cfg.tb // c
    heads = range(M_HEADS)

    row = lax.broadcasted_iota(jnp.int32, (c, LANES), 0)
    pre = {}
    for si, (rev, din, _, _) in enumerate(sides):
        g_ref, gt_ref = din[4], din[5]
        sel = tri_ref[int(rev)]
        for ci, r0 in enumerate(_chunk_rows(cfg, rev)):
            gc = g_ref[r0:r0 + c, :]
            gtc = gt_ref[:, r0:r0 + c]
            bcol = _sum01(sel, gc)
            brow = _sum01_nt(gtc[GC_LF:GC_LF + 8], sel)
            ba = pltpu.roll(bcol, LANES - (GC_LF - GC_IG), 1)
            cm = gc - ba
            k = 1
            while k < c:
                if rev:
                    cm = jnp.maximum(cm, jnp.where(row < c - k, pltpu.roll(cm, c - k, 0), NEG_BIG))
                else:
                    cm = jnp.maximum(cm, jnp.where(row >= k, pltpu.roll(cm, k, 0), NEG_BIG))
                k *= 2
            pre[si, ci] = dict(r0=r0, ba=ba, cm=cm,
                               urow=gtc[GC_IG:GC_IG + 8] - brow)
    probs = [(si, ci, h) for si in range(len(sides)) for ci in range(nch) for h in heads]
    st = {}
    for si, ci, h in probs:
        q_ref, k_ref = sides[si][1][0], sides[si][1][1]
        r0 = pre[si, ci]["r0"]
        hs = slice(h * M_DH, (h + 1) * M_DH)
        q = q_ref[r0:r0 + c, hs]
        st[si, ci, h] = dict(q=q, qk=_dot_nt(q, k_ref[r0:r0 + c, hs]))
    masks = [_scan_masks(c, rev)[2] for rev, _, _, _ in sides]
    lane1 = lax.broadcasted_iota(jnp.int32, (1, LANES), 1)
    log_kscale = float(np.log(M_DH ** -0.5))

    for ci in range(nch):
        step = [(si, h) for si in range(len(sides)) for h in heads]
        col_q = {}
        for si, (rev, _, _, (_, m_sc)) in enumerate(sides):
            p = pre[si, ci]
            end = 0 if rev else c - 1
            mrow = jnp.zeros((1, LANES), F32)
            for h in heads:
                mrow = jnp.where(lane1 == int(rev) * M_HEADS + h, m_sc[h], mrow)
            z = jnp.maximum(p["cm"], mrow)
            mt = p["ba"] + z
            m_end = mt[end:end + 1, :]
            b_end = p["ba"][end:end + 1, :]
            col_q[si] = dict(z=z, wi=jnp.exp(mrow - z), emt=jnp.exp(-mt), m_end=m_end,
                             bm=b_end - m_end, carry=jnp.exp(b_end + mrow - m_end))
        for si, h in step:
            e, cq = st[si, ci, h], col_q[si]
            col = int(sides[si][0]) * M_HEADS + h
            e["u_r"] = pre[si, ci]["urow"][col:col + 1, :]
            s = e["qk"] * jnp.exp(jnp.where(masks[si], (e["u_r"] + log_kscale) - cq["z"][:, col:col + 1], NEG_BIG))
            qi = e["q"].astype(F32) * cq["wi"][:, col:col + 1]
            e["lhs"] = jnp.concatenate([s, qi], axis=1).astype(BF16)
        for si, h in step:
            e = st[si, ci, h]
            v_ref, c_sc = sides[si][1][2], sides[si][3][0]
            r0 = pre[si, ci]["r0"]
            hs = slice(h * M_DH, (h + 1) * M_DH)
            e["vaug"] = jnp.concatenate([v_ref[r0:r0 + c, hs].astype(BF16), unit.astype(BF16)], axis=1)
            e["caug"] = c_sc[h]
            rhs = jnp.concatenate([e["vaug"], e["caug"].astype(BF16)], axis=0)
            e["numden"] = jnp.dot(e["lhs"], rhs, preferred_element_type=F32)
        for si, h in step:
            e = st[si, ci, h]
            rev, din, y_ref, _ = sides[si]
            r0 = pre[si, ci]["r0"]
            hs = slice(h * M_DH, (h + 1) * M_DH)
            cq = col_q[si]
            col = int(rev) * M_HEADS + h
            den = e["numden"][:, M_DH:M_DH + 1]
            hc = e["numden"][:, :M_DH] / jnp.maximum(jnp.abs(den), cq["emt"][:, col:col + 1])
            y_ref[r0:r0 + c, hs] = hc.astype(y_ref.dtype)
            wk = jnp.exp(e["u_r"] + cq["bm"][:, col:col + 1]) * (M_DH ** -0.5)
            e["ktw"] = (din[3][hs, r0:r0 + c].astype(F32) * wk).astype(BF16)
        for si, h in step:
            e, cq = st[si, ci, h], col_q[si]
            c_sc, m_sc = sides[si][3]
            col = int(sides[si][0]) * M_HEADS + h
            c_sc[h] = (cq["carry"][:, col:col + 1] * e["caug"]
                       + jnp.dot(e["ktw"], e["vaug"], preferred_element_type=F32))
            m_sc[h] = jnp.broadcast_to(cq["m_end"][:, col:col + 1], (1, LANES))


def _copy_state(src, dst):
    for s, d in zip(src, dst):
        d[...] = s[...]


def _mlstm(cfg, p, kt, g, gt, tri, c0, m0):
    tb = cfg.tb
    qi = OFF_M // M_W
    dir_inputs = [(p, (tb, M_W), lambda d, b: (b, qi)),
                  (p, (tb, M_W), lambda d, b: (b, qi + 1)),
                  (p, (tb, M_W), lambda d, b: (b, qi + 2)),
                  (kt, (M_W, tb), lambda d, b: (0, b)),
                  (g, (tb, LANES), lambda d, b: (b, 0)),
                  (gt, (LANES, tb), lambda d, b: (0, b))]
    shapes = [(M_HEADS, M_DH, 2 * M_DH), (M_HEADS, 1, LANES)]
    return _bidir_call(cfg, "mlstm", M_W, dir_inputs, [tri], [c0, m0], shapes, shapes,
                       functools.partial(_mlstm_run, cfg), _copy_state, _copy_state)


def _mid_rows(g, lev, rev):
    c, w = g.shape
    half = 1 << lev
    blk = 2 * half
    m = half if rev else half - 1
    if blk >= 8:
        g3 = g.reshape(c // blk, blk, w)
        return jnp.broadcast_to(g3[:, m:m + 1, :], g3.shape).reshape(c, w)
    g3 = g.reshape(c // 8, 8, w)
    sub = lax.broadcasted_iota(jnp.int32, g3.shape, 1)
    out = jnp.broadcast_to(g3[:, m:m + 1, :], g3.shape)
    for b0 in range(blk, 8, blk):
        out = jnp.where(sub >= b0, jnp.broadcast_to(g3[:, b0 + m:b0 + m + 1, :], g3.shape), out)
    return out.reshape(c, w)


def _hgrn_run(cfg, shared, sides):
    (tri_ref,) = shared
    c = cfg.c
    levels = int(np.log2(c))
    ones = jnp.ones((H_DK, LANES), BF16)
    nch = cfg.tb // c
    heads = range(H_HEADS)
    dn = (((1,), (1,)), ((), ()))

    pre, pair = {}, {}
    for si, (rev, din, _, _) in enumerate(sides):
        f_ref, lb_ref = din[3], din[4]
        sel = tri_ref[int(rev)]
        lb = lb_ref[0]
        tpos, spos, _ = _scan_masks(c, rev)
        before = (spos > tpos) if rev else (spos < tpos)
        lvl = jnp.where(before, tpos ^ spos, 0)
        pair[si] = [(lvl >> lev) == 1 for lev in range(levels)]
        for ci, r0 in enumerate(_chunk_rows(cfg, rev)):
            f = lb + (1.0 - lb) * _sigmoid(f_ref[r0:r0 + c, :])
            pre[si, ci] = dict(r0=r0, kk=1.0 - f, g=_sum01(sel, jnp.log(f)))
    probs = [(si, ci, h) for si in range(len(sides)) for ci in range(nch) for h in heads]
    st = {}
    for si, ci, h in probs:
        p = pre[si, ci]
        hs = slice(h * H_DK, (h + 1) * H_DK)
        q = sides[si][1][0][p["r0"]:p["r0"] + c, hs]
        kk = p["kk"][:, hs]
        st[si, ci, h] = dict(q=q, kk=kk, qb=q.astype(BF16), kb=kk.astype(BF16), g=p["g"][:, hs],
                             att=jnp.zeros((c, c), F32))
    for lev in range(levels):
        for si, ci, h in probs:
            e = st[si, ci, h]
            x = jnp.exp(-jnp.abs(e["g"] - _mid_rows(e["g"], lev, sides[si][0]))).astype(BF16)
            a = lax.dot_general(e["qb"] * x, e["kb"] * x, dn, preferred_element_type=F32)
            e["att"] = jnp.where(pair[si][lev], a, e["att"])
    for si, ci, h in probs:
        e = st[si, ci, h]
        e["diag"] = jnp.dot((e["q"] * e["kk"]).astype(BF16), ones, preferred_element_type=F32)

    for ci in range(nch):
        step = [(si, h) for si in range(len(sides)) for h in heads]
        for si, h in step:
            e = st[si, ci, h]
            rev, din, y_ref, (s_sc,) = sides[si]
            r0 = pre[si, ci]["r0"]
            hs = slice(h * H_DK, (h + 1) * H_DK)
            v = din[1][r0:r0 + c, hs]
            e["st"] = s_sc[h]
            o = _dot(e["att"], v) + e["diag"] * v + _dot_nt(e["q"] * jnp.exp(e["g"]), e["st"])
            y_ref[r0:r0 + c, hs] = o.astype(y_ref.dtype)
        for si, h in step:
            e = st[si, ci, h]
            rev, din, _, (s_sc,) = sides[si]
            r0 = pre[si, ci]["r0"]
            hs = slice(h * H_DK, (h + 1) * H_DK)
            end = 0 if rev else c - 1
            g_end = e["g"][end:end + 1, :]
            vt = din[2][hs, r0:r0 + c]
            s_sc[h] = jnp.exp(g_end) * e["st"] + _dot(vt, e["kk"] * jnp.exp(g_end - e["g"]))


def _hgrn_load(sin, scr):
    for h in range(H_HEADS):
        scr[0][h] = sin[0][h].T


def _hgrn_store(scr, sout):
    for h in range(H_HEADS):
        sout[0][h] = scr[0][h].T


def _hgrn(cfg, pf, vt, lb, tri, s0):
    tb = cfg.tb
    dir_inputs = [(pf, (tb, H_W), lambda d, b: (b, 0)),
                  (pf, (tb, H_W), lambda d, b: (b, 1)),
                  (vt, (H_W, tb), lambda d, b: (0, b)),
                  (pf, (tb, H_W), lambda d, b: (b, 2 + d)),
                  (lb, (1, 1, H_W), lambda d, b: (d, 0, 0))]
    return _bidir_call(cfg, "hgrn", H_W, dir_inputs, [tri], [s0], [(H_HEADS, H_DK, H_DV)],
                       [(H_HEADS, H_DV, H_DK)], functools.partial(_hgrn_run, cfg), _hgrn_load, _hgrn_store)


def _ssd_run(cfg, shared, sides):
    tri_ref, ex_ref, sd_ref = shared
    c = cfg.c
    n_rep = S_HEADS // S_GROUPS
    gw = n_rep * S_P
    nch = cfg.tb // c
    lane = lax.broadcasted_iota(jnp.int32, (c, LANES), 1)
    head_of_lane = lax.broadcasted_iota(jnp.int32, (c, gw), 1) // S_P
    in_head = [head_of_lane == r for r in range(n_rep)]

    rows = {(si, ci): r0 for si, (rev, _, _, _) in enumerate(sides)
            for ci, r0 in enumerate(_chunk_rows(cfg, rev))}
    probs = [(si, ci, grp) for si in range(len(sides)) for ci in range(nch) for grp in range(S_GROUPS)]
    st = {}
    for si, ci, grp in probs:
        cm_ref, bmt_ref = sides[si][1][1], sides[si][1][2]
        r0 = rows[si, ci]
        cmg = cm_ref[r0:r0 + c, grp * S_N:(grp + 1) * S_N].astype(BF16)
        bmt = bmt_ref[grp * S_N:(grp + 1) * S_N, r0:r0 + c]
        st[si, ci, grp] = dict(cmg=cmg, bmt=bmt, cb=jnp.dot(cmg, bmt, preferred_element_type=F32))
    pre, mask = {}, {}
    for si, (rev, din, _, _) in enumerate(sides):
        x_ref, g_ref, gt_ref = din[0], din[3], din[4]
        dd = int(rev)
        sel = tri_ref[dd]
        mask[si] = _scan_masks(c, rev)[2]
        la0 = GC_DTA + dd * S_HEADS
        is_la = (lane - la0).astype(jnp.uint32) < S_HEADS
        for ci in range(nch):
            r0 = rows[si, ci]
            gc = g_ref[r0:r0 + c, :]
            gtc = gt_ref[:, r0:r0 + c]
            pre[si, ci] = dict(r0=r0, la0=la0, gc=gc, xall=x_ref[r0:r0 + c, :].astype(F32),
                               lam=jnp.where(is_la, _sum01(sel, gc), 0.0),
                               larow=_sum01_nt(gtc[la0:la0 + S_HEADS], sel),
                               dtrow=gtc[GC_DT + dd * S_HEADS:GC_DT + (dd + 1) * S_HEADS])

    def expansions(si, ci):
        rev = sides[si][0]
        p = pre[si, ci]
        ex = ex_ref[int(rev)]
        end = 0 if rev else c - 1
        lam = p["lam"]
        ela = jnp.exp(lam)
        p["ea"] = jnp.dot(ela.astype(BF16), ex, preferred_element_type=F32)
        e8 = 0 if rev else c - 8
        p["ea_end"] = _expand2(ela[e8:e8 + 8, :], ex)[end - e8:end - e8 + 1, :]
        wk = jnp.exp(lam[end:end + 1, :] - lam) * pltpu.roll(p["gc"], GC_DTA - GC_DT, 1)
        ewk = jnp.dot(wk.astype(BF16), ex, preferred_element_type=F32)
        p["xw"] = (p["xall"] * ewk).astype(BF16)
    for si, ci, grp in probs:
        e, p = st[si, ci, grp], pre[si, ci]
        xg = p["xall"][:, grp * gw:(grp + 1) * gw]
        lhs, rhs = [], []
        for r in range(n_rep):
            h = grp * n_rep + r
            la_c = p["lam"][:, p["la0"] + h:p["la0"] + h + 1]
            seg = jnp.exp(jnp.where(mask[si], la_c - p["larow"][h:h + 1, :], NEG_BIG)) * p["dtrow"][h:h + 1, :]
            lhs.append((seg * e["cb"]).astype(BF16))
            rhs.append(jnp.where(in_head[r], xg, 0.0).astype(BF16))
        y = jnp.dot(jnp.concatenate(lhs, axis=1), jnp.concatenate(rhs, axis=0), preferred_element_type=F32)
        if not sides[si][0]:
            y = y + sd_ref[:, grp * gw:(grp + 1) * gw] * xg
        e["y"] = y
    for si, ci in rows:
        expansions(si, ci)

    for ci in range(nch):
        step = [(si, grp) for si in range(len(sides)) for grp in range(S_GROUPS)]
        for si, grp in step:
            e, p = st[si, ci, grp], pre[si, ci]
            _, _, y_ref, (h_sc,) = sides[si]
            gs = slice(grp * gw, (grp + 1) * gw)
            e["ht"] = h_sc[grp]
            y = e["y"] + p["ea"][:, gs] * jnp.dot(e["cmg"], e["ht"].astype(BF16), preferred_element_type=F32)
            y_ref[p["r0"]:p["r0"] + c, gs] = y.astype(y_ref.dtype)
        for si, grp in step:
            e, p = st[si, ci, grp], pre[si, ci]
            (h_sc,) = sides[si][3]
            gs = slice(grp * gw, (grp + 1) * gw)
            h_sc[grp] = p["ea_end"][:, gs] * e["ht"] + jnp.dot(e["bmt"], p["xw"][:, gs],
                                                               preferred_element_type=F32)


def _ssd_load(sin, scr):
    n_rep = S_HEADS // S_GROUPS
    for grp in range(S_GROUPS):
        scr[0][grp] = sin[0][grp * n_rep:(grp + 1) * n_rep].reshape(n_rep * S_P, S_N).T


def _ssd_store(scr, sout):
    n_rep = S_HEADS // S_GROUPS
    for grp in range(S_GROUPS):
        sout[0][grp * n_rep:(grp + 1) * n_rep] = scr[0][grp].T.reshape(n_rep, S_P, S_N)


def _ssd(cfg, p, bmt, g, gt, tri, ex, sd, h0):
    tb = cfg.tb
    gn = S_GROUPS * S_N
    dir_inputs = [(p, (tb, S_W), lambda d, b: (b, OFF_XBC // S_W)),
                  (p, (tb, gn), lambda d, b: (b, (OFF_XBC + S_W) // gn + 1)),
                  (bmt, (gn, tb), lambda d, b: (0, b)),
                  (g, (tb, LANES), lambda d, b: (b, 0)),
                  (gt, (LANES, tb), lambda d, b: (0, b))]
    return _bidir_call(cfg, "ssd", S_W, dir_inputs, [tri, ex, sd], [h0], [(S_HEADS, S_P, S_N)],
                       [(S_GROUPS, S_N, (S_HEADS // S_GROUPS) * S_P)], functools.partial(_ssd_run, cfg),
                       _ssd_load, _ssd_store)


def _head_norm(y, heads, width):
    outs = []
    for h in range(heads):
        s = y[:, h * width:(h + 1) * width]
        outs.append(s * lax.rsqrt(jnp.mean(s * s, axis=-1, keepdims=True) + EPS))
    return jnp.concatenate(outs, axis=1)


def _merge_kernel(x_ref, gate_ref, mo_ref, hg_ref, sz_ref, bg_ref, ymf_ref, ymb_ref, yhf_ref, yhb_ref,
                  ysf_ref, ysb_ref, mn_ref, hn_ref, sn_ref, wbm_ref, wbh_ref, wbs_ref, wo_ref, o_ref,
                  *, d_model):
    add = lambda a, b: a[...].astype(F32) + b[...].astype(F32)
    f32 = lambda r: r[...].astype(F32)
    y_m = _head_norm(add(ymf_ref, ymb_ref), M_HEADS, M_DH) * mn_ref[...] * _sigmoid(f32(mo_ref))
    y_h = _head_norm(add(yhf_ref, yhb_ref), H_HEADS, H_DV) * hn_ref[...] * _silu(f32(hg_ref))
    ys = add(ysf_ref, ysb_ref) * _silu(f32(sz_ref))
    y_s = ys * lax.rsqrt(jnp.mean(ys * ys, axis=-1, keepdims=True) + EPS) * sn_ref[...]
    bg = f32(bg_ref)
    dm = d_model
    merged = (_sigmoid(bg[:, :dm]) * jnp.dot(y_m.astype(BF16), wbm_ref[...], preferred_element_type=F32)
              + _sigmoid(bg[:, dm:2 * dm]) * jnp.dot(y_h.astype(BF16), wbh_ref[...], preferred_element_type=F32)
              + _sigmoid(bg[:, 2 * dm:]) * jnp.dot(y_s.astype(BF16), wbs_ref[...], preferred_element_type=F32))
    out = jnp.dot(merged.astype(BF16), wo_ref[...], preferred_element_type=F32)
    o_ref[...] = x_ref[...] + gate_ref[0] * out


def _merge(cfg, x, mod, p, ym, yh, ys, mn, hn, sn, wbm, wbh, wbs, wo):
    tm = cfg.tm_merge
    row = cfg.mod_row(tm)
    rows_pad = mod.shape[0] // 6
    d = cfg.d
    const = lambda shape: pl.BlockSpec(shape, lambda m: (0,) * len(shape))
    tok = lambda w: pl.BlockSpec((tm, w), lambda m: (m, 0))
    return pl.pallas_call(
        functools.partial(_merge_kernel, d_model=d),
        grid=(cfg.t // tm,),
        in_specs=[tok(d),
                  pl.BlockSpec((1, 1, d), lambda m: (2 * rows_pad + row(m), 0, 0)),
                  pl.BlockSpec((tm, M_W), lambda m: (m, (OFF_M + 3 * M_W) // M_W)),
                  pl.BlockSpec((tm, H_W), lambda m: (m, (OFF_H + 2 * H_W) // H_W)),
                  pl.BlockSpec((tm, S_W), lambda m: (m, OFF_SZ // S_W)),
                  pl.BlockSpec((tm, 3 * d), lambda m: (m, OFF_BG // (3 * d))),
                  tok(M_W), tok(M_W), tok(H_W), tok(H_W), tok(S_W), tok(S_W),
                  const((1, M_W)), const((1, H_W)), const((1, S_W)),
                  const((M_W, d)), const((H_W, d)), const((S_W, d)), const((d, d))],
        out_specs=tok(d),
        out_shape=jax.ShapeDtypeStruct((cfg.t, d), F32),
        compiler_params=_cparams(("arbitrary",)),
        name="merge",
    )(x, mod, p, p, p, p, ym[0], ym[1], yh[0], yh[1], ys[0], ys[1], mn, hn, sn, wbm, wbh, wbs, wo)


def _ffn_up_kernel(x_ref, sc_ref, sh_ref, nw_ref, wa_ref, wb_ref, o_ref, *, d_ff, tc):
    x = x_ref[...]
    y = x * lax.rsqrt(jnp.mean(x * x, axis=-1, keepdims=True) + EPS) * nw_ref[...]
    h = (y * (1.0 + sc_ref[0]) + sh_ref[0]).astype(BF16)
    for j in range(d_ff // tc):
        sl = slice(j * tc, (j + 1) * tc)
        a = jnp.dot(h, wa_ref[:, sl], preferred_element_type=F32)
        b = jnp.dot(h, wb_ref[:, sl], preferred_element_type=F32)
        o_ref[:, sl] = (_silu(a) * b).astype(BF16)


def _ffn_up(cfg, x, mod, nw, w_gu):
    tm = cfg.tm_ffn
    row = cfg.mod_row(tm)
    rows_pad = mod.shape[0] // 6
    d, d_ff = cfg.d, cfg.d_ff
    return pl.pallas_call(
        functools.partial(_ffn_up_kernel, d_ff=d_ff, tc=MXU_N),
        grid=(cfg.t // tm,),
        in_specs=[pl.BlockSpec((tm, d), lambda m: (m, 0)),
                  pl.BlockSpec((1, 1, d), lambda m: (4 * rows_pad + row(m), 0, 0)),
                  pl.BlockSpec((1, 1, d), lambda m: (3 * rows_pad + row(m), 0, 0)),
                  pl.BlockSpec((1, d), lambda m: (0, 0)),
                  pl.BlockSpec((d, d_ff), lambda m: (0, 0)),
                  pl.BlockSpec((d, d_ff), lambda m: (0, 1))],
        out_specs=pl.BlockSpec((tm, d_ff), lambda m: (m, 0)),
        out_shape=jax.ShapeDtypeStruct((cfg.t, d_ff), BF16),
        compiler_params=_cparams(("arbitrary",)),
        name="ffn_up",
    )(x, mod, mod, nw, w_gu, w_gu)


def _ffn_down_kernel(x_ref, gate_ref, a_ref, w_ref, o_ref):
    o_ref[...] = x_ref[...] + gate_ref[0] * jnp.dot(a_ref[...], w_ref[...], preferred_element_type=F32)


def _ffn_down_final_kernel(x_ref, gate_ref, a_ref, w_ref, nf_ref, op_ref, os_ref, *, nct):
    m = pl.program_id(0)
    x = x_ref[...] + gate_ref[0] * jnp.dot(a_ref[...], w_ref[...], preferred_element_type=F32)
    y = x * lax.rsqrt(jnp.mean(x * x, axis=-1, keepdims=True) + EPS) * nf_ref[...]

    @pl.when(m < nct)
    def _():
        op_ref[...] = y

    @pl.when(m >= nct)
    def _():
        os_ref[...] = y


def _ffn_down(cfg, x, mod, act, w_down, norm_f=None):
    tm = cfg.tm_ffn
    row = cfg.mod_row(tm)
    rows_pad = mod.shape[0] // 6
    d, d_ff = cfg.d, cfg.d_ff
    in_specs = [pl.BlockSpec((tm, d), lambda m: (m, 0)),
                pl.BlockSpec((1, 1, d), lambda m: (5 * rows_pad + row(m), 0, 0)),
                pl.BlockSpec((tm, d_ff), lambda m: (m, 0)),
                pl.BlockSpec((d_ff, d), lambda m: (0, 0))]
    if norm_f is None:
        return pl.pallas_call(
            _ffn_down_kernel,
            grid=(cfg.t // tm,),
            in_specs=in_specs,
            out_specs=pl.BlockSpec((tm, d), lambda m: (m, 0)),
            out_shape=jax.ShapeDtypeStruct((cfg.t, d), F32),
            compiler_params=_cparams(("arbitrary",)),
            name="ffn_down",
        )(x, mod, act, w_down)
    nct = cfg.t_ctx // tm
    return pl.pallas_call(
        functools.partial(_ffn_down_final_kernel, nct=nct),
        grid=(cfg.t // tm,),
        in_specs=in_specs + [pl.BlockSpec((1, d), lambda m: (0, 0))],
        out_specs=[pl.BlockSpec((tm, d), lambda m: (jnp.minimum(m, nct - 1), 0)),
                   pl.BlockSpec((tm, d), lambda m: (jnp.maximum(m - nct, 0), 0))],
        out_shape=[jax.ShapeDtypeStruct((cfg.t_ctx, d), F32),
                   jax.ShapeDtypeStruct((cfg.t_smp, d), F32)],
        compiler_params=_cparams(("arbitrary",)),
        name="ffn_down_final",
    )(x, mod, act, w_down, norm_f)


def _ffn_fused_kernel(x_ref, sc_ref, sh_ref, gate_ref, nw_ref, wa_ref, wb_ref, wd_ref, *rest, d_ff, tc, nct):
    final = nct is not None
    if final:
        nf_ref, op_ref, os_ref, act_ref = rest
    else:
        o_ref, act_ref = rest
    x = x_ref[...]
    y = x * lax.rsqrt(jnp.mean(x * x, axis=-1, keepdims=True) + EPS) * nw_ref[...]
    h = (y * (1.0 + sc_ref[0]) + sh_ref[0]).astype(BF16)
    for j in range(d_ff // tc):
        sl = slice(j * tc, (j + 1) * tc)
        a = jnp.dot(h, wa_ref[:, sl], preferred_element_type=F32)
        b = jnp.dot(h, wb_ref[:, sl], preferred_element_type=F32)
        act_ref[:, sl] = (_silu(a) * b).astype(BF16)
    xo = x + gate_ref[0] * jnp.dot(act_ref[...], wd_ref[...], preferred_element_type=F32)
    if not final:
        o_ref[...] = xo
        return
    m = pl.program_id(0)
    yo = xo * lax.rsqrt(jnp.mean(xo * xo, axis=-1, keepdims=True) + EPS) * nf_ref[...]

    @pl.when(m < nct)
    def _():
        op_ref[...] = yo

    @pl.when(m >= nct)
    def _():
        os_ref[...] = yo


def _ffn_fused(cfg, x, mod, nw, w_gu, w_down, norm_f=None):
    tm = 512 if cfg.tm == 1024 else 256
    row = cfg.mod_row(tm)
    rows_pad = mod.shape[0] // 6
    d, d_ff = cfg.d, cfg.d_ff
    once = dict(pipeline_mode=pl.Buffered(1))
    in_specs = [pl.BlockSpec((tm, d), lambda m: (m, 0)),
                pl.BlockSpec((1, 1, d), lambda m: (4 * rows_pad + row(m), 0, 0)),
                pl.BlockSpec((1, 1, d), lambda m: (3 * rows_pad + row(m), 0, 0)),
                pl.BlockSpec((1, 1, d), lambda m: (5 * rows_pad + row(m), 0, 0)),
                pl.BlockSpec((1, d), lambda m: (0, 0)),
                pl.BlockSpec((d, d_ff), lambda m: (0, 0), **once),
                pl.BlockSpec((d, d_ff), lambda m: (0, 1), **once),
                pl.BlockSpec((d_ff, d), lambda m: (0, 0), **once)]
    args = [x, mod, mod, mod, nw, w_gu, w_gu, w_down]
    final = norm_f is not None
    nct = cfg.t_ctx // tm if final else None
    if final:
        in_specs.append(pl.BlockSpec((1, d), lambda m: (0, 0)))
        args.append(norm_f)
        out_specs = [pl.BlockSpec((tm, d), lambda m: (jnp.minimum(m, nct - 1), 0)),
                     pl.BlockSpec((tm, d), lambda m: (jnp.maximum(m - nct, 0), 0))]
        out_shape = [jax.ShapeDtypeStruct((cfg.t_ctx, d), F32), jax.ShapeDtypeStruct((cfg.t_smp, d), F32)]
    else:
        out_specs = pl.BlockSpec((tm, d), lambda m: (m, 0))
        out_shape = jax.ShapeDtypeStruct((cfg.t, d), F32)
    return pl.pallas_call(
        functools.partial(_ffn_fused_kernel, d_ff=d_ff, tc=MXU_N, nct=nct),
        grid=(cfg.t // tm,),
        in_specs=in_specs,
        out_specs=out_specs,
        out_shape=out_shape,
        scratch_shapes=[pltpu.VMEM((tm, d_ff), BF16)],
        compiler_params=_cparams(("arbitrary",)),
        name="ffn_final" if final else "ffn",
    )(*args)


def _grid_pos_embed(rows, d_model):
    quarter = d_model // 4
    freq = POS_BASE ** (-jnp.arange(quarter, dtype=F32) / quarter)
    r = jnp.arange(rows, dtype=F32)[:, None] * freq
    cl = jnp.arange(GRID_W, dtype=F32)[:, None] * freq
    row_e = jnp.concatenate([jnp.sin(r), jnp.cos(r)], axis=-1)
    col_e = jnp.concatenate([jnp.sin(cl), jnp.cos(cl)], axis=-1)
    emb = jnp.concatenate([jnp.broadcast_to(row_e[:, None], (rows, GRID_W, d_model // 2)),
                           jnp.broadcast_to(col_e[None], (rows, GRID_W, d_model // 2))], axis=-1)
    return emb.reshape(rows * GRID_W, d_model)


def kernel(x_prompt, x_sample, state_mlstm_c, state_mlstm_n, state_mlstm_m, state_hgrn, state_ssm, c, c_ctx, w_ada, b_ada, norm1, norm2, w_in, m_bi, m_bf, m_norm, h_lb, h_norm, s_conv_w, s_conv_b, s_dt_bias, s_a_log, s_d, s_norm, w_bm, w_bh, w_bs, w_out, w_gu, w_down, norm_f):
    n_ctx, l_ctx, d = x_prompt.shape
    n_smp, l_smp, _ = x_sample.shape
    depth = w_in.shape[0]
    d_ff = w_down.shape[1]
    cfg = _Cfg(n_ctx, l_ctx, n_smp, l_smp, d, d_ff)

    tri = _tri_consts(cfg.c)
    ex = _expand_consts()

    rows_pad = -(-(n_smp + 1) // 8) * 8
    cvec = jnp.zeros((rows_pad, d), F32).at[:n_smp].set(c).at[n_smp].set(c_ctx)
    mod_all = _ada(cvec, w_ada, b_ada)
    mod_all = mod_all.reshape(depth, rows_pad, 6, d).transpose(0, 2, 1, 3).reshape(depth, 6 * rows_pad, 1, d)

    p_lb = jax.nn.softmax(h_lb.astype(F32), axis=1)
    lower = jnp.cumsum(p_lb, axis=1) - p_lb[:, :1]

    pos = _grid_pos_embed(l_smp // GRID_W, d)
    x = _embed(cfg, x_prompt.reshape(cfg.t_ctx, d), x_sample.reshape(cfg.t_smp, d), pos)

    a_neg = -jnp.exp(s_a_log.astype(F32))
    new_c, new_n, new_m, new_h, new_s = [], [], [], [], []
    y_p = y_s = None
    for i in range(depth):
        w = w_in[i]
        o_mi, o_hq, o_sz, o_xbc, o_dt, o_bg = 2048, 2064, 4624, 5648, 7696, 7728
        w_main = jnp.concatenate([w[:, o_xbc:o_dt], w[:, o_sz:o_xbc], w[:, o_bg:], w[:, :o_mi],
                                  w[:, o_hq:o_sz]], axis=1).astype(BF16)
        w_small = jnp.concatenate([w[:, o_mi:o_hq], w[:, o_dt:o_bg], w[:, o_dt:o_bg],
                                   jnp.zeros((d, LANES - GC_DTA - 2 * S_HEADS), F32)], axis=1).astype(BF16)
        dtb = s_dt_bias[i].reshape(-1)
        gbias = jnp.concatenate([m_bi[i].reshape(-1), m_bf[i].reshape(-1), dtb, dtb,
                                 jnp.zeros((LANES - GC_DTA - 2 * S_HEADS,), F32)]).reshape(1, LANES)
        gmul = jnp.concatenate([jnp.ones((GC_DTA,), F32), a_neg[i].reshape(-1),
                                jnp.zeros((LANES - GC_DTA - 2 * S_HEADS,), F32)]).reshape(1, LANES)
        ncw = -(-CONV_CH // PROJ_TN) * PROJ_TN
        cw = jnp.zeros((3, ncw), F32).at[:, :CONV_CH].set(s_conv_w[i]).reshape(3, ncw // PROJ_TN, PROJ_TN)
        cw = cw.transpose(1, 0, 2)
        cb = jnp.zeros((ncw,), F32).at[:CONV_CH].set(s_conv_b[i]).reshape(ncw // PROJ_TN, 1, PROJ_TN)
        mod = mod_all[i]

        p, pf, g, gt, bmt, kt, vt = _inproj(cfg, x, mod, norm1[i].reshape(1, d), w_main, w_small, gbias, gmul,
                                            cw, cb)

        c0 = jnp.concatenate([state_mlstm_c[:, i], state_mlstm_n[:, i][..., None],
                              jnp.zeros(state_mlstm_c[:, i].shape[:-1] + (M_DH - 1,), F32)], axis=-1)
        m0 = jnp.broadcast_to(state_mlstm_m[:, i][..., None, None], (n_smp, 2, M_HEADS, 1, LANES))
        ymf, ymb, (c_fin, m_fin) = _mlstm(cfg, p, kt, g, gt, tri, c0, m0)
        yhf, yhb, (h_fin,) = _hgrn(cfg, pf, vt, lower[:, i].reshape(2, 1, H_W), tri, state_hgrn[:, i])
        sd = jnp.repeat(s_d[i].astype(F32), S_P).reshape(1, S_W)
        ysf, ysb, (s_fin,) = _ssd(cfg, p, bmt, g, gt, tri, ex, sd, state_ssm[:, i])

        x = _merge(cfg, x, mod, p, (ymf, ymb), (yhf, yhb), (ysf, ysb), m_norm[i].reshape(1, M_W),
                   h_norm[i].reshape(1, H_W), s_norm[i].reshape(1, S_W), w_bm[i].astype(BF16),
                   w_bh[i].astype(BF16), w_bs[i].astype(BF16), w_out[i].astype(BF16))
        ffn_args = (cfg, x, mod, norm2[i].reshape(1, d), w_gu[i].astype(BF16), w_down[i].astype(BF16))
        if i + 1 < depth:
            x = _ffn_fused(*ffn_args)
        else:
            y_p, y_s = _ffn_fused(*ffn_args, norm_f.reshape(1, d))

        for cd, md, hd, sd_ in zip(c_fin, m_fin, h_fin, s_fin):
            new_c.append(cd[..., :M_DH])
            new_n.append(cd[..., M_DH])
            new_m.append(md[..., 0, 0])
            new_h.append(hd)
            new_s.append(sd_)

    def layers_dirs(parts):
        a = jnp.stack(parts, axis=1)
        return a.reshape((n_ctx, depth, 2) + a.shape[2:])

    return (y_p.reshape(n_ctx, l_ctx, d), y_s.reshape(n_smp, l_smp, d),
            layers_dirs(new_c), layers_dirs(new_n), layers_dirs(new_m), layers_dirs(new_h), layers_dirs(new_s))
```
